```python
import math
import jax, jax.numpy as jnp
from jax import lax
import numpy as np

D_MODEL = 1024
BATCH = 4
SEQ = 4096
DEPTH = 2
DEC_BATCH = 32
DEC_SEQ = 8
PAST_LEN = 8192
PAGE_SIZE = 128

N_A_LAYERS = DEPTH // 2
N_B_LAYERS = DEPTH - N_A_LAYERS
CONV_WIDTH = 31
N_HEADS = 8
HEAD_DIM = 64
V_HEAD_DIM = 2 * HEAD_DIM
QK_WIDTH = N_HEADS * 2 * HEAD_DIM
V_WIDTH = N_HEADS * V_HEAD_DIM
D_FF = 4 * D_MODEL
N_BUCKETS = 32
MAX_DISTANCE = 128
Q_BLOCK = 128
NORM_EPS = 1e-6
NEG_INF = -1e30

kernel_name = 'yoco_conformer_conv_diff_attn_step'


def _rmsnorm(x, g):
    xf = x.astype(jnp.float32)
    y = xf * lax.rsqrt(jnp.mean(xf * xf, axis=-1, keepdims=True) + NORM_EPS)
    return (y * g.astype(jnp.float32)).astype(x.dtype)


def _layernorm(x, g, b):
    xf = x.astype(jnp.float32)
    mu = jnp.mean(xf, axis=-1, keepdims=True)
    var = jnp.mean(jnp.square(xf - mu), axis=-1, keepdims=True)
    y = (xf - mu) * lax.rsqrt(var + NORM_EPS)
    return (y * g.astype(jnp.float32) + b.astype(jnp.float32)).astype(x.dtype)


def _sq_relu_mlp(h, w1, w2):
    u = jnp.maximum(h @ w1, 0)
    return (u * u) @ w2


def _conv_module(h, conv_in, pw1_w, pw1_b, dw_w, dw_b, ln_g, ln_b, pw2_w, pw2_b):
    u = h @ pw1_w + pw1_b
    g = u[..., :D_MODEL] * jax.nn.sigmoid(u[..., D_MODEL:])
    ext = jnp.concatenate([conv_in.astype(g.dtype), g], axis=1)
    c = lax.conv_general_dilated(ext, dw_w[:, None, :].astype(ext.dtype), (1,), 'VALID',
                                 dimension_numbers=('NWC', 'WIO', 'NWC'),
                                 feature_group_count=D_MODEL) + dw_b
    c = jax.nn.silu(_layernorm(c, ln_g, ln_b))
    return c @ pw2_w + pw2_b, ext[:, -(CONV_WIDTH - 1):]


def _rel_bucket(q_pos, k_pos):
    n = jnp.maximum(q_pos[:, None] - k_pos[None, :], 0)
    max_exact = N_BUCKETS // 2
    nf = jnp.maximum(n, 1).astype(jnp.float32)
    large = max_exact + (jnp.log(nf / max_exact) / math.log(MAX_DISTANCE / max_exact)
                         * (N_BUCKETS - max_exact)).astype(jnp.int32)
    large = jnp.minimum(large, N_BUCKETS - 1)
    return jnp.where(n < max_exact, n, large)


def _diff_block(q, k, v, q_pos, k_pos, rel_table, lam):
    s = jnp.einsum('bqhcd,bkhcd->bhcqk', q, k).astype(jnp.float32) * (HEAD_DIM ** -0.5)
    bias = jnp.transpose(rel_table[_rel_bucket(q_pos, k_pos)], (2, 0, 1)).astype(jnp.float32)
    s = s + bias[None, :, None]
    causal = k_pos[None, :] <= q_pos[:, None]
    s = jnp.where(causal, s, NEG_INF)
    p = jax.nn.softmax(s, axis=-1)
    a = p[:, :, 0] - lam * p[:, :, 1]
    return jnp.einsum('bhqk,bkhe->bqhe', a.astype(v.dtype), v)


def _diff_attention(hn, k_all, v_all, pos0, lam_init, w_q, lq1, lk1, lq2, lk2, subln_g, w_o, rel_table):
    bsz, t, _ = hn.shape
    q = (hn @ w_q).reshape(bsz, t, N_HEADS, 2, HEAD_DIM)
    lam = (jnp.exp(jnp.sum(lq1.astype(jnp.float32) * lk1.astype(jnp.float32)))
           - jnp.exp(jnp.sum(lq2.astype(jnp.float32) * lk2.astype(jnp.float32))) + lam_init)
    q_pos = pos0 + jnp.arange(t, dtype=jnp.int32)
    k_pos = jnp.arange(k_all.shape[1], dtype=jnp.int32)
    if t > Q_BLOCK:
        nb = t // Q_BLOCK
        qb = q.reshape(bsz, nb, Q_BLOCK, N_HEADS, 2, HEAD_DIM).swapaxes(0, 1)
        pb = q_pos.reshape(nb, Q_BLOCK)
        o = lax.map(lambda qp: _diff_block(qp[0], k_all, v_all, qp[1], k_pos, rel_table, lam), (qb, pb))
        o = o.swapaxes(0, 1).reshape(bsz, t, N_HEADS, V_HEAD_DIM)
    else:
        o = _diff_block(q, k_all, v_all, q_pos, k_pos, rel_table, lam)
    o = _rmsnorm(o, subln_g) * (1.0 - lam_init)
    return o.reshape(bsz, t, V_WIDTH) @ w_o


def _trunk(x, conv_in, past_k, past_v, pos0, prm):
    bsz, t, _ = x.shape
    h = x
    conv_new = []
    k_new = None
    v_new = None
    k_all = None
    v_all = None
    for l in range(DEPTH):
        hn = _rmsnorm(h, prm['norm_mix_g'][l])
        if l < N_A_LAYERS:
            y, st = _conv_module(hn, conv_in[l], prm['conv_pw1_w'][l], prm['conv_pw1_b'][l],
                                 prm['conv_dw_w'][l], prm['conv_dw_b'][l], prm['conv_ln_g'][l],
                                 prm['conv_ln_b'][l], prm['conv_pw2_w'][l], prm['conv_pw2_b'][l])
            conv_new.append(st)
        else:
            if l == N_A_LAYERS:
                kvn = _rmsnorm(h, prm['kv_norm_g'])
                k_new = (kvn @ prm['w_k']).reshape(bsz, t, N_HEADS, 2, HEAD_DIM)
                v_new = (kvn @ prm['w_v']).reshape(bsz, t, N_HEADS, V_HEAD_DIM)
                if past_k is None:
                    k_all, v_all = k_new, v_new
                else:
                    k_all = jnp.concatenate([past_k.astype(k_new.dtype), k_new], axis=1)
                    v_all = jnp.concatenate([past_v.astype(v_new.dtype), v_new], axis=1)
            b = l - N_A_LAYERS
            lam_init = 0.8 - 0.6 * math.exp(-0.3 * l)
            y = _diff_attention(hn, k_all, v_all, pos0, lam_init, prm['w_q'][b],
                                prm['lambda_q1'][b], prm['lambda_k1'][b], prm['lambda_q2'][b],
                                prm['lambda_k2'][b], prm['subln_g'][b], prm['w_o'][b],
                                prm['rel_bias_table'])
        h = h + y
        h = h + _sq_relu_mlp(_rmsnorm(h, prm['norm_mlp_g'][l]), prm['mlp_w1'][l], prm['mlp_w2'][l])
    y_out = _rmsnorm(h, prm['final_norm_g'])
    return y_out, jnp.stack(conv_new), k_new.reshape(bsz, t, N_HEADS, 2 * HEAD_DIM), v_new


def setup_inputs(seed: int = 0) -> dict:
    key = jax.random.key(seed)
    ks = jax.random.split(key, 32)
    f32 = jnp.float32
    n_pages = PAST_LEN // PAGE_SIZE
    n_used = DEC_BATCH * n_pages
    n_pool = n_used + max(1, n_used // 4)

    def nrm(k, shape, scale):
        return jax.random.normal(k, shape, f32) * scale

    def gain(k, shape):
        return 1.0 + 0.02 * jax.random.normal(k, shape, f32)

    page_table = jax.random.permutation(ks[5], n_pool)[:n_used].reshape(DEC_BATCH, n_pages).astype(jnp.int32)
    return {
        'x_prompt': nrm(ks[0], (BATCH, SEQ, D_MODEL), 1.0),
        'x_sample': nrm(ks[1], (DEC_BATCH, DEC_SEQ, D_MODEL), 1.0),
        'state_conv': nrm(ks[2], (N_A_LAYERS, DEC_BATCH, CONV_WIDTH - 1, D_MODEL), 0.5),
        'cache_k': nrm(ks[3], (n_pool, PAGE_SIZE, N_HEADS, 2 * HEAD_DIM), 1.0),
        'cache_v': nrm(ks[4], (n_pool, PAGE_SIZE, N_HEADS, V_HEAD_DIM), 1.0),
        'page_table': page_table,
        'norm_mix_g': gain(ks[6], (DEPTH, D_MODEL)),
        'norm_mlp_g': gain(ks[7], (DEPTH, D_MODEL)),
        'conv_pw1_w': nrm(ks[8], (N_A_LAYERS, D_MODEL, 2 * D_MODEL), D_MODEL ** -0.5),
        'conv_pw1_b': nrm(ks[9], (N_A_LAYERS, 2 * D_MODEL), 0.02),
        'conv_dw_w': nrm(ks[10], (N_A_LAYERS, CONV_WIDTH, D_MODEL), CONV_WIDTH ** -0.5),
        'conv_dw_b': nrm(ks[11], (N_A_LAYERS, D_MODEL), 0.02),
        'conv_ln_g': gain(ks[12], (N_A_LAYERS, D_MODEL)),
        'conv_ln_b': nrm(ks[13], (N_A_LAYERS, D_MODEL), 0.02),
        'conv_pw2_w': nrm(ks[14], (N_A_LAYERS, D_MODEL, D_MODEL), D_MODEL ** -0.5),
        'conv_pw2_b': nrm(ks[15], (N_A_LAYERS, D_MODEL), 0.02),
        'kv_norm_g': gain(ks[16], (D_MODEL,)),
        'w_k': nrm(ks[17], (D_MODEL, QK_WIDTH), D_MODEL ** -0.5),
        'w_v': nrm(ks[18], (D_MODEL, V_WIDTH), D_MODEL ** -0.5),
        'w_q': nrm(ks[19], (N_B_LAYERS, D_MODEL, QK_WIDTH), D_MODEL ** -0.5),
        'lambda_q1': nrm(ks[20], (N_B_LAYERS, HEAD_DIM), 0.1),
        'lambda_k1': nrm(ks[21], (N_B_LAYERS, HEAD_DIM), 0.1),
        'lambda_q2': nrm(ks[22], (N_B_LAYERS, HEAD_DIM), 0.1),
        'lambda_k2': nrm(ks[23], (N_B_LAYERS, HEAD_DIM), 0.1),
        'subln_g': gain(ks[24], (N_B_LAYERS, V_HEAD_DIM)),
        'w_o': nrm(ks[25], (N_B_LAYERS, V_WIDTH, D_MODEL), V_WIDTH ** -0.5),
        'rel_bias_table': nrm(ks[26], (N_BUCKETS, N_HEADS), 0.3),
        'mlp_w1': nrm(ks[27], (DEPTH, D_MODEL, D_FF), D_MODEL ** -0.5),
        'mlp_w2': nrm(ks[28], (DEPTH, D_FF, D_MODEL), D_FF ** -0.5),
        'final_norm_g': gain(ks[29], (D_MODEL,)),
    }


def reference(x_prompt, x_sample, state_conv, cache_k, cache_v, page_table,
              norm_mix_g, norm_mlp_g, conv_pw1_w, conv_pw1_b, conv_dw_w, conv_dw_b,
              conv_ln_g, conv_ln_b, conv_pw2_w, conv_pw2_b, kv_norm_g, w_k, w_v, w_q,
              lambda_q1, lambda_k1, lambda_q2, lambda_k2, subln_g, w_o, rel_bias_table,
              mlp_w1, mlp_w2, final_norm_g):
    prm = {
        'norm_mix_g': norm_mix_g, 'norm_mlp_g': norm_mlp_g,
        'conv_pw1_w': conv_pw1_w, 'conv_pw1_b': conv_pw1_b, 'conv_dw_w': conv_dw_w,
        'conv_dw_b': conv_dw_b, 'conv_ln_g': conv_ln_g, 'conv_ln_b': conv_ln_b,
        'conv_pw2_w': conv_pw2_w, 'conv_pw2_b': conv_pw2_b, 'kv_norm_g': kv_norm_g,
        'w_k': w_k, 'w_v': w_v, 'w_q': w_q, 'lambda_q1': lambda_q1, 'lambda_k1': lambda_k1,
        'lambda_q2': lambda_q2, 'lambda_k2': lambda_k2, 'subln_g': subln_g, 'w_o': w_o,
        'rel_bias_table': rel_bias_table, 'mlp_w1': mlp_w1, 'mlp_w2': mlp_w2,
        'final_norm_g': final_norm_g,
    }
    conv_zero = jnp.zeros((N_A_LAYERS, x_prompt.shape[0], CONV_WIDTH - 1, D_MODEL), x_prompt.dtype)
    y_prompt, conv_state_prompt, k_prompt, v_prompt = _trunk(x_prompt, conv_zero, None, None, 0, prm)
    db, n_pages = page_table.shape
    past_len = n_pages * cache_k.shape[1]
    past_k = cache_k[page_table].reshape(db, past_len, N_HEADS, 2, HEAD_DIM)
    past_v = cache_v[page_table].reshape(db, past_len, N_HEADS, V_HEAD_DIM)
    y_sample, conv_state_sample, k_sample, v_sample = _trunk(x_sample, state_conv, past_k, past_v, past_len, prm)
    return (y_prompt, y_sample, conv_state_prompt, conv_state_sample, k_prompt, v_prompt, k_sample, v_sample)
```

```python
import functools
import math

import jax
import jax.numpy as jnp
from jax import lax
from jax.experimental import pallas as pl
from jax.experimental.pallas import tpu as pltpu

F32 = jnp.float32
BF16 = jnp.bfloat16

N_HEADS = 8
HEAD_DIM = 64
V_HEAD_DIM = 2 * HEAD_DIM
CONV_WIDTH = 31
N_BUCKETS = 32
MAX_DISTANCE = 128
NORM_EPS = 1e-6
NEG_INF = -1e30
LAM_INIT = 0.8 - 0.6 * math.exp(-0.3 * 1)

SUBLANES = 8
LANES = 128
HIST_ROWS = 32
VMEM_LIMIT = 56 * 1024 * 1024

TOKEN_TILE = 512
FF_CHUNK = 512
CONV_ROWS = 32
ATTN_BLOCK = 512
PAGES_PER_STEP = 8


def _params(n_axes):
    return pltpu.CompilerParams(dimension_semantics=("arbitrary",) * n_axes,
                                vmem_limit_bytes=VMEM_LIMIT)


def _resident(shape):
    nd = len(shape)
    return pl.BlockSpec(shape, lambda *_: (0,) * nd, pipeline_mode=pl.Buffered(1))


def _rms_unit(x):
    return x * lax.rsqrt(jnp.mean(x * x, axis=-1, keepdims=True) + NORM_EPS)


def _mlp(h, g_ref, w1_ref, w2_ref):
    hn = (_rms_unit(h) * g_ref[...]).astype(BF16)
    d_ff = w1_ref.shape[1]
    acc = h
    for c in range(d_ff // FF_CHUNK):
        u = jnp.dot(hn, w1_ref[:, c * FF_CHUNK:(c + 1) * FF_CHUNK], preferred_element_type=F32)
        u = jnp.maximum(u, 0.0)
        acc = acc + jnp.dot((u * u).astype(BF16), w2_ref[c * FF_CHUNK:(c + 1) * FF_CHUNK, :],
                            preferred_element_type=F32)
    return acc


def _glu_kernel(x_ref, g_ref, w_ref, b_ref, o_ref):
    d = x_ref.shape[1]
    hn = (_rms_unit(x_ref[...]) * g_ref[...]).astype(BF16)
    u = jnp.dot(hn, w_ref[...], preferred_element_type=F32) + b_ref[...]
    o_ref[...] = u[:, :d] * jax.nn.sigmoid(u[:, d:])


def _glu_call(x2, g, w, b):
    n, d = x2.shape
    tm = min(TOKEN_TILE, n)
    return pl.pallas_call(
        _glu_kernel,
        grid=(n // tm,),
        in_specs=[pl.BlockSpec((tm, d), lambda i: (i, 0)),
                  _resident((1, d)), _resident((d, 2 * d)), _resident((1, 2 * d))],
        out_specs=pl.BlockSpec((tm, d), lambda i: (i, 0)),
        out_shape=jax.ShapeDtypeStruct((n, d), F32),
        compiler_params=_params(1),
        name="glu",
    )(x2, g, w, b)


def _ln_swish(c, ln_g_ref, ln_b_ref):
    mu = jnp.mean(c, axis=-1, keepdims=True)
    cc = c - mu
    var = jnp.mean(cc * cc, axis=-1, keepdims=True)
    y = cc * lax.rsqrt(var + NORM_EPS) * ln_g_ref[...] + ln_b_ref[...]
    return y * jax.nn.sigmoid(y)


def _conv_long_kernel(g_ref, hist_ref, dw_ref, dwb_ref, ln_g_ref, ln_b_ref, o_ref, sh_ref):
    tl = g_ref.shape[1]
    first = pl.program_id(1) == 0
    hist = jnp.where(first, 0.0, hist_ref[0])
    sh_ref[0, 0:HIST_ROWS, :] = hist
    sh_ref[0, HIST_ROWS:HIST_ROWS + tl, :] = g_ref[0]
    n_sh = tl + HIST_ROWS - SUBLANES
    for b in range(1, SUBLANES):
        sh_ref[b, 0:n_sh, :] = sh_ref[0, b:b + n_sh, :]

    def chunk(ci, carry):
        r0 = pl.multiple_of(ci * CONV_ROWS, CONV_ROWS)
        acc = jnp.broadcast_to(dwb_ref[...], (CONV_ROWS, dwb_ref.shape[1]))
        for k in range(CONV_WIDTH):
            off = HIST_ROWS - (CONV_WIDTH - 1) + k
            a, b = divmod(off, SUBLANES)
            acc = acc + dw_ref[k:k + 1, :] * sh_ref[b, pl.ds(r0 + a * SUBLANES, CONV_ROWS), :]
        o_ref[0, pl.ds(r0, CONV_ROWS), :] = _ln_swish(acc, ln_g_ref, ln_b_ref).astype(o_ref.dtype)
        return carry

    lax.fori_loop(0, tl // CONV_ROWS, chunk, 0)


def _conv_long_call(g3, dw, dwb, ln_g, ln_b):
    bsz, t, d = g3.shape
    tl = min(TOKEN_TILE, t)
    per = tl // HIST_ROWS
    return pl.pallas_call(
        _conv_long_kernel,
        grid=(bsz, t // tl),
        in_specs=[pl.BlockSpec((1, tl, d), lambda b, j: (b, j, 0)),
                  pl.BlockSpec((1, HIST_ROWS, d), lambda b, j: (b, jnp.maximum(j * per - 1, 0), 0)),
                  _resident((HIST_ROWS, d)), _resident((1, d)), _resident((1, d)), _resident((1, d))],
        out_specs=pl.BlockSpec((1, tl, d), lambda b, j: (b, j, 0)),
        out_shape=jax.ShapeDtypeStruct((bsz, t, d), BF16),
        scratch_shapes=[pltpu.VMEM((SUBLANES, tl + HIST_ROWS, d), F32)],
        compiler_params=_params(2),
        name="conv_long",
    )(g3, g3, dw, dwb, ln_g, ln_b)


def _conv_short_kernel(ext_ref, dw_ref, dwb_ref, ln_g_ref, ln_b_ref, o_ref):
    s, rows, d = ext_ref.shape
    t = rows - HIST_ROWS
    acc = jnp.broadcast_to(dwb_ref[...].reshape(1, 1, d), (s, t, d))
    for k in range(CONV_WIDTH):
        off = HIST_ROWS - (CONV_WIDTH - 1) + k
        acc = acc + dw_ref[k:k + 1, :].reshape(1, 1, d) * ext_ref[:, off:off + t, :]
    o_ref[...] = _ln_swish(acc.reshape(s * t, d), ln_g_ref, ln_b_ref).astype(o_ref.dtype)


def _conv_short_call(ext3, dw, dwb, ln_g, ln_b):
    s, rows, d = ext3.shape
    t = rows - HIST_ROWS
    return pl.pallas_call(
        _conv_short_kernel,
        grid=(1,),
        in_specs=[_resident((s, rows, d)), _resident((HIST_ROWS, d)),
                  _resident((1, d)), _resident((1, d)), _resident((1, d))],
        out_specs=pl.BlockSpec((s * t, d), lambda i: (0, 0)),
        out_shape=jax.ShapeDtypeStruct((s * t, d), BF16),
        compiler_params=_params(1),
        name="conv_short",
    )(ext3, dw, dwb, ln_g, ln_b)


def _pw2_mlp_kernel(c_ref, x_ref, pw2_ref, pw2b_ref, gm_ref, w1_ref, w2_ref, o_ref):
    h = x_ref[...] + jnp.dot(c_ref[...], pw2_ref[...], preferred_element_type=F32) + pw2b_ref[...]
    o_ref[...] = _mlp(h, gm_ref, w1_ref, w2_ref)


def _pw2_mlp_call(c2, x2, pw2, pw2b, gm, w1, w2):
    n, d = x2.shape
    tm = min(TOKEN_TILE, n)
    row = lambda i: (i, 0)
    return pl.pallas_call(
        _pw2_mlp_kernel,
        grid=(n // tm,),
        in_specs=[pl.BlockSpec((tm, d), row), pl.BlockSpec((tm, d), row),
                  _resident((d, d)), _resident((1, d)), _resident((1, d)),
                  _resident(w1.shape), _resident(w2.shape)],
        out_specs=pl.BlockSpec((tm, d), row),
        out_shape=jax.ShapeDtypeStruct((n, d), F32),
        compiler_params=_params(1),
        name="pw2_mlp",
    )(c2, x2, pw2, pw2b, gm, w1, w2)


def _qkv_kernel(h_ref, gkv_ref, gq_ref, wk_ref, wv_ref, wq_ref,
                k_ref, v_ref, q_ref, kz_ref, va_ref):
    tm, d = h_ref.shape
    y = _rms_unit(h_ref[...])
    kvn = (y * gkv_ref[...]).astype(BF16)
    hn = (y * gq_ref[...]).astype(BF16)
    k = jnp.dot(kvn, wk_ref[...], preferred_element_type=F32)
    v = jnp.dot(kvn, wv_ref[...], preferred_element_type=F32)
    q = jnp.dot(hn, wq_ref[...], preferred_element_type=F32)
    k_ref[...] = k
    v_ref[...] = v
    q_ref[...] = (q * (HEAD_DIM ** -0.5)).astype(BF16)
    first_half = (lax.broadcasted_iota(jnp.int32, (tm, d), 1) % V_HEAD_DIM) < HEAD_DIM
    kz_ref[0] = jnp.where(first_half, k, 0.0).astype(BF16)
    kz_ref[1] = jnp.where(first_half, 0.0, k).astype(BF16)
    ones_col = (lax.broadcasted_iota(jnp.int32, (tm, V_HEAD_DIM), 1) == 0).astype(BF16)
    vb = v.astype(BF16)
    for h in range(N_HEADS):
        va_ref[:, 2 * h * V_HEAD_DIM:(2 * h + 1) * V_HEAD_DIM] = vb[:, h * V_HEAD_DIM:(h + 1) * V_HEAD_DIM]
        va_ref[:, (2 * h + 1) * V_HEAD_DIM:(2 * h + 2) * V_HEAD_DIM] = ones_col


def _qkv_call(h2, gkv, gq, wk, wv, wq):
    n, d = h2.shape
    tm = min(TOKEN_TILE, n)
    row = lambda i: (i, 0)
    return pl.pallas_call(
        _qkv_kernel,
        grid=(n // tm,),
        in_specs=[pl.BlockSpec((tm, d), row), _resident((1, d)), _resident((1, d)),
                  _resident((d, d)), _resident((d, d)), _resident((d, d))],
        out_specs=[pl.BlockSpec((tm, d), row), pl.BlockSpec((tm, d), row), pl.BlockSpec((tm, d), row),
                   pl.BlockSpec((2, tm, d), lambda i: (0, i, 0)),
                   pl.BlockSpec((tm, 2 * d), row)],
        out_shape=[jax.ShapeDtypeStruct((n, d), F32), jax.ShapeDtypeStruct((n, d), F32),
                   jax.ShapeDtypeStruct((n, d), BF16), jax.ShapeDtypeStruct((2, n, d), BF16),
                   jax.ShapeDtypeStruct((n, 2 * d), BF16)],
        compiler_params=_params(1),
        name="qkv",
    )(h2, gkv, gq, wk, wv, wq)


def _lambda(lam_ref):
    lv = lam_ref[...]
    s1 = jnp.sum(lv[0:1, :] * lv[1:2, :], axis=-1, keepdims=True)
    s2 = jnp.sum(lv[2:3, :] * lv[3:4, :], axis=-1, keepdims=True)
    return jnp.exp(s1) - jnp.exp(s2) + LAM_INIT


def _sub_norm(o, subg_ref):
    return (_rms_unit(o) * subg_ref[...]) * (1.0 - LAM_INIT)


def _rel_bucket(n):
    max_exact = N_BUCKETS // 2
    nf = jnp.maximum(n, 1).astype(F32)
    large = max_exact + (jnp.log(nf / max_exact) / math.log(MAX_DISTANCE / max_exact)
                         * (N_BUCKETS - max_exact)).astype(jnp.int32)
    large = jnp.minimum(large, N_BUCKETS - 1)
    return jnp.where(n < max_exact, n, large)


def _shifted_bias(rel_table, dist, valid):
    far = rel_table[N_BUCKETS - 1]
    vals = rel_table[_rel_bucket(jnp.maximum(dist, 0))] - far
    vals = jnp.where(valid[..., None], vals, NEG_INF)
    return jnp.moveaxis(vals, -1, 0).astype(F32)


def _attn_kernel(q_ref, kz_ref, va_ref, bias_ref, lam_ref, subg_ref, o_ref, acc_ref, m_ref):
    blk = q_ref.shape[1]
    i = pl.program_id(2)
    q = q_ref[0]
    acc_ref[...] = jnp.zeros_like(acc_ref)
    m_ref[...] = jnp.full_like(m_ref, NEG_INF)
    reps = blk // LANES

    def update(j, bias):
        r0 = pl.multiple_of(j * blk, blk)
        va = va_ref[0, pl.ds(r0, blk), :]
        for c in range(2):
            kc = kz_ref[c, 0, pl.ds(r0, blk), :]
            s = lax.dot_general(q, kc, (((1,), (1,)), ((), ())), preferred_element_type=F32)
            if bias is not None:
                s = s + bias
            m_prev = m_ref[c]
            m_new = jnp.maximum(m_prev, jnp.max(s, axis=1, keepdims=True))
            alpha = jnp.exp(m_prev - m_new)
            p = jnp.exp(s - jnp.concatenate([m_new] * reps, axis=1))
            pv = jnp.dot(p.astype(BF16), va, preferred_element_type=F32)
            acc_ref[c] = jnp.concatenate([alpha, alpha], axis=1) * acc_ref[c] + pv
            m_ref[c] = m_new

    def far_body(j, carry):
        update(j, None)
        return carry

    lax.fori_loop(0, jnp.maximum(i - 1, 0), far_body, 0)

    @pl.when(i > 0)
    def _():
        update(i - 1, bias_ref[0, 1])

    update(i, bias_ref[0, 0])

    lam = _lambda(lam_ref)
    a1 = acc_ref[0]
    a2 = acc_ref[1]
    o = (a1[:, :V_HEAD_DIM] / a1[:, V_HEAD_DIM:V_HEAD_DIM + 1]
         - lam * (a2[:, :V_HEAD_DIM] / a2[:, V_HEAD_DIM:V_HEAD_DIM + 1]))
    o_ref[0] = _sub_norm(o, subg_ref).astype(o_ref.dtype)


def _attn_call(q3, kz4, va3, bias, lamv, subg):
    bsz, t, d = q3.shape
    blk = min(ATTN_BLOCK, t)
    return pl.pallas_call(
        _attn_kernel,
        grid=(bsz, N_HEADS, t // blk),
        in_specs=[pl.BlockSpec((1, blk, V_HEAD_DIM), lambda b, h, i: (b, i, h)),
                  pl.BlockSpec((2, 1, t, V_HEAD_DIM), lambda b, h, i: (0, b, 0, h)),
                  pl.BlockSpec((1, t, 2 * V_HEAD_DIM), lambda b, h, i: (b, 0, h)),
                  pl.BlockSpec((1, 2, blk, blk), lambda b, h, i: (h, 0, 0, 0)),
                  _resident((SUBLANES, LANES)), _resident((1, V_HEAD_DIM))],
        out_specs=pl.BlockSpec((1, blk, V_HEAD_DIM), lambda b, h, i: (b, i, h)),
        out_shape=jax.ShapeDtypeStruct((bsz, t, d), BF16),
        scratch_shapes=[pltpu.VMEM((2, blk, 2 * V_HEAD_DIM), F32),
                        pltpu.VMEM((2, blk, LANES), F32)],
        compiler_params=_params(3),
        name="attn_prompt",
    )(q3, kz4, va3, bias, lamv, subg)


def _decode_kernel(pt_ref, q_ref, *refs, n_pages_step):
    g_n = n_pages_step
    k_refs = refs[:g_n]
    v_refs = refs[g_n:2 * g_n]
    (knew_ref, vnew_ref, mask_ref, blast_ref, bnew_ref, lam_ref, subg_ref,
     o_ref, qall_ref, acc_ref, m_ref, l_ref) = refs[2 * g_n:]
    st = pl.program_id(1)
    last = pl.num_programs(1) - 1
    tq = q_ref.shape[1]

    @pl.when(st == 0)
    def _():
        lane = lax.broadcasted_iota(jnp.int32, (tq, V_HEAD_DIM), 1)
        for h in range(N_HEADS):
            qh = q_ref[0, :, h * V_HEAD_DIM:(h + 1) * V_HEAD_DIM]
            qall_ref[2 * h * tq:(2 * h + 1) * tq, :] = jnp.where(lane < HEAD_DIM, qh, jnp.zeros_like(qh))
            qall_ref[(2 * h + 1) * tq:(2 * h + 2) * tq, :] = jnp.where(lane < HEAD_DIM, jnp.zeros_like(qh), qh)
        acc_ref[...] = jnp.zeros_like(acc_ref)
        m_ref[...] = jnp.full_like(m_ref, NEG_INF)
        l_ref[...] = jnp.zeros_like(l_ref)

    qall = qall_ref[...]

    def update(s_list, v_list):
        m_prev = m_ref[...]
        m_cur = functools.reduce(jnp.maximum, [jnp.max(s, axis=1, keepdims=True) for s in s_list])
        m_new = jnp.maximum(m_prev, m_cur)
        alpha = jnp.exp(m_prev - m_new)
        p_list = [jnp.exp(s - m_new) for s in s_list]
        l_ref[...] = alpha * l_ref[...] + functools.reduce(
            jnp.add, [jnp.sum(p, axis=1, keepdims=True) for p in p_list])
        pv = functools.reduce(jnp.add, [jnp.dot(p.astype(BF16), v, preferred_element_type=F32)
                                        for p, v in zip(p_list, v_list)])
        acc_ref[...] = alpha * acc_ref[...] + pv
        m_ref[...] = m_new

    nt = (((1,), (1,)), ((), ()))
    is_last = (st == last).astype(F32)
    s_list = []
    for g in range(g_n):
        s = lax.dot_general(qall, k_refs[g][0].astype(BF16), nt, preferred_element_type=F32)
        s = s + mask_ref[...]
        if g == g_n - 1:
            s = s + is_last * blast_ref[...]
        s_list.append(s)
    update(s_list, [v_refs[g][0].astype(BF16) for g in range(g_n)])

    @pl.when(st == last)
    def _():
        s_new = lax.dot_general(qall, knew_ref[0], nt, preferred_element_type=F32) + bnew_ref[...]
        update([s_new], [vnew_ref[0]])
        lam = _lambda(lam_ref)
        o_all = acc_ref[...] / l_ref[...]
        for h in range(N_HEADS):
            o = o_all[2 * h * tq:(2 * h + 1) * tq] - lam * o_all[(2 * h + 1) * tq:(2 * h + 2) * tq]
            o_ref[0, :, h * V_HEAD_DIM:(h + 1) * V_HEAD_DIM] = _sub_norm(o, subg_ref).astype(o_ref.dtype)


def _decode_call(page_table, q3, cache_k3, cache_v3, knew, vnew, mask, blast, bnew, lamv, subg):
    db, n_pages = page_table.shape
    _, tq, d = q3.shape
    prow = cache_k3.shape[1]
    g_n = math.gcd(PAGES_PER_STEP, n_pages)
    rows = N_HEADS * 2 * tq

    def page_spec(g):
        return pl.BlockSpec((1, prow, V_HEAD_DIM), lambda b, s, pt: (pt[b, s * g_n + g], 0, 0))

    per_b = lambda b, s, pt: (b, 0, 0)
    const2 = lambda b, s, pt: (0, 0)
    tile = pl.BlockSpec((rows, prow), const2)
    grid_spec = pltpu.PrefetchScalarGridSpec(
        num_scalar_prefetch=1,
        grid=(db, n_pages // g_n),
        in_specs=([pl.BlockSpec((1, tq, d), per_b)]
                  + [page_spec(g) for g in range(g_n)] + [page_spec(g) for g in range(g_n)]
                  + [pl.BlockSpec((1, prow, V_HEAD_DIM), per_b), pl.BlockSpec((1, prow, V_HEAD_DIM), per_b),
                     tile, tile, tile,
                     pl.BlockSpec((SUBLANES, LANES), const2), pl.BlockSpec((1, V_HEAD_DIM), const2)]),
        out_specs=pl.BlockSpec((1, tq, d), per_b),
        scratch_shapes=[pltpu.VMEM((rows, V_HEAD_DIM), BF16), pltpu.VMEM((rows, V_HEAD_DIM), F32),
                        pltpu.VMEM((rows, 1), F32), pltpu.VMEM((rows, 1), F32)],
    )
    return pl.pallas_call(
        functools.partial(_decode_kernel, n_pages_step=g_n),
        grid_spec=grid_spec,
        out_shape=jax.ShapeDtypeStruct((db, tq, d), BF16),
        compiler_params=_params(2),
        name="attn_decode",
    )(page_table, q3, *([cache_k3] * g_n), *([cache_v3] * g_n), knew, vnew, mask, blast, bnew, lamv, subg)


def _post_kernel(a_ref, h_ref, wo_ref, gm_ref, w1_ref, w2_ref, gf_ref, o_ref):
    h = h_ref[...] + jnp.dot(a_ref[...], wo_ref[...], preferred_element_type=F32)
    h = _mlp(h, gm_ref, w1_ref, w2_ref)
    o_ref[...] = _rms_unit(h) * gf_ref[...]


def _post_call(a2, h2, wo, gm, w1, w2, gf):
    n, d = h2.shape
    tm = min(TOKEN_TILE, n)
    row = lambda i: (i, 0)
    return pl.pallas_call(
        _post_kernel,
        grid=(n // tm,),
        in_specs=[pl.BlockSpec((tm, d), row), pl.BlockSpec((tm, d), row),
                  _resident((d, d)), _resident((1, d)), _resident(w1.shape), _resident(w2.shape),
                  _resident((1, d))],
        out_specs=pl.BlockSpec((tm, d), row),
        out_shape=jax.ShapeDtypeStruct((n, d), F32),
        compiler_params=_params(1),
        name="post",
    )(a2, h2, wo, gm, w1, w2, gf)


def _row(v):
    return v.reshape(1, -1).astype(F32)


def kernel(x_prompt, x_sample, state_conv, cache_k, cache_v, page_table, norm_mix_g, norm_mlp_g, conv_pw1_w, conv_pw1_b, conv_dw_w, conv_dw_b, conv_ln_g, conv_ln_b, conv_pw2_w, conv_pw2_b, kv_norm_g, w_k, w_v, w_q, lambda_q1, lambda_k1, lambda_q2, lambda_k2, subln_g, w_o, rel_bias_table, mlp_w1, mlp_w2, final_norm_g):
    bsz, seq, d = x_prompt.shape
    db, dseq, _ = x_sample.shape
    n_pool, page = cache_k.shape[0], cache_k.shape[1]
    n_pages = page_table.shape[1]
    past_len = n_pages * page
    hist = CONV_WIDTH - 1

    pw1 = conv_pw1_w[0].astype(BF16)
    pw1b = _row(conv_pw1_b[0])
    dw = jnp.pad(conv_dw_w[0].astype(F32), ((0, HIST_ROWS - CONV_WIDTH), (0, 0)))
    dwb, ln_g, ln_b = _row(conv_dw_b[0]), _row(conv_ln_g[0]), _row(conv_ln_b[0])
    pw2 = conv_pw2_w[0].astype(BF16)
    pw2b = _row(conv_pw2_b[0])
    w1 = mlp_w1.astype(BF16)
    w2 = mlp_w2.astype(BF16)
    wk, wv, wq, wo = w_k.astype(BF16), w_v.astype(BF16), w_q[0].astype(BF16), w_o[0].astype(BF16)
    lamv = jnp.pad(jnp.stack([lambda_q1[0], lambda_k1[0], lambda_q2[0], lambda_k2[0]]).astype(F32),
                   ((0, SUBLANES - 4), (0, LANES - HEAD_DIM)))
    subg = _row(subln_g[0])
    rel = rel_bias_table.astype(F32)

    def front(x3, conv_fn):
        n = x3.shape[0] * x3.shape[1]
        x2 = x3.reshape(n, d)
        g2 = _glu_call(x2, _row(norm_mix_g[0]), pw1, pw1b)
        c2 = conv_fn(g2)
        h2 = _pw2_mlp_call(c2, x2, pw2, pw2b, _row(norm_mlp_g[0]), w1[0], w2[0])
        k2, v2, q2, kz, va = _qkv_call(h2, _row(kv_norm_g), _row(norm_mix_g[1]), wk, wv, wq)
        return g2, h2, k2, v2, q2, kz, va

    def back(a2, h2):
        return _post_call(a2, h2, wo, _row(norm_mlp_g[1]), w1[1], w2[1], _row(final_norm_g))

    conv_p = lambda g2: _conv_long_call(g2.reshape(bsz, seq, d), dw, dwb, ln_g, ln_b).reshape(bsz * seq, d)
    g_p, h_p, k_p, v_p, q_p, kz_p, va_p = front(x_prompt, conv_p)
    blk = min(ATTN_BLOCK, seq)
    r = jnp.arange(blk, dtype=jnp.int32)[:, None]
    c = jnp.arange(blk, dtype=jnp.int32)[None, :]
    bias_p = jnp.stack([_shifted_bias(rel, r - c, r >= c),
                        _shifted_bias(rel, blk + r - c, jnp.full((blk, blk), True))], axis=1)
    a_p = _attn_call(q_p.reshape(bsz, seq, d), kz_p.reshape(2, bsz, seq, d),
                     va_p.reshape(bsz, seq, 2 * d), bias_p, lamv, subg)
    y_prompt = back(a_p.reshape(bsz * seq, d), h_p).reshape(bsz, seq, d)
    conv_state_prompt = g_p.reshape(bsz, seq, d)[:, seq - hist:][None]

    def conv_s(g2):
        ext = jnp.concatenate([jnp.zeros((db, HIST_ROWS - hist, d), F32), state_conv[0].astype(F32),
                               g2.reshape(db, dseq, d)], axis=1)
        return _conv_short_call(ext, dw, dwb, ln_g, ln_b)

    g_s, h_s, k_s, v_s, q_s, _, _ = front(x_sample, conv_s)
    prow = page * N_HEADS
    pad_new = lambda a: jnp.pad(a.reshape(db, dseq, N_HEADS, V_HEAD_DIM).astype(BF16),
                                ((0, 0), (0, page - dseq), (0, 0), (0, 0))).reshape(db, prow, V_HEAD_DIM)
    n_rows = 2 * N_HEADS * dseq
    row_q = jnp.tile(jnp.arange(dseq, dtype=jnp.int32), 2 * N_HEADS)[:, None]
    row_h = jnp.repeat(jnp.arange(N_HEADS, dtype=jnp.int32), 2 * dseq)
    key = (jnp.arange(prow, dtype=jnp.int32) // N_HEADS)[None, :]
    same_head = (jnp.arange(prow, dtype=jnp.int32) % N_HEADS)[None, :] == row_h[:, None]
    pick = lambda b4: b4[row_h, jnp.arange(n_rows)]
    mask = jnp.where(same_head, 0.0, NEG_INF).astype(F32)
    blast = jnp.where(same_head, pick(_shifted_bias(rel, page + row_q - key, jnp.full((n_rows, prow), True))), 0.0)
    bnew = pick(_shifted_bias(rel, row_q - key, same_head & (key <= row_q) & (key < dseq)))
    a_s = _decode_call(page_table.astype(jnp.int32), q_s.reshape(db, dseq, d),
                       cache_k.reshape(n_pool, prow, V_HEAD_DIM), cache_v.reshape(n_pool, prow, V_HEAD_DIM),
                       pad_new(k_s), pad_new(v_s), mask, blast, bnew, lamv, subg)
    y_sample = back(a_s.reshape(db * dseq, d), h_s).reshape(db, dseq, d)
    conv_state_sample = jnp.concatenate([state_conv[0].astype(F32), g_s.reshape(db, dseq, d)],
                                        axis=1)[:, dseq:][None]

    kv_shape = (N_HEADS, V_HEAD_DIM)
    return (y_prompt, y_sample, conv_state_prompt, conv_state_sample,
            k_p.reshape(bsz, seq, *kv_shape), v_p.reshape(bsz, seq, *kv_shape),
            k_s.reshape(db, dseq, *kv_shape), v_s.reshape(db, dseq, *kv_shape))
```

```python
import functools
import math

import jax
import jax.numpy as jnp
from jax import lax
from jax.experimental import pallas as pl
from jax.experimental.pallas import tpu as pltpu

F32 = jnp.float32
BF16 = jnp.bfloat16

N_HEADS = 8
HEAD_DIM = 64
V_HEAD_DIM = 2 * HEAD_DIM
CONV_WIDTH = 31
N_BUCKETS = 32
MAX_DISTANCE = 128
NORM_EPS = 1e-6
NEG_INF = -1e30
LAM_INIT = 0.8 - 0.6 * math.exp(-0.3 * 1)
LOG2E = math.log2(math.e)
Q_SCALE = HEAD_DIM ** -0.5 * LOG2E

SUBLANES = 8
LANES = 128
HIST_ROWS = 32
VMEM_LIMIT = 56 * 1024 * 1024

TOKEN_TILE = 512
FF_CHUNK = 512
CONV_ROWS = 32
ATTN_BLOCK = 512
PAGES_PER_STEP = 8


def _params(n_axes):
    return pltpu.CompilerParams(dimension_semantics=("arbitrary",) * n_axes,
                                vmem_limit_bytes=VMEM_LIMIT)


def _resident(shape):
    nd = len(shape)
    return pl.BlockSpec(shape, lambda *_: (0,) * nd, pipeline_mode=pl.Buffered(1))


def _rms_unit(x):
    return x * lax.rsqrt(jnp.mean(x * x, axis=-1, keepdims=True) + NORM_EPS)


def _mlp(h, g_ref, w1_ref, w2_ref):
    hn = (_rms_unit(h) * g_ref[...]).astype(BF16)
    d_ff = w1_ref.shape[1]
    acc = h
    for c in range(d_ff // FF_CHUNK):
        u = jnp.dot(hn, w1_ref[:, c * FF_CHUNK:(c + 1) * FF_CHUNK], preferred_element_type=F32)
        u = jnp.maximum(u, 0.0)
        acc = acc + jnp.dot((u * u).astype(BF16), w2_ref[c * FF_CHUNK:(c + 1) * FF_CHUNK, :],
                            preferred_element_type=F32)
    return acc


def _glu_kernel(x_ref, g_ref, w_ref, b_ref, o_ref):
    d = x_ref.shape[1]
    hn = (_rms_unit(x_ref[...]) * g_ref[...]).astype(BF16)
    u = jnp.dot(hn, w_ref[...], preferred_element_type=F32) + b_ref[...]
    o_ref[...] = u[:, :d] * jax.nn.sigmoid(u[:, d:])


def _glu_call(x2, g, w, b):
    n, d = x2.shape
    tm = min(TOKEN_TILE, n)
    return pl.pallas_call(
        _glu_kernel,
        grid=(n // tm,),
        in_specs=[pl.BlockSpec((tm, d), lambda i: (i, 0)),
                  _resident((1, d)), _resident((d, 2 * d)), _resident((1, 2 * d))],
        out_specs=pl.BlockSpec((tm, d), lambda i: (i, 0)),
        out_shape=jax.ShapeDtypeStruct((n, d), F32),
        compiler_params=_params(1),
        name="glu",
    )(x2, g, w, b)


def _ln_swish(c, ln_g_ref, ln_b_ref):
    mu = jnp.mean(c, axis=-1, keepdims=True)
    cc = c - mu
    var = jnp.mean(cc * cc, axis=-1, keepdims=True)
    y = cc * lax.rsqrt(var + NORM_EPS) * ln_g_ref[...] + ln_b_ref[...]
    return y * jax.nn.sigmoid(y)


def _conv_long_kernel(g_ref, hist_ref, dw_ref, dwb_ref, ln_g_ref, ln_b_ref, o_ref, sh_ref):
    tl = g_ref.shape[1]
    first = pl.program_id(1) == 0
    hist = jnp.where(first, 0.0, hist_ref[0])
    sh_ref[0, 0:HIST_ROWS, :] = hist
    sh_ref[0, HIST_ROWS:HIST_ROWS + tl, :] = g_ref[0]
    n_sh = tl + HIST_ROWS - SUBLANES
    for b in range(1, SUBLANES):
        sh_ref[b, 0:n_sh, :] = sh_ref[0, b:b + n_sh, :]

    def chunk(ci, carry):
        r0 = pl.multiple_of(ci * CONV_ROWS, CONV_ROWS)
        acc = jnp.broadcast_to(dwb_ref[...], (CONV_ROWS, dwb_ref.shape[1]))
        for k in range(CONV_WIDTH):
            off = HIST_ROWS - (CONV_WIDTH - 1) + k
            a, b = divmod(off, SUBLANES)
            acc = acc + dw_ref[k:k + 1, :] * sh_ref[b, pl.ds(r0 + a * SUBLANES, CONV_ROWS), :]
        o_ref[0, pl.ds(r0, CONV_ROWS), :] = _ln_swish(acc, ln_g_ref, ln_b_ref).astype(o_ref.dtype)
        return carry

    lax.fori_loop(0, tl // CONV_ROWS, chunk, 0, unroll=2)


def _conv_long_call(g3, dw, dwb, ln_g, ln_b):
    bsz, t, d = g3.shape
    tl = min(TOKEN_TILE, t)
    per = tl // HIST_ROWS
    return pl.pallas_call(
        _conv_long_kernel,
        grid=(bsz, t // tl),
        in_specs=[pl.BlockSpec((1, tl, d), lambda b, j: (b, j, 0)),
                  pl.BlockSpec((1, HIST_ROWS, d), lambda b, j: (b, jnp.maximum(j * per - 1, 0), 0)),
                  _resident((HIST_ROWS, d)), _resident((1, d)), _resident((1, d)), _resident((1, d))],
        out_specs=pl.BlockSpec((1, tl, d), lambda b, j: (b, j, 0)),
        out_shape=jax.ShapeDtypeStruct((bsz, t, d), BF16),
        scratch_shapes=[pltpu.VMEM((SUBLANES, tl + HIST_ROWS, d), F32)],
        compiler_params=_params(2),
        name="conv_long",
    )(g3, g3, dw, dwb, ln_g, ln_b)


def _conv_short_kernel(ext_ref, dw_ref, dwb_ref, ln_g_ref, ln_b_ref, o_ref):
    s, rows, d = ext_ref.shape
    t = rows - HIST_ROWS
    acc = jnp.broadcast_to(dwb_ref[...].reshape(1, 1, d), (s, t, d))
    for k in range(CONV_WIDTH):
        off = HIST_ROWS - (CONV_WIDTH - 1) + k
        acc = acc + dw_ref[k:k + 1, :].reshape(1, 1, d) * ext_ref[:, off:off + t, :]
    o_ref[...] = _ln_swish(acc.reshape(s * t, d), ln_g_ref, ln_b_ref).astype(o_ref.dtype)


def _conv_short_call(ext3, dw, dwb, ln_g, ln_b):
    s, rows, d = ext3.shape
    t = rows - HIST_ROWS
    return pl.pallas_call(
        _conv_short_kernel,
        grid=(1,),
        in_specs=[_resident((s, rows, d)), _resident((HIST_ROWS, d)),
                  _resident((1, d)), _resident((1, d)), _resident((1, d))],
        out_specs=pl.BlockSpec((s * t, d), lambda i: (0, 0)),
        out_shape=jax.ShapeDtypeStruct((s * t, d), BF16),
        compiler_params=_params(1),
        name="conv_short",
    )(ext3, dw, dwb, ln_g, ln_b)


def _pw2_mlp_kernel(c_ref, x_ref, pw2_ref, pw2b_ref, gm_ref, w1_ref, w2_ref, o_ref):
    h = x_ref[...] + jnp.dot(c_ref[...], pw2_ref[...], preferred_element_type=F32) + pw2b_ref[...]
    o_ref[...] = _mlp(h, gm_ref, w1_ref, w2_ref)


def _pw2_mlp_call(c2, x2, pw2, pw2b, gm, w1, w2):
    n, d = x2.shape
    tm = min(TOKEN_TILE, n)
    row = lambda i: (i, 0)
    return pl.pallas_call(
        _pw2_mlp_kernel,
        grid=(n // tm,),
        in_specs=[pl.BlockSpec((tm, d), row), pl.BlockSpec((tm, d), row),
                  _resident((d, d)), _resident((1, d)), _resident((1, d)),
                  _resident(w1.shape), _resident(w2.shape)],
        out_specs=pl.BlockSpec((tm, d), row),
        out_shape=jax.ShapeDtypeStruct((n, d), F32),
        compiler_params=_params(1),
        name="pw2_mlp",
    )(c2, x2, pw2, pw2b, gm, w1, w2)


def _qkv_kernel(h_ref, gkv_ref, gq_ref, wk_ref, wv_ref, wq_ref,
                k_ref, v_ref, q_ref, kb_ref, va_ref):
    tm, d = h_ref.shape
    y = _rms_unit(h_ref[...])
    kvn = (y * gkv_ref[...]).astype(BF16)
    hn = (y * gq_ref[...]).astype(BF16)
    k = jnp.dot(kvn, wk_ref[...], preferred_element_type=F32)
    v = jnp.dot(kvn, wv_ref[...], preferred_element_type=F32)
    q = jnp.dot(hn, wq_ref[...], preferred_element_type=F32)
    k_ref[...] = k
    v_ref[...] = v
    q_ref[...] = (q * Q_SCALE).astype(BF16)
    kb_ref[...] = k.astype(BF16)
    ones_col = (lax.broadcasted_iota(jnp.int32, (tm, V_HEAD_DIM), 1) == 0).astype(BF16)
    vb = v.astype(BF16)
    for h in range(N_HEADS):
        va_ref[:, 2 * h * V_HEAD_DIM:(2 * h + 1) * V_HEAD_DIM] = vb[:, h * V_HEAD_DIM:(h + 1) * V_HEAD_DIM]
        va_ref[:, (2 * h + 1) * V_HEAD_DIM:(2 * h + 2) * V_HEAD_DIM] = ones_col


def _qkv_call(h2, gkv, gq, wk, wv, wq):
    n, d = h2.shape
    tm = min(TOKEN_TILE, n)
    row = lambda i: (i, 0)
    return pl.pallas_call(
        _qkv_kernel,
        grid=(n // tm,),
        in_specs=[pl.BlockSpec((tm, d), row), _resident((1, d)), _resident((1, d)),
                  _resident((d, d)), _resident((d, d)), _resident((d, d))],
        out_specs=[pl.BlockSpec((tm, d), row), pl.BlockSpec((tm, d), row), pl.BlockSpec((tm, d), row),
                   pl.BlockSpec((tm, d), row),
                   pl.BlockSpec((tm, 2 * d), row)],
        out_shape=[jax.ShapeDtypeStruct((n, d), F32), jax.ShapeDtypeStruct((n, d), F32),
                   jax.ShapeDtypeStruct((n, d), BF16), jax.ShapeDtypeStruct((n, d), BF16),
                   jax.ShapeDtypeStruct((n, 2 * d), BF16)],
        compiler_params=_params(1),
        name="qkv",
    )(h2, gkv, gq, wk, wv, wq)


def _lambda(lam_ref):
    lv = lam_ref[...]
    s1 = jnp.sum(lv[0:1, :] * lv[1:2, :], axis=-1, keepdims=True)
    s2 = jnp.sum(lv[2:3, :] * lv[3:4, :], axis=-1, keepdims=True)
    return jnp.exp(s1) - jnp.exp(s2) + LAM_INIT


def _sub_norm(o, subg_ref):
    return (_rms_unit(o) * subg_ref[...]) * (1.0 - LAM_INIT)


def _rel_bucket(n):
    max_exact = N_BUCKETS // 2
    nf = jnp.maximum(n, 1).astype(F32)
    large = max_exact + (jnp.log(nf / max_exact) / math.log(MAX_DISTANCE / max_exact)
                         * (N_BUCKETS - max_exact)).astype(jnp.int32)
    large = jnp.minimum(large, N_BUCKETS - 1)
    return jnp.where(n < max_exact, n, large)


def _shifted_bias(rel_table, dist, valid):
    bucket = _rel_bucket(jnp.maximum(dist, 0))[None]
    lead = (N_HEADS,) + (1,) * dist.ndim
    vals = jnp.zeros((N_HEADS,) + dist.shape, F32)
    for n in range(N_BUCKETS - 1):
        vals = jnp.where(bucket == n, (rel_table[n] - rel_table[N_BUCKETS - 1]).reshape(lead), vals)
    return jnp.where(valid[None], vals * LOG2E, NEG_INF)


def _attn_kernel(q_ref, k_ref, va_ref, bias_ref, lam_ref, subg_ref, o_ref, qz_ref, acc_ref, m_ref):
    blk = q_ref.shape[1]
    i = pl.program_id(2)
    q = q_ref[0]
    first_half = lax.broadcasted_iota(jnp.int32, q.shape, 1) < HEAD_DIM
    zero = jnp.zeros_like(q)
    qz_ref[0:blk, :] = jnp.where(first_half, q, zero)
    qz_ref[blk:2 * blk, :] = jnp.where(first_half, zero, q)
    acc_ref[...] = jnp.zeros_like(acc_ref)
    m_ref[...] = jnp.full_like(m_ref, NEG_INF)
    reps = blk // LANES

    nt = (((1,), (1,)), ((), ()))
    halves = [slice(c * blk, (c + 1) * blk) for c in range(2)]

    def scores(j):
        kj = k_ref[0, pl.ds(pl.multiple_of(j * blk, blk), blk), :]
        return [lax.dot_general(qz_ref[h, :], kj, nt, preferred_element_type=F32) for h in halves]

    def accumulate(j, s, bias):
        va = va_ref[0, pl.ds(pl.multiple_of(j * blk, blk), blk), :]
        if bias is not None:
            s = [sc + bias_ref[0, bias] for sc in s]
        m_prev = [m_ref[h, :] for h in halves]
        m_new = [jnp.maximum(mp, jnp.max(sc, axis=1, keepdims=True)) for mp, sc in zip(m_prev, s)]
        alpha = [jnp.exp2(mp - mn) for mp, mn in zip(m_prev, m_new)]
        p = [jnp.exp2(sc - jnp.concatenate([mn] * reps, axis=1)).astype(BF16) for sc, mn in zip(s, m_new)]
        pv = [jnp.dot(pc, va, preferred_element_type=F32) for pc in p]
        for h, al, pvc, mn in zip(halves, alpha, pv, m_new):
            acc_ref[h, :] = jnp.concatenate([al, al], axis=1) * acc_ref[h, :] + pvc
            m_ref[h, :] = mn

    def process(blocks):
        s_all = [scores(j) for j, _ in blocks]
        for (j, bias), s in zip(blocks, s_all):
            accumulate(j, s, bias)

    n_far = jnp.maximum(i - 1, 0)

    def far_pair(t, carry):
        process([(2 * t, None), (2 * t + 1, None)])
        return carry

    lax.fori_loop(0, n_far // 2, far_pair, 0)
    odd = n_far % 2 == 1
    diag, prev = 0, 1

    @pl.when(i == 0)
    def _():
        process([(i, diag)])

    @pl.when(jnp.logical_and(i > 0, jnp.logical_not(odd)))
    def _():
        process([(i - 1, prev), (i, diag)])

    @pl.when(odd)
    def _():
        process([(i - 2, None), (i - 1, prev), (i, diag)])

    lam = _lambda(lam_ref)
    a1 = acc_ref[0:blk, :]
    a2 = acc_ref[blk:2 * blk, :]
    o = (a1[:, :V_HEAD_DIM] / a1[:, V_HEAD_DIM:V_HEAD_DIM + 1]
         - lam * (a2[:, :V_HEAD_DIM] / a2[:, V_HEAD_DIM:V_HEAD_DIM + 1]))
    o_ref[0] = _sub_norm(o, subg_ref).astype(o_ref.dtype)


def _attn_call(q3, k3, va3, bias, lamv, subg):
    bsz, t, d = q3.shape
    blk = min(ATTN_BLOCK, t)
    return pl.pallas_call(
        _attn_kernel,
        grid=(bsz, N_HEADS, t // blk),
        in_specs=[pl.BlockSpec((1, blk, V_HEAD_DIM), lambda b, h, i: (b, i, h)),
                  pl.BlockSpec((1, t, V_HEAD_DIM), lambda b, h, i: (b, 0, h)),
                  pl.BlockSpec((1, t, 2 * V_HEAD_DIM), lambda b, h, i: (b, 0, h)),
                  pl.BlockSpec((1, 2, blk, blk), lambda b, h, i: (h, 0, 0, 0)),
                  _resident((SUBLANES, LANES)), _resident((1, V_HEAD_DIM))],
        out_specs=pl.BlockSpec((1, blk, V_HEAD_DIM), lambda b, h, i: (b, i, h)),
        out_shape=jax.ShapeDtypeStruct((bsz, t, d), BF16),
        scratch_shapes=[pltpu.VMEM((2 * blk, V_HEAD_DIM), BF16),
                        pltpu.VMEM((2 * blk, 2 * V_HEAD_DIM), F32),
                        pltpu.VMEM((2 * blk, LANES), F32)],
        compiler_params=_params(3),
        name="attn_prompt",
    )(q3, k3, va3, bias, lamv, subg)


def _decode_kernel(pt_ref, q_ref, *refs, n_pages_step):
    g_n = n_pages_step
    k_refs = refs[:g_n]
    v_refs = refs[g_n:2 * g_n]
    (knew_ref, vnew_ref, mask_ref, blast_ref, bnew_ref, lam_ref, subg_ref,
     o_ref, qall_ref, acc_ref, m_ref, l_ref) = refs[2 * g_n:]
    st = pl.program_id(1)
    last = pl.num_programs(1) - 1
    tq = q_ref.shape[1]

    @pl.when(st == 0)
    def _():
        lane = lax.broadcasted_iota(jnp.int32, (tq, V_HEAD_DIM), 1)
        for h in range(N_HEADS):
            qh = q_ref[0, :, h * V_HEAD_DIM:(h + 1) * V_HEAD_DIM]
            qall_ref[2 * h * tq:(2 * h + 1) * tq, :] = jnp.where(lane < HEAD_DIM, qh, jnp.zeros_like(qh))
            qall_ref[(2 * h + 1) * tq:(2 * h + 2) * tq, :] = jnp.where(lane < HEAD_DIM, jnp.zeros_like(qh), qh)
        acc_ref[...] = jnp.zeros_like(acc_ref)
        m_ref[...] = jnp.full_like(m_ref, NEG_INF)
        l_ref[...] = jnp.zeros_like(l_ref)

    qall = qall_ref[...]

    def update(s_list, v_list):
        m_prev = m_ref[...]
        m_cur = functools.reduce(jnp.maximum, [jnp.max(s, axis=1, keepdims=True) for s in s_list])
        m_new = jnp.maximum(m_prev, m_cur)
        alpha = jnp.exp2(m_prev - m_new)
        p_list = [jnp.exp2(s - m_new) for s in s_list]
        l_ref[...] = alpha * l_ref[...] + functools.reduce(
            jnp.add, [jnp.sum(p, axis=1, keepdims=True) for p in p_list])
        pv = functools.reduce(jnp.add, [jnp.dot(p.astype(BF16), v, preferred_element_type=F32)
                                        for p, v in zip(p_list, v_list)])
        acc_ref[...] = alpha * acc_ref[...] + pv
        m_ref[...] = m_new

    nt = (((1,), (1,)), ((), ()))
    is_last = (st == last).astype(F32)
    s_list = []
    for g in range(g_n):
        s = lax.dot_general(qall, k_refs[g][0].astype(BF16), nt, preferred_element_type=F32)
        s = s + mask_ref[...]
        if g == g_n - 1:
            s = s + is_last * blast_ref[...]
        s_list.append(s)
    update(s_list, [v_refs[g][0].astype(BF16) for g in range(g_n)])

    @pl.when(st == last)
    def _():
        s_new = lax.dot_general(qall, knew_ref[0], nt, preferred_element_type=F32) + bnew_ref[...]
        update([s_new], [vnew_ref[0]])
        lam = _lambda(lam_ref)
        o_all = acc_ref[...] / l_ref[...]
        for h in range(N_HEADS):
            o = o_all[2 * h * tq:(2 * h + 1) * tq] - lam * o_all[(2 * h + 1) * tq:(2 * h + 2) * tq]
            o_ref[0, :, h * V_HEAD_DIM:(h + 1) * V_HEAD_DIM] = _sub_norm(o, subg_ref).astype(o_ref.dtype)


def _decode_call(page_table, q3, cache_k3, cache_v3, knew, vnew, mask, blast, bnew, lamv, subg):
    db, n_pages = page_table.shape
    _, tq, d = q3.shape
    prow = cache_k3.shape[1]
    g_n = math.gcd(PAGES_PER_STEP, n_pages)
    rows = N_HEADS * 2 * tq

    def page_spec(g):
        return pl.BlockSpec((1, prow, V_HEAD_DIM), lambda b, s, pt: (pt[b, s * g_n + g], 0, 0))

    per_b = lambda b, s, pt: (b, 0, 0)
    const2 = lambda b, s, pt: (0, 0)
    tile = pl.BlockSpec((rows, prow), const2)
    grid_spec = pltpu.PrefetchScalarGridSpec(
        num_scalar_prefetch=1,
        grid=(db, n_pages // g_n),
        in_specs=([pl.BlockSpec((1, tq, d), per_b)]
                  + [page_spec(g) for g in range(g_n)] + [page_spec(g) for g in range(g_n)]
                  + [pl.BlockSpec((1, prow, V_HEAD_DIM), per_b), pl.BlockSpec((1, prow, V_HEAD_DIM), per_b),
                     tile, tile, tile,
                     pl.BlockSpec((SUBLANES, LANES), const2), pl.BlockSpec((1, V_HEAD_DIM), const2)]),
        out_specs=pl.BlockSpec((1, tq, d), per_b),
        scratch_shapes=[pltpu.VMEM((rows, V_HEAD_DIM), BF16), pltpu.VMEM((rows, V_HEAD_DIM), F32),
                        pltpu.VMEM((rows, 1), F32), pltpu.VMEM((rows, 1), F32)],
    )
    return pl.pallas_call(
        functools.partial(_decode_kernel, n_pages_step=g_n),
        grid_spec=grid_spec,
        out_shape=jax.ShapeDtypeStruct((db, tq, d), BF16),
        compiler_params=_params(2),
        name="attn_decode",
    )(page_table, q3, *([cache_k3] * g_n), *([cache_v3] * g_n), knew, vnew, mask, blast, bnew, lamv, subg)


def _post_kernel(a_ref, h_ref, wo_ref, gm_ref, w1_ref, w2_ref, gf_ref, o_ref):
    h = h_ref[...] + jnp.dot(a_ref[...], wo_ref[...], preferred_element_type=F32)
    h = _mlp(h, gm_ref, w1_ref, w2_ref)
    o_ref[...] = _rms_unit(h) * gf_ref[...]


def _post_call(a2, h2, wo, gm, w1, w2, gf):
    n, d = h2.shape
    tm = min(TOKEN_TILE, n)
    row = lambda i: (i, 0)
    return pl.pallas_call(
        _post_kernel,
        grid=(n // tm,),
        in_specs=[pl.BlockSpec((tm, d), row), pl.BlockSpec((tm, d), row),
                  _resident((d, d)), _resident((1, d)), _resident(w1.shape), _resident(w2.shape),
                  _resident((1, d))],
        out_specs=pl.BlockSpec((tm, d), row),
        out_shape=jax.ShapeDtypeStruct((n, d), F32),
        compiler_params=_params(1),
        name="post",
    )(a2, h2, wo, gm, w1, w2, gf)


def _row(v):
    return v.reshape(1, -1).astype(F32)


def kernel(x_prompt, x_sample, state_conv, cache_k, cache_v, page_table, norm_mix_g, norm_mlp_g, conv_pw1_w, conv_pw1_b, conv_dw_w, conv_dw_b, conv_ln_g, conv_ln_b, conv_pw2_w, conv_pw2_b, kv_norm_g, w_k, w_v, w_q, lambda_q1, lambda_k1, lambda_q2, lambda_k2, subln_g, w_o, rel_bias_table, mlp_w1, mlp_w2, final_norm_g):
    bsz, seq, d = x_prompt.shape
    db, dseq, _ = x_sample.shape
    n_pool, page = cache_k.shape[0], cache_k.shape[1]
    n_pages = page_table.shape[1]
    past_len = n_pages * page
    hist = CONV_WIDTH - 1

    pw1 = conv_pw1_w[0].astype(BF16)
    pw1b = _row(conv_pw1_b[0])
    dw = jnp.pad(conv_dw_w[0].astype(F32), ((0, HIST_ROWS - CONV_WIDTH), (0, 0)))
    dwb, ln_g, ln_b = _row(conv_dw_b[0]), _row(conv_ln_g[0]), _row(conv_ln_b[0])
    pw2 = conv_pw2_w[0].astype(BF16)
    pw2b = _row(conv_pw2_b[0])
    w1 = mlp_w1.astype(BF16)
    w2 = mlp_w2.astype(BF16)
    wk, wv, wq, wo = w_k.astype(BF16), w_v.astype(BF16), w_q[0].astype(BF16), w_o[0].astype(BF16)
    lamv = jnp.pad(jnp.stack([lambda_q1[0], lambda_k1[0], lambda_q2[0], lambda_k2[0]]).astype(F32),
                   ((0, SUBLANES - 4), (0, LANES - HEAD_DIM)))
    subg = _row(subln_g[0])
    rel = rel_bias_table.astype(F32)

    def front(x3, conv_fn):
        n = x3.shape[0] * x3.shape[1]
        x2 = x3.reshape(n, d)
        g2 = _glu_call(x2, _row(norm_mix_g[0]), pw1, pw1b)
        c2 = conv_fn(g2)
        h2 = _pw2_mlp_call(c2, x2, pw2, pw2b, _row(norm_mlp_g[0]), w1[0], w2[0])
        k2, v2, q2, kb, va = _qkv_call(h2, _row(kv_norm_g), _row(norm_mix_g[1]), wk, wv, wq)
        return g2, h2, k2, v2, q2, kb, va

    def back(a2, h2):
        return _post_call(a2, h2, wo, _row(norm_mlp_g[1]), w1[1], w2[1], _row(final_norm_g))

    conv_p = lambda g2: _conv_long_call(g2.reshape(bsz, seq, d), dw, dwb, ln_g, ln_b).reshape(bsz * seq, d)
    g_p, h_p, k_p, v_p, q_p, kb_p, va_p = front(x_prompt, conv_p)
    blk = min(ATTN_BLOCK, seq)
    r = jnp.arange(blk, dtype=jnp.int32)[:, None]
    c = jnp.arange(blk, dtype=jnp.int32)[None, :]
    bias_p = jnp.stack([_shifted_bias(rel, r - c, r >= c),
                        _shifted_bias(rel, blk + r - c, jnp.full((blk, blk), True))], axis=1)
    a_p = _attn_call(q_p.reshape(bsz, seq, d), kb_p.reshape(bsz, seq, d),
                     va_p.reshape(bsz, seq, 2 * d), bias_p, lamv, subg)
    y_prompt = back(a_p.reshape(bsz * seq, d), h_p).reshape(bsz, seq, d)
    conv_state_prompt = g_p.reshape(bsz, seq, d)[:, seq - hist:][None]

    def conv_s(g2):
        ext = jnp.concatenate([jnp.zeros((db, HIST_ROWS - hist, d), F32), state_conv[0].astype(F32),
                               g2.reshape(db, dseq, d)], axis=1)
        return _conv_short_call(ext, dw, dwb, ln_g, ln_b)

    g_s, h_s, k_s, v_s, q_s, _, _ = front(x_sample, conv_s)
    prow = page * N_HEADS
    pad_new = lambda a: jnp.pad(a.reshape(db, dseq, N_HEADS, V_HEAD_DIM).astype(BF16),
                                ((0, 0), (0, page - dseq), (0, 0), (0, 0))).reshape(db, prow, V_HEAD_DIM)
    n_rows = 2 * N_HEADS * dseq
    row_q = jnp.tile(jnp.arange(dseq, dtype=jnp.int32), 2 * N_HEADS)[:, None]
    row_h = jnp.repeat(jnp.arange(N_HEADS, dtype=jnp.int32), 2 * dseq)
    key = (jnp.arange(prow, dtype=jnp.int32) // N_HEADS)[None, :]
    same_head = (jnp.arange(prow, dtype=jnp.int32) % N_HEADS)[None, :] == row_h[:, None]
    pick = lambda b4: jnp.concatenate([b4[h, 2 * h * dseq:2 * (h + 1) * dseq] for h in range(N_HEADS)], axis=0)
    mask = jnp.where(same_head, 0.0, NEG_INF).astype(F32)
    blast = jnp.where(same_head, pick(_shifted_bias(rel, page + row_q - key, jnp.full((n_rows, prow), True))), 0.0)
    bnew = pick(_shifted_bias(rel, row_q - key, same_head & (key <= row_q) & (key < dseq)))
    a_s = _decode_call(page_table.astype(jnp.int32), q_s.reshape(db, dseq, d),
                       cache_k.reshape(n_pool, prow, V_HEAD_DIM), cache_v.reshape(n_pool, prow, V_HEAD_DIM),
                       pad_new(k_s), pad_new(v_s), mask, blast, bnew, lamv, subg)
    y_sample = back(a_s.reshape(db * dseq, d), h_s).reshape(db, dseq, d)
    conv_state_sample = jnp.concatenate([state_conv[0].astype(F32), g_s.reshape(db, dseq, d)],
                                        axis=1)[:, dseq:][None]

    kv_shape = (N_HEADS, V_HEAD_DIM)
    return (y_prompt, y_sample, conv_state_prompt, conv_state_sample,
            k_p.reshape(bsz, seq, *kv_shape), v_p.reshape(bsz, seq, *kv_shape),
            k_s.reshape(db, dseq, *kv_shape), v_s.reshape(db, dseq, *kv_shape))
```

```python
import functools
import math

import jax
import jax.numpy as jnp
from jax import lax
from jax.experimental import pallas as pl
from jax.experimental.pallas import tpu as pltpu

F32 = jnp.float32
BF16 = jnp.bfloat16

N_HEADS = 8
HEAD_DIM = 64
V_HEAD_DIM = 2 * HEAD_DIM
CONV_WIDTH = 31
N_BUCKETS = 32
MAX_DISTANCE = 128
NORM_EPS = 1e-6
NEG_INF = -1e30
LAM_INIT = 0.8 - 0.6 * math.exp(-0.3 * 1)
LOG2E = math.log2(math.e)
Q_SCALE = HEAD_DIM ** -0.5 * LOG2E

SUBLANES = 8
LANES = 128
HIST_ROWS = 32
VMEM_LIMIT = 56 * 1024 * 1024

TOKEN_TILE = 512
FF_CHUNK = 512
CONV_ROWS = 32
ATTN_BLOCK = 512
PAGES_PER_STEP = 8
DECODE_GROUP = 4


def _params(n_axes):
    return pltpu.CompilerParams(dimension_semantics=("arbitrary",) * n_axes,
                                vmem_limit_bytes=VMEM_LIMIT)


def _resident(shape):
    nd = len(shape)
    return pl.BlockSpec(shape, lambda *_: (0,) * nd, pipeline_mode=pl.Buffered(1))


def _rms_unit(x):
    return x * lax.rsqrt(jnp.mean(x * x, axis=-1, keepdims=True) + NORM_EPS)


def _mlp(h, g_ref, w1_ref, w2_ref):
    hn = (_rms_unit(h) * g_ref[...]).astype(BF16)
    d_ff = w1_ref.shape[1]
    acc = h
    for c in range(d_ff // FF_CHUNK):
        u = jnp.dot(hn, w1_ref[:, c * FF_CHUNK:(c + 1) * FF_CHUNK], preferred_element_type=F32)
        u = jnp.maximum(u, 0.0)
        acc = acc + jnp.dot((u * u).astype(BF16), w2_ref[c * FF_CHUNK:(c + 1) * FF_CHUNK, :],
                            preferred_element_type=F32)
    return acc


def _glu_kernel(x_ref, g_ref, w_ref, b_ref, o_ref):
    d = x_ref.shape[1]
    hn = (_rms_unit(x_ref[...]) * g_ref[...]).astype(BF16)
    u = jnp.dot(hn, w_ref[...], preferred_element_type=F32) + b_ref[...]
    o_ref[...] = u[:, :d] * jax.nn.sigmoid(u[:, d:])


def _glu_call(x2, g, w, b):
    n, d = x2.shape
    tm = min(TOKEN_TILE, n)
    return pl.pallas_call(
        _glu_kernel,
        grid=(n // tm,),
        in_specs=[pl.BlockSpec((tm, d), lambda i: (i, 0)),
                  _resident((1, d)), _resident((d, 2 * d)), _resident((1, 2 * d))],
        out_specs=pl.BlockSpec((tm, d), lambda i: (i, 0)),
        out_shape=jax.ShapeDtypeStruct((n, d), F32),
        compiler_params=_params(1),
        name="glu",
    )(x2, g, w, b)


def _ln_swish(c, ln_g_ref, ln_b_ref):
    mu = jnp.mean(c, axis=-1, keepdims=True)
    cc = c - mu
    var = jnp.mean(cc * cc, axis=-1, keepdims=True)
    y = cc * lax.rsqrt(var + NORM_EPS) * ln_g_ref[...] + ln_b_ref[...]
    return y * jax.nn.sigmoid(y)


def _conv_long_kernel(g_ref, hist_ref, dw_ref, dwb_ref, ln_g_ref, ln_b_ref, o_ref, sh_ref):
    tl = g_ref.shape[1]
    first = pl.program_id(1) == 0
    hist = jnp.where(first, 0.0, hist_ref[0])
    sh_ref[0, 0:HIST_ROWS, :] = hist
    sh_ref[0, HIST_ROWS:HIST_ROWS + tl, :] = g_ref[0]
    n_sh = tl + HIST_ROWS - SUBLANES
    for b in range(1, SUBLANES):
        sh_ref[b, 0:n_sh, :] = sh_ref[0, b:b + n_sh, :]

    def chunk(ci, carry):
        r0 = pl.multiple_of(ci * CONV_ROWS, CONV_ROWS)
        acc = jnp.broadcast_to(dwb_ref[...], (CONV_ROWS, dwb_ref.shape[1]))
        for k in range(CONV_WIDTH):
            off = HIST_ROWS - (CONV_WIDTH - 1) + k
            a, b = divmod(off, SUBLANES)
            acc = acc + dw_ref[k:k + 1, :] * sh_ref[b, pl.ds(r0 + a * SUBLANES, CONV_ROWS), :]
        o_ref[0, pl.ds(r0, CONV_ROWS), :] = _ln_swish(acc, ln_g_ref, ln_b_ref).astype(o_ref.dtype)
        return carry

    lax.fori_loop(0, tl // CONV_ROWS, chunk, 0, unroll=2)


def _conv_long_call(g3, dw, dwb, ln_g, ln_b):
    bsz, t, d = g3.shape
    tl = min(TOKEN_TILE, t)
    per = tl // HIST_ROWS
    return pl.pallas_call(
        _conv_long_kernel,
        grid=(bsz, t // tl),
        in_specs=[pl.BlockSpec((1, tl, d), lambda b, j: (b, j, 0)),
                  pl.BlockSpec((1, HIST_ROWS, d), lambda b, j: (b, jnp.maximum(j * per - 1, 0), 0)),
                  _resident((HIST_ROWS, d)), _resident((1, d)), _resident((1, d)), _resident((1, d))],
        out_specs=pl.BlockSpec((1, tl, d), lambda b, j: (b, j, 0)),
        out_shape=jax.ShapeDtypeStruct((bsz, t, d), BF16),
        scratch_shapes=[pltpu.VMEM((SUBLANES, tl + HIST_ROWS, d), F32)],
        compiler_params=_params(2),
        name="conv_long",
    )(g3, g3, dw, dwb, ln_g, ln_b)


def _conv_short_kernel(ext_ref, dw_ref, dwb_ref, ln_g_ref, ln_b_ref, o_ref):
    s, rows, d = ext_ref.shape
    t = rows - HIST_ROWS
    acc = jnp.broadcast_to(dwb_ref[...].reshape(1, 1, d), (s, t, d))
    for k in range(CONV_WIDTH):
        off = HIST_ROWS - (CONV_WIDTH - 1) + k
        acc = acc + dw_ref[k:k + 1, :].reshape(1, 1, d) * ext_ref[:, off:off + t, :]
    o_ref[...] = _ln_swish(acc.reshape(s * t, d), ln_g_ref, ln_b_ref).astype(o_ref.dtype)


def _conv_short_call(ext3, dw, dwb, ln_g, ln_b):
    s, rows, d = ext3.shape
    t = rows - HIST_ROWS
    return pl.pallas_call(
        _conv_short_kernel,
        grid=(1,),
        in_specs=[_resident((s, rows, d)), _resident((HIST_ROWS, d)),
                  _resident((1, d)), _resident((1, d)), _resident((1, d))],
        out_specs=pl.BlockSpec((s * t, d), lambda i: (0, 0)),
        out_shape=jax.ShapeDtypeStruct((s * t, d), BF16),
        compiler_params=_params(1),
        name="conv_short",
    )(ext3, dw, dwb, ln_g, ln_b)


def _pw2_mlp_kernel(c_ref, x_ref, pw2_ref, pw2b_ref, gm_ref, w1_ref, w2_ref, o_ref):
    h = x_ref[...] + jnp.dot(c_ref[...], pw2_ref[...], preferred_element_type=F32) + pw2b_ref[...]
    o_ref[...] = _mlp(h, gm_ref, w1_ref, w2_ref)


def _pw2_mlp_call(c2, x2, pw2, pw2b, gm, w1, w2):
    n, d = x2.shape
    tm = min(TOKEN_TILE, n)
    row = lambda i: (i, 0)
    return pl.pallas_call(
        _pw2_mlp_kernel,
        grid=(n // tm,),
        in_specs=[pl.BlockSpec((tm, d), row), pl.BlockSpec((tm, d), row),
                  _resident((d, d)), _resident((1, d)), _resident((1, d)),
                  _resident(w1.shape), _resident(w2.shape)],
        out_specs=pl.BlockSpec((tm, d), row),
        out_shape=jax.ShapeDtypeStruct((n, d), F32),
        compiler_params=_params(1),
        name="pw2_mlp",
    )(c2, x2, pw2, pw2b, gm, w1, w2)


def _qkv_kernel(h_ref, gkv_ref, gq_ref, wk_ref, wv_ref, wq_ref,
                k_ref, v_ref, q_ref, kb_ref, va_ref):
    tm, d = h_ref.shape
    y = _rms_unit(h_ref[...])
    kvn = (y * gkv_ref[...]).astype(BF16)
    hn = (y * gq_ref[...]).astype(BF16)
    k = jnp.dot(kvn, wk_ref[...], preferred_element_type=F32)
    v = jnp.dot(kvn, wv_ref[...], preferred_element_type=F32)
    q = jnp.dot(hn, wq_ref[...], preferred_element_type=F32)
    k_ref[...] = k
    v_ref[...] = v
    q_ref[...] = (q * Q_SCALE).astype(BF16)
    kb_ref[...] = k.astype(BF16)
    ones_col = (lax.broadcasted_iota(jnp.int32, (tm, V_HEAD_DIM), 1) == 0).astype(BF16)
    vb = v.astype(BF16)
    for h in range(N_HEADS):
        va_ref[:, 2 * h * V_HEAD_DIM:(2 * h + 1) * V_HEAD_DIM] = vb[:, h * V_HEAD_DIM:(h + 1) * V_HEAD_DIM]
        va_ref[:, (2 * h + 1) * V_HEAD_DIM:(2 * h + 2) * V_HEAD_DIM] = ones_col


def _qkv_call(h2, gkv, gq, wk, wv, wq):
    n, d = h2.shape
    tm = min(TOKEN_TILE, n)
    row = lambda i: (i, 0)
    return pl.pallas_call(
        _qkv_kernel,
        grid=(n // tm,),
        in_specs=[pl.BlockSpec((tm, d), row), _resident((1, d)), _resident((1, d)),
                  _resident((d, d)), _resident((d, d)), _resident((d, d))],
        out_specs=[pl.BlockSpec((tm, d), row), pl.BlockSpec((tm, d), row), pl.BlockSpec((tm, d), row),
                   pl.BlockSpec((tm, d), row),
                   pl.BlockSpec((tm, 2 * d), row)],
        out_shape=[jax.ShapeDtypeStruct((n, d), F32), jax.ShapeDtypeStruct((n, d), F32),
                   jax.ShapeDtypeStruct((n, d), BF16), jax.ShapeDtypeStruct((n, d), BF16),
                   jax.ShapeDtypeStruct((n, 2 * d), BF16)],
        compiler_params=_params(1),
        name="qkv",
    )(h2, gkv, gq, wk, wv, wq)


def _lambda(lam_ref):
    lv = lam_ref[...]
    s1 = jnp.sum(lv[0:1, :] * lv[1:2, :], axis=-1, keepdims=True)
    s2 = jnp.sum(lv[2:3, :] * lv[3:4, :], axis=-1, keepdims=True)
    return jnp.exp(s1) - jnp.exp(s2) + LAM_INIT


def _sub_norm(o, subg_ref):
    return (_rms_unit(o) * subg_ref[...]) * (1.0 - LAM_INIT)


def _rel_bucket(n):
    max_exact = N_BUCKETS // 2
    nf = jnp.maximum(n, 1).astype(F32)
    large = max_exact + (jnp.log(nf / max_exact) / math.log(MAX_DISTANCE / max_exact)
                         * (N_BUCKETS - max_exact)).astype(jnp.int32)
    large = jnp.minimum(large, N_BUCKETS - 1)
    return jnp.where(n < max_exact, n, large)


def _shifted_bias(rel_table, dist, valid):
    bucket = _rel_bucket(jnp.maximum(dist, 0))[None]
    lead = (N_HEADS,) + (1,) * dist.ndim
    vals = jnp.zeros((N_HEADS,) + dist.shape, F32)
    for n in range(N_BUCKETS - 1):
        vals = jnp.where(bucket == n, (rel_table[n] - rel_table[N_BUCKETS - 1]).reshape(lead), vals)
    return jnp.where(valid[None], vals * LOG2E, NEG_INF)


def _attn_kernel(q_ref, k_ref, va_ref, bias_ref, lam_ref, subg_ref, o_ref, qz_ref, acc_ref, m_ref):
    blk = q_ref.shape[1]
    i = pl.program_id(2)
    q = q_ref[0]
    first_half = lax.broadcasted_iota(jnp.int32, q.shape, 1) < HEAD_DIM
    zero = jnp.zeros_like(q)
    qz_ref[0:blk, :] = jnp.where(first_half, q, zero)
    qz_ref[blk:2 * blk, :] = jnp.where(first_half, zero, q)
    acc_ref[...] = jnp.zeros_like(acc_ref)
    m_ref[...] = jnp.full_like(m_ref, NEG_INF)
    reps = blk // LANES

    nt = (((1,), (1,)), ((), ()))
    halves = [slice(c * blk, (c + 1) * blk) for c in range(2)]

    def scores(j):
        kj = k_ref[0, pl.ds(pl.multiple_of(j * blk, blk), blk), :]
        return [lax.dot_general(qz_ref[h, :], kj, nt, preferred_element_type=F32) for h in halves]

    def accumulate(j, s, bias):
        va = va_ref[0, pl.ds(pl.multiple_of(j * blk, blk), blk), :]
        if bias is not None:
            s = [sc + bias_ref[0, bias] for sc in s]
        m_prev = [m_ref[h, :] for h in halves]
        m_new = [jnp.maximum(mp, jnp.max(sc, axis=1, keepdims=True)) for mp, sc in zip(m_prev, s)]
        alpha = [jnp.exp2(mp - mn) for mp, mn in zip(m_prev, m_new)]
        p = [jnp.exp2(sc - jnp.concatenate([mn] * reps, axis=1)).astype(BF16) for sc, mn in zip(s, m_new)]
        pv = [jnp.dot(pc, va, preferred_element_type=F32) for pc in p]
        for h, al, pvc, mn in zip(halves, alpha, pv, m_new):
            acc_ref[h, :] = jnp.concatenate([al, al], axis=1) * acc_ref[h, :] + pvc
            m_ref[h, :] = mn

    def process(blocks):
        s_all = [scores(j) for j, _ in blocks]
        for (j, bias), s in zip(blocks, s_all):
            accumulate(j, s, bias)

    n_far = jnp.maximum(i - 1, 0)

    def far_pair(t, carry):
        process([(2 * t, None), (2 * t + 1, None)])
        return carry

    lax.fori_loop(0, n_far // 2, far_pair, 0)
    odd = n_far % 2 == 1
    diag, prev = 0, 1

    @pl.when(i == 0)
    def _():
        process([(i, diag)])

    @pl.when(jnp.logical_and(i > 0, jnp.logical_not(odd)))
    def _():
        process([(i - 1, prev), (i, diag)])

    @pl.when(odd)
    def _():
        process([(i - 2, None), (i - 1, prev), (i, diag)])

    lam = _lambda(lam_ref)
    a1 = acc_ref[0:blk, :]
    a2 = acc_ref[blk:2 * blk, :]
    o = (a1[:, :V_HEAD_DIM] / a1[:, V_HEAD_DIM:V_HEAD_DIM + 1]
         - lam * (a2[:, :V_HEAD_DIM] / a2[:, V_HEAD_DIM:V_HEAD_DIM + 1]))
    o_ref[0] = _sub_norm(o, subg_ref).astype(o_ref.dtype)


def _attn_call(q3, k3, va3, bias, lamv, subg):
    bsz, t, d = q3.shape
    blk = min(ATTN_BLOCK, t)
    return pl.pallas_call(
        _attn_kernel,
        grid=(bsz, N_HEADS, t // blk),
        in_specs=[pl.BlockSpec((1, blk, V_HEAD_DIM), lambda b, h, i: (b, i, h)),
                  pl.BlockSpec((1, t, V_HEAD_DIM), lambda b, h, i: (b, 0, h)),
                  pl.BlockSpec((1, t, 2 * V_HEAD_DIM), lambda b, h, i: (b, 0, h)),
                  pl.BlockSpec((1, 2, blk, blk), lambda b, h, i: (h, 0, 0, 0)),
                  _resident((SUBLANES, LANES)), _resident((1, V_HEAD_DIM))],
        out_specs=pl.BlockSpec((1, blk, V_HEAD_DIM), lambda b, h, i: (b, i, h)),
        out_shape=jax.ShapeDtypeStruct((bsz, t, d), BF16),
        scratch_shapes=[pltpu.VMEM((2 * blk, V_HEAD_DIM), BF16),
                        pltpu.VMEM((2 * blk, 2 * V_HEAD_DIM), F32),
                        pltpu.VMEM((2 * blk, LANES), F32)],
        compiler_params=_params(3),
        name="attn_prompt",
    )(q3, k3, va3, bias, lamv, subg)


def _decode_kernel(pt_ref, q_ref, *refs, n_pages_step):
    g_n = n_pages_step
    k_refs = refs[:g_n]
    v_refs = refs[g_n:2 * g_n]
    (knew_ref, vnew_ref, blast_ref, bnew_ref, lam_ref, subg_ref,
     o_ref, qbd_ref, acc_ref, m_ref, l_ref) = refs[2 * g_n:]
    st = pl.program_id(1)
    last = pl.num_programs(1) - 1
    tq = q_ref.shape[1]
    cols, d = qbd_ref.shape
    page = blast_ref.shape[0]

    @pl.when(st == 0)
    def _():
        qt = jnp.concatenate([q_ref[0]] * (cols // tq), axis=0)
        row = lax.broadcasted_iota(jnp.int32, (cols, d), 0)
        lane = lax.broadcasted_iota(jnp.int32, (cols, d), 1)
        qbd_ref[...] = jnp.where(row // tq == lane // HEAD_DIM, qt, jnp.zeros_like(qt))
        acc_ref[...] = jnp.zeros_like(acc_ref)
        m_ref[...] = jnp.full_like(m_ref, NEG_INF)
        l_ref[...] = jnp.zeros_like(l_ref)

    def slab(ref):
        return jnp.concatenate([ref[0, pl.ds(h, page, stride=N_HEADS), :] for h in range(N_HEADS)],
                               axis=1).astype(BF16)

    def per_row(v):
        return jnp.broadcast_to(v, (cols, cols)).T

    nt = (((1,), (1,)), ((), ()))
    tn = (((0,), (0,)), ((), ()))

    def scores(k):
        return lax.dot_general(k, qbd_ref[...], nt, preferred_element_type=F32)

    def update(s, v):
        m_prev = m_ref[...]
        m_new = jnp.maximum(m_prev, jnp.max(s, axis=0, keepdims=True))
        alpha = jnp.exp2(m_prev - m_new)
        p = jnp.exp2(s - m_new)
        l_ref[...] = alpha * l_ref[...] + jnp.sum(p, axis=0, keepdims=True)
        pv = lax.dot_general(p.astype(BF16), v, tn, preferred_element_type=F32)
        acc_ref[...] = jnp.concatenate([per_row(alpha)] * (d // cols), axis=1) * acc_ref[...] + pv
        m_ref[...] = m_new

    sub = math.gcd(DECODE_GROUP, g_n)
    groups = [slice(a, a + sub) for a in range(0, g_n, sub)]
    s_all = [scores(jnp.concatenate([slab(r) for r in k_refs[gs]], axis=0)) for gs in groups]
    is_last = (st == last).astype(F32)
    s_all[-1] = s_all[-1] + jnp.concatenate(
        [jnp.zeros(((sub - 1) * page, cols), F32), is_last * blast_ref[...]], axis=0)
    for gs, s in zip(groups, s_all):
        update(s, jnp.concatenate([slab(r) for r in v_refs[gs]], axis=0))

    @pl.when(st == last)
    def _():
        update(scores(knew_ref[0]) + bnew_ref[...], vnew_ref[0])
        lam = _lambda(lam_ref)
        o_all = acc_ref[...] / jnp.concatenate([per_row(l_ref[...])] * (d // cols), axis=1)
        for h in range(N_HEADS):
            blk = o_all[2 * h * tq:(2 * h + 2) * tq, h * V_HEAD_DIM:(h + 1) * V_HEAD_DIM]
            o = blk[:tq] - lam * blk[tq:]
            o_ref[0, :, h * V_HEAD_DIM:(h + 1) * V_HEAD_DIM] = _sub_norm(o, subg_ref).astype(o_ref.dtype)


def _decode_call(page_table, q3, cache_k3, cache_v3, knew, vnew, blast, bnew, lamv, subg):
    db, n_pages = page_table.shape
    _, tq, d = q3.shape
    prow = cache_k3.shape[1]
    page = prow // N_HEADS
    g_n = math.gcd(PAGES_PER_STEP, n_pages)
    cols = N_HEADS * 2 * tq

    def page_spec(g):
        return pl.BlockSpec((1, prow, V_HEAD_DIM), lambda b, s, pt: (pt[b, s * g_n + g], 0, 0))

    per_b = lambda b, s, pt: (b, 0, 0)
    const2 = lambda b, s, pt: (0, 0)
    tile = pl.BlockSpec((page, cols), const2)
    grid_spec = pltpu.PrefetchScalarGridSpec(
        num_scalar_prefetch=1,
        grid=(db, n_pages // g_n),
        in_specs=([pl.BlockSpec((1, tq, d), per_b)]
                  + [page_spec(g) for g in range(g_n)] + [page_spec(g) for g in range(g_n)]
                  + [pl.BlockSpec((1, page, d), per_b), pl.BlockSpec((1, page, d), per_b),
                     tile, tile,
                     pl.BlockSpec((SUBLANES, LANES), const2), pl.BlockSpec((1, V_HEAD_DIM), const2)]),
        out_specs=pl.BlockSpec((1, tq, d), per_b),
        scratch_shapes=[pltpu.VMEM((cols, d), BF16), pltpu.VMEM((cols, d), F32),
                        pltpu.VMEM((1, cols), F32), pltpu.VMEM((1, cols), F32)],
    )
    return pl.pallas_call(
        functools.partial(_decode_kernel, n_pages_step=g_n),
        grid_spec=grid_spec,
        out_shape=jax.ShapeDtypeStruct((db, tq, d), BF16),
        compiler_params=_params(2),
        name="attn_decode",
    )(page_table, q3, *([cache_k3] * g_n), *([cache_v3] * g_n), knew, vnew, blast, bnew, lamv, subg)


def _post_kernel(a_ref, h_ref, wo_ref, gm_ref, w1_ref, w2_ref, gf_ref, o_ref):
    h = h_ref[...] + jnp.dot(a_ref[...], wo_ref[...], preferred_element_type=F32)
    h = _mlp(h, gm_ref, w1_ref, w2_ref)
    o_ref[...] = _rms_unit(h) * gf_ref[...]


def _post_call(a2, h2, wo, gm, w1, w2, gf):
    n, d = h2.shape
    tm = min(TOKEN_TILE, n)
    row = lambda i: (i, 0)
    return pl.pallas_call(
        _post_kernel,
        grid=(n // tm,),
        in_specs=[pl.BlockSpec((tm, d), row), pl.BlockSpec((tm, d), row),
                  _resident((d, d)), _resident((1, d)), _resident(w1.shape), _resident(w2.shape),
                  _resident((1, d))],
        out_specs=pl.BlockSpec((tm, d), row),
        out_shape=jax.ShapeDtypeStruct((n, d), F32),
        compiler_params=_params(1),
        name="post",
    )(a2, h2, wo, gm, w1, w2, gf)


def _row(v):
    return v.reshape(1, -1).astype(F32)


def kernel(x_prompt, x_sample, state_conv, cache_k, cache_v, page_table, norm_mix_g, norm_mlp_g, conv_pw1_w, conv_pw1_b, conv_dw_w, conv_dw_b, conv_ln_g, conv_ln_b, conv_pw2_w, conv_pw2_b, kv_norm_g, w_k, w_v, w_q, lambda_q1, lambda_k1, lambda_q2, lambda_k2, subln_g, w_o, rel_bias_table, mlp_w1, mlp_w2, final_norm_g):
    bsz, seq, d = x_prompt.shape
    db, dseq, _ = x_sample.shape
    n_pool, page = cache_k.shape[0], cache_k.shape[1]
    n_pages = page_table.shape[1]
    past_len = n_pages * page
    hist = CONV_WIDTH - 1

    pw1 = conv_pw1_w[0].astype(BF16)
    pw1b = _row(conv_pw1_b[0])
    dw = jnp.pad(conv_dw_w[0].astype(F32), ((0, HIST_ROWS - CONV_WIDTH), (0, 0)))
    dwb, ln_g, ln_b = _row(conv_dw_b[0]), _row(conv_ln_g[0]), _row(conv_ln_b[0])
    pw2 = conv_pw2_w[0].astype(BF16)
    pw2b = _row(conv_pw2_b[0])
    w1 = mlp_w1.astype(BF16)
    w2 = mlp_w2.astype(BF16)
    wk, wv, wq, wo = w_k.astype(BF16), w_v.astype(BF16), w_q[0].astype(BF16), w_o[0].astype(BF16)
    lamv = jnp.pad(jnp.stack([lambda_q1[0], lambda_k1[0], lambda_q2[0], lambda_k2[0]]).astype(F32),
                   ((0, SUBLANES - 4), (0, LANES - HEAD_DIM)))
    subg = _row(subln_g[0])
    rel = rel_bias_table.astype(F32)

    def front(x3, conv_fn):
        n = x3.shape[0] * x3.shape[1]
        x2 = x3.reshape(n, d)
        g2 = _glu_call(x2, _row(norm_mix_g[0]), pw1, pw1b)
        c2 = conv_fn(g2)
        h2 = _pw2_mlp_call(c2, x2, pw2, pw2b, _row(norm_mlp_g[0]), w1[0], w2[0])
        k2, v2, q2, kb, va = _qkv_call(h2, _row(kv_norm_g), _row(norm_mix_g[1]), wk, wv, wq)
        return g2, h2, k2, v2, q2, kb, va

    def back(a2, h2):
        return _post_call(a2, h2, wo, _row(norm_mlp_g[1]), w1[1], w2[1], _row(final_norm_g))

    conv_p = lambda g2: _conv_long_call(g2.reshape(bsz, seq, d), dw, dwb, ln_g, ln_b).reshape(bsz * seq, d)
    g_p, h_p, k_p, v_p, q_p, kb_p, va_p = front(x_prompt, conv_p)
    blk = min(ATTN_BLOCK, seq)
    r = jnp.arange(blk, dtype=jnp.int32)[:, None]
    c = jnp.arange(blk, dtype=jnp.int32)[None, :]
    bias_p = jnp.stack([_shifted_bias(rel, r - c, r >= c),
                        _shifted_bias(rel, blk + r - c, jnp.full((blk, blk), True))], axis=1)
    a_p = _attn_call(q_p.reshape(bsz, seq, d), kb_p.reshape(bsz, seq, d),
                     va_p.reshape(bsz, seq, 2 * d), bias_p, lamv, subg)
    y_prompt = back(a_p.reshape(bsz * seq, d), h_p).reshape(bsz, seq, d)
    conv_state_prompt = g_p.reshape(bsz, seq, d)[:, seq - hist:][None]

    def conv_s(g2):
        ext = jnp.concatenate([jnp.zeros((db, HIST_ROWS - hist, d), F32), state_conv[0].astype(F32),
                               g2.reshape(db, dseq, d)], axis=1)
        return _conv_short_call(ext, dw, dwb, ln_g, ln_b)

    g_s, h_s, k_s, v_s, q_s, _, _ = front(x_sample, conv_s)
    prow = page * N_HEADS
    n_cols = 2 * N_HEADS * dseq
    assert n_cols == LANES and dseq <= page
    pad_new = lambda a: jnp.pad(a.reshape(db, dseq, d).astype(BF16), ((0, 0), (0, page - dseq), (0, 0)))
    col_q = jnp.tile(jnp.arange(dseq, dtype=jnp.int32), 2 * N_HEADS)[None, :]
    key = jnp.arange(page, dtype=jnp.int32)[:, None]
    pick = lambda b4: jnp.concatenate([b4[h, :, 2 * h * dseq:2 * (h + 1) * dseq] for h in range(N_HEADS)], axis=1)
    blast = pick(_shifted_bias(rel, page + col_q - key, jnp.full((page, n_cols), True)))
    bnew = pick(_shifted_bias(rel, col_q - key, (key <= col_q) & (key < dseq)))
    a_s = _decode_call(page_table.astype(jnp.int32), q_s.reshape(db, dseq, d),
                       cache_k.reshape(n_pool, prow, V_HEAD_DIM), cache_v.reshape(n_pool, prow, V_HEAD_DIM),
                       pad_new(k_s), pad_new(v_s), blast, bnew, lamv, subg)
    y_sample = back(a_s.reshape(db * dseq, d), h_s).reshape(db, dseq, d)
    conv_state_sample = jnp.concatenate([state_conv[0].astype(F32), g_s.reshape(db, dseq, d)],
                                        axis=1)[:, dseq:][None]

    kv_shape = (N_HEADS, V_HEAD_DIM)
    return (y_prompt, y_sample, conv_state_prompt, conv_state_sample,
            k_p.reshape(bsz, seq, *kv_shape), v_p.reshape(bsz, seq, *kv_shape),
            k_s.reshape(db, dseq, *kv_shape), v_s.reshape(db, dseq, *kv_shape))
```

```python
import functools
import math

import jax
import jax.numpy as jnp
from jax import lax
from jax.experimental import pallas as pl
from jax.experimental.pallas import tpu as pltpu

F32 = jnp.float32
BF16 = jnp.bfloat16

N_HEADS = 8
HEAD_DIM = 64
V_HEAD_DIM = 2 * HEAD_DIM
CONV_WIDTH = 31
N_BUCKETS = 32
MAX_DISTANCE = 128
NORM_EPS = 1e-6
NEG_INF = -1e30
LAM_INIT = 0.8 - 0.6 * math.exp(-0.3 * 1)
LOG2E = math.log2(math.e)
Q_SCALE = HEAD_DIM ** -0.5 * LOG2E

SUBLANES = 8
LANES = 128
HIST_ROWS = 32
VMEM_LIMIT = 56 * 1024 * 1024

TOKEN_TILE = 512
FF_CHUNK = 512
CONV_ROWS = 32
ATTN_BLOCK = 512
PAGES_PER_STEP = 16
DECODE_GROUP = 4


def _params(n_axes):
    return pltpu.CompilerParams(dimension_semantics=("arbitrary",) * n_axes,
                                vmem_limit_bytes=VMEM_LIMIT)


def _resident(shape):
    nd = len(shape)
    return pl.BlockSpec(shape, lambda *_: (0,) * nd, pipeline_mode=pl.Buffered(1))


def _rms_unit(x):
    return x * lax.rsqrt(jnp.mean(x * x, axis=-1, keepdims=True) + NORM_EPS)


def _mlp(h, g_ref, w1_ref, w2_ref):
    hn = (_rms_unit(h) * g_ref[...]).astype(BF16)
    d_ff = w1_ref.shape[1]
    acc = h
    for c in range(d_ff // FF_CHUNK):
        u = jnp.dot(hn, w1_ref[:, c * FF_CHUNK:(c + 1) * FF_CHUNK], preferred_element_type=F32)
        u = jnp.maximum(u, 0.0)
        acc = acc + jnp.dot((u * u).astype(BF16), w2_ref[c * FF_CHUNK:(c + 1) * FF_CHUNK, :],
                            preferred_element_type=F32)
    return acc


def _glu(x, g_ref, w_ref, b_ref):
    d = x.shape[1]
    hn = (_rms_unit(x) * g_ref[...]).astype(BF16)
    u = jnp.dot(hn, w_ref[...], preferred_element_type=F32) + b_ref[...]
    return u[:, :d] * jax.nn.sigmoid(u[:, d:])


def _glu_kernel(x_ref, g_ref, w_ref, b_ref, o_ref):
    o_ref[...] = _glu(x_ref[...], g_ref, w_ref, b_ref)


def _glu_call(x2, g, w, b):
    n, d = x2.shape
    tm = min(TOKEN_TILE, n)
    return pl.pallas_call(
        _glu_kernel,
        grid=(n // tm,),
        in_specs=[pl.BlockSpec((tm, d), lambda i: (i, 0)),
                  _resident((1, d)), _resident((d, 2 * d)), _resident((1, 2 * d))],
        out_specs=pl.BlockSpec((tm, d), lambda i: (i, 0)),
        out_shape=jax.ShapeDtypeStruct((n, d), F32),
        compiler_params=_params(1),
        name="glu",
    )(x2, g, w, b)


def _ln_swish(c, ln_g_ref, ln_b_ref):
    mu = jnp.mean(c, axis=-1, keepdims=True)
    cc = c - mu
    var = jnp.mean(cc * cc, axis=-1, keepdims=True)
    y = cc * lax.rsqrt(var + NORM_EPS) * ln_g_ref[...] + ln_b_ref[...]
    return y * jax.nn.sigmoid(y)


def _conv_long_kernel(g_ref, hist_ref, dw_ref, dwb_ref, ln_g_ref, ln_b_ref, o_ref, sh_ref):
    tl = g_ref.shape[1]
    first = pl.program_id(1) == 0
    hist = jnp.where(first, 0.0, hist_ref[0])
    sh_ref[0, 0:HIST_ROWS, :] = hist
    sh_ref[0, HIST_ROWS:HIST_ROWS + tl, :] = g_ref[0]
    n_sh = tl + HIST_ROWS - SUBLANES
    for b in range(1, SUBLANES):
        sh_ref[b, 0:n_sh, :] = sh_ref[0, b:b + n_sh, :]

    def chunk(ci, carry):
        r0 = pl.multiple_of(ci * CONV_ROWS, CONV_ROWS)
        acc = jnp.broadcast_to(dwb_ref[...], (CONV_ROWS, dwb_ref.shape[1]))
        for k in range(CONV_WIDTH):
            off = HIST_ROWS - (CONV_WIDTH - 1) + k
            a, b = divmod(off, SUBLANES)
            acc = acc + dw_ref[k:k + 1, :] * sh_ref[b, pl.ds(r0 + a * SUBLANES, CONV_ROWS), :]
        o_ref[0, pl.ds(r0, CONV_ROWS), :] = _ln_swish(acc, ln_g_ref, ln_b_ref).astype(o_ref.dtype)
        return carry

    lax.fori_loop(0, tl // CONV_ROWS, chunk, 0, unroll=4)


def _conv_long_call(g3, dw, dwb, ln_g, ln_b):
    bsz, t, d = g3.shape
    tl = min(TOKEN_TILE, t)
    per = tl // HIST_ROWS
    return pl.pallas_call(
        _conv_long_kernel,
        grid=(bsz, t // tl),
        in_specs=[pl.BlockSpec((1, tl, d), lambda b, j: (b, j, 0)),
                  pl.BlockSpec((1, HIST_ROWS, d), lambda b, j: (b, jnp.maximum(j * per - 1, 0), 0)),
                  _resident((HIST_ROWS, d)), _resident((1, d)), _resident((1, d)), _resident((1, d))],
        out_specs=pl.BlockSpec((1, tl, d), lambda b, j: (b, j, 0)),
        out_shape=jax.ShapeDtypeStruct((bsz, t, d), BF16),
        scratch_shapes=[pltpu.VMEM((SUBLANES, tl + HIST_ROWS, d), F32)],
        compiler_params=_params(2),
        name="conv_long",
    )(g3, g3, dw, dwb, ln_g, ln_b)


def _short_front_kernel(x_ref, st_ref, g0_ref, pw1_ref, pw1b_ref, dw_ref, dwb_ref, ln_g_ref, ln_b_ref,
                        pw2_ref, pw2b_ref, gm_ref, w1_ref, w2_ref, gkv_ref, gq_ref, wk_ref, wv_ref, wq_ref,
                        g_ref, h_ref, k_ref, v_ref, q_ref, ext_ref):
    n, d = x_ref.shape
    s = st_ref.shape[0]
    t = n // s
    x = x_ref[...]
    g = _glu(x, g0_ref, pw1_ref, pw1b_ref)
    g_ref[...] = g
    ext_ref[:, 0:HIST_ROWS, :] = st_ref[...]
    ext_ref[:, HIST_ROWS:HIST_ROWS + t, :] = g.reshape(s, t, d)
    acc = jnp.broadcast_to(dwb_ref[...].reshape(1, 1, d), (s, t, d))
    for k in range(CONV_WIDTH):
        off = HIST_ROWS - (CONV_WIDTH - 1) + k
        acc = acc + dw_ref[k:k + 1, :].reshape(1, 1, d) * ext_ref[:, off:off + t, :]
    c = _ln_swish(acc.reshape(n, d), ln_g_ref, ln_b_ref).astype(BF16)
    h = x + jnp.dot(c, pw2_ref[...], preferred_element_type=F32) + pw2b_ref[...]
    h = _mlp(h, gm_ref, w1_ref, w2_ref)
    h_ref[...] = h
    k_ref[...], v_ref[...], q_ref[...] = _project_qkv(h, gkv_ref, gq_ref, wk_ref, wv_ref, wq_ref)


def _short_front_call(x2, st3, *params):
    n, d = x2.shape
    s = st3.shape[0]
    out = lambda dt: jax.ShapeDtypeStruct((n, d), dt)
    return pl.pallas_call(
        _short_front_kernel,
        grid=(1,),
        in_specs=[_resident(a.shape) for a in (x2, st3) + params],
        out_specs=[pl.BlockSpec((n, d), lambda i: (0, 0))] * 5,
        out_shape=[out(F32), out(F32), out(F32), out(F32), out(BF16)],
        scratch_shapes=[pltpu.VMEM((s, HIST_ROWS + n // s, d), F32)],
        compiler_params=_params(1),
        name="short_front",
    )(x2, st3, *params)


def _pw2_mlp_kernel(c_ref, x_ref, pw2_ref, pw2b_ref, gm_ref, w1_ref, w2_ref, o_ref):
    h = x_ref[...] + jnp.dot(c_ref[...], pw2_ref[...], preferred_element_type=F32) + pw2b_ref[...]
    o_ref[...] = _mlp(h, gm_ref, w1_ref, w2_ref)


def _pw2_mlp_call(c2, x2, pw2, pw2b, gm, w1, w2):
    n, d = x2.shape
    tm = min(TOKEN_TILE, n)
    row = lambda i: (i, 0)
    return pl.pallas_call(
        _pw2_mlp_kernel,
        grid=(n // tm,),
        in_specs=[pl.BlockSpec((tm, d), row), pl.BlockSpec((tm, d), row),
                  _resident((d, d)), _resident((1, d)), _resident((1, d)),
                  _resident(w1.shape), _resident(w2.shape)],
        out_specs=pl.BlockSpec((tm, d), row),
        out_shape=jax.ShapeDtypeStruct((n, d), F32),
        compiler_params=_params(1),
        name="pw2_mlp",
    )(c2, x2, pw2, pw2b, gm, w1, w2)


def _project_qkv(h, gkv_ref, gq_ref, wk_ref, wv_ref, wq_ref):
    y = _rms_unit(h)
    kvn = (y * gkv_ref[...]).astype(BF16)
    hn = (y * gq_ref[...]).astype(BF16)
    k = jnp.dot(kvn, wk_ref[...], preferred_element_type=F32)
    v = jnp.dot(kvn, wv_ref[...], preferred_element_type=F32)
    q = jnp.dot(hn, wq_ref[...], preferred_element_type=F32)
    return k, v, (q * Q_SCALE).astype(BF16)


def _qkv_kernel(h_ref, gkv_ref, gq_ref, wk_ref, wv_ref, wq_ref,
                k_ref, v_ref, q_ref, kb_ref, va_ref):
    tm, d = h_ref.shape
    k, v, q = _project_qkv(h_ref[...], gkv_ref, gq_ref, wk_ref, wv_ref, wq_ref)
    k_ref[...] = k
    v_ref[...] = v
    q_ref[...] = q
    kb_ref[...] = k.astype(BF16)
    ones_col = (lax.broadcasted_iota(jnp.int32, (tm, V_HEAD_DIM), 1) == 0).astype(BF16)
    vb = v.astype(BF16)
    for h in range(N_HEADS):
        va_ref[:, 2 * h * V_HEAD_DIM:(2 * h + 1) * V_HEAD_DIM] = vb[:, h * V_HEAD_DIM:(h + 1) * V_HEAD_DIM]
        va_ref[:, (2 * h + 1) * V_HEAD_DIM:(2 * h + 2) * V_HEAD_DIM] = ones_col


def _qkv_call(h2, gkv, gq, wk, wv, wq):
    n, d = h2.shape
    tm = min(TOKEN_TILE, n)
    row = lambda i: (i, 0)
    return pl.pallas_call(
        _qkv_kernel,
        grid=(n // tm,),
        in_specs=[pl.BlockSpec((tm, d), row), _resident((1, d)), _resident((1, d)),
                  _resident((d, d)), _resident((d, d)), _resident((d, d))],
        out_specs=[pl.BlockSpec((tm, d), row), pl.BlockSpec((tm, d), row), pl.BlockSpec((tm, d), row),
                   pl.BlockSpec((tm, d), row),
                   pl.BlockSpec((tm, 2 * d), row)],
        out_shape=[jax.ShapeDtypeStruct((n, d), F32), jax.ShapeDtypeStruct((n, d), F32),
                   jax.ShapeDtypeStruct((n, d), BF16), jax.ShapeDtypeStruct((n, d), BF16),
                   jax.ShapeDtypeStruct((n, 2 * d), BF16)],
        compiler_params=_params(1),
        name="qkv",
    )(h2, gkv, gq, wk, wv, wq)


def _lambda(lam_ref):
    lv = lam_ref[...]
    s1 = jnp.sum(lv[0:1, :] * lv[1:2, :], axis=-1, keepdims=True)
    s2 = jnp.sum(lv[2:3, :] * lv[3:4, :], axis=-1, keepdims=True)
    return jnp.exp(s1) - jnp.exp(s2) + LAM_INIT


def _sub_norm(o, subg_ref):
    return (_rms_unit(o) * subg_ref[...]) * (1.0 - LAM_INIT)


def _rel_bucket(n):
    max_exact = N_BUCKETS // 2
    nf = jnp.maximum(n, 1).astype(F32)
    large = max_exact + (jnp.log(nf / max_exact) / math.log(MAX_DISTANCE / max_exact)
                         * (N_BUCKETS - max_exact)).astype(jnp.int32)
    large = jnp.minimum(large, N_BUCKETS - 1)
    return jnp.where(n < max_exact, n, large)


def _shifted_bias(rel_table, dist, valid):
    bucket = _rel_bucket(jnp.maximum(dist, 0))[None]
    lead = (N_HEADS,) + (1,) * dist.ndim
    vals = jnp.zeros((N_HEADS,) + dist.shape, F32)
    for n in range(N_BUCKETS - 1):
        vals = jnp.where(bucket == n, (rel_table[n] - rel_table[N_BUCKETS - 1]).reshape(lead), vals)
    return jnp.where(valid[None], vals * LOG2E, NEG_INF)


def _attn_kernel(q_ref, k_ref, va_ref, bias_ref, lam_ref, subg_ref, o_ref, qz_ref, acc_ref, m_ref):
    blk = q_ref.shape[1]
    i = pl.program_id(2)
    q = q_ref[0]
    first_half = lax.broadcasted_iota(jnp.int32, q.shape, 1) < HEAD_DIM
    zero = jnp.zeros_like(q)
    qz_ref[0:blk, :] = jnp.where(first_half, q, zero)
    qz_ref[blk:2 * blk, :] = jnp.where(first_half, zero, q)
    acc_ref[...] = jnp.zeros_like(acc_ref)
    m_ref[...] = jnp.full_like(m_ref, NEG_INF)
    reps = blk // LANES

    nt = (((1,), (1,)), ((), ()))
    halves = [slice(c * blk, (c + 1) * blk) for c in range(2)]

    def scores(j):
        kj = k_ref[0, pl.ds(pl.multiple_of(j * blk, blk), blk), :]
        return [lax.dot_general(qz_ref[h, :], kj, nt, preferred_element_type=F32) for h in halves]

    def accumulate(j, s, bias):
        va = va_ref[0, pl.ds(pl.multiple_of(j * blk, blk), blk), :]
        if bias is not None:
            s = [sc + bias_ref[0, bias] for sc in s]
        m_prev = [m_ref[h, :] for h in halves]
        m_new = [jnp.maximum(mp, jnp.max(sc, axis=1, keepdims=True)) for mp, sc in zip(m_prev, s)]
        alpha = [jnp.exp2(mp - mn) for mp, mn in zip(m_prev, m_new)]
        p = [jnp.exp2(sc - jnp.concatenate([mn] * reps, axis=1)).astype(BF16) for sc, mn in zip(s, m_new)]
        pv = [jnp.dot(pc, va, preferred_element_type=F32) for pc in p]
        for h, al, pvc, mn in zip(halves, alpha, pv, m_new):
            acc_ref[h, :] = jnp.concatenate([al, al], axis=1) * acc_ref[h, :] + pvc
            m_ref[h, :] = mn

    def process(blocks):
        s_all = [scores(j) for j, _ in blocks]
        for (j, bias), s in zip(blocks, s_all):
            accumulate(j, s, bias)

    n_far = jnp.maximum(i - 1, 0)

    def far_pair(t, carry):
        process([(2 * t, None), (2 * t + 1, None)])
        return carry

    lax.fori_loop(0, n_far // 2, far_pair, 0)
    odd = n_far % 2 == 1
    diag, prev = 0, 1

    @pl.when(i == 0)
    def _():
        process([(i, diag)])

    @pl.when(jnp.logical_and(i > 0, jnp.logical_not(odd)))
    def _():
        process([(i - 1, prev), (i, diag)])

    @pl.when(odd)
    def _():
        process([(i - 2, None), (i - 1, prev), (i, diag)])

    lam = _lambda(lam_ref)
    a1 = acc_ref[0:blk, :]
    a2 = acc_ref[blk:2 * blk, :]
    o = (a1[:, :V_HEAD_DIM] / a1[:, V_HEAD_DIM:V_HEAD_DIM + 1]
         - lam * (a2[:, :V_HEAD_DIM] / a2[:, V_HEAD_DIM:V_HEAD_DIM + 1]))
    o_ref[0] = _sub_norm(o, subg_ref).astype(o_ref.dtype)


def _attn_call(q3, k3, va3, bias, lamv, subg):
    bsz, t, d = q3.shape
    blk = min(ATTN_BLOCK, t)
    return pl.pallas_call(
        _attn_kernel,
        grid=(bsz, N_HEADS, t // blk),
        in_specs=[pl.BlockSpec((1, blk, V_HEAD_DIM), lambda b, h, i: (b, i, h)),
                  pl.BlockSpec((1, t, V_HEAD_DIM), lambda b, h, i: (b, 0, h)),
                  pl.BlockSpec((1, t, 2 * V_HEAD_DIM), lambda b, h, i: (b, 0, h)),
                  pl.BlockSpec((1, 2, blk, blk), lambda b, h, i: (h, 0, 0, 0)),
                  _resident((SUBLANES, LANES)), _resident((1, V_HEAD_DIM))],
        out_specs=pl.BlockSpec((1, blk, V_HEAD_DIM), lambda b, h, i: (b, i, h)),
        out_shape=jax.ShapeDtypeStruct((bsz, t, d), BF16),
        scratch_shapes=[pltpu.VMEM((2 * blk, V_HEAD_DIM), BF16),
                        pltpu.VMEM((2 * blk, 2 * V_HEAD_DIM), F32),
                        pltpu.VMEM((2 * blk, LANES), F32)],
        compiler_params=_params(3),
        name="attn_prompt",
    )(q3, k3, va3, bias, lamv, subg)


def _decode_kernel(pt_ref, q_ref, *refs, n_pages_step):
    g_n = n_pages_step
    k_refs = refs[:g_n]
    v_refs = refs[g_n:2 * g_n]
    (knew_ref, vnew_ref, blast_ref, bnew_ref, lam_ref, subg_ref,
     o_ref, qbd_ref, acc_ref, m_ref, l_ref) = refs[2 * g_n:]
    st = pl.program_id(1)
    last = pl.num_programs(1) - 1
    tq = q_ref.shape[1]
    cols, d = qbd_ref.shape
    page = blast_ref.shape[0]

    @pl.when(st == 0)
    def _():
        qt = jnp.concatenate([q_ref[0]] * (cols // tq), axis=0)
        row = lax.broadcasted_iota(jnp.int32, (cols, d), 0)
        lane = lax.broadcasted_iota(jnp.int32, (cols, d), 1)
        qbd_ref[...] = jnp.where(row // tq == lane // HEAD_DIM, qt, jnp.zeros_like(qt))
        acc_ref[...] = jnp.zeros_like(acc_ref)
        m_ref[...] = jnp.full_like(m_ref, NEG_INF)
        l_ref[...] = jnp.zeros_like(l_ref)

    def slab(ref):
        return jnp.concatenate([ref[0, pl.ds(h, page, stride=N_HEADS), :] for h in range(N_HEADS)],
                               axis=1).astype(BF16)

    def per_row(v):
        return jnp.broadcast_to(v, (cols, cols)).T

    nt = (((1,), (1,)), ((), ()))
    tn = (((0,), (0,)), ((), ()))

    def scores(k):
        return lax.dot_general(k, qbd_ref[...], nt, preferred_element_type=F32)

    def update(s, v):
        m_prev = m_ref[...]
        m_new = jnp.maximum(m_prev, jnp.max(s, axis=0, keepdims=True))
        alpha = jnp.exp2(m_prev - m_new)
        p = jnp.exp2(s - m_new)
        l_ref[...] = alpha * l_ref[...] + jnp.sum(p, axis=0, keepdims=True)
        pv = lax.dot_general(p.astype(BF16), v, tn, preferred_element_type=F32)
        acc_ref[...] = jnp.concatenate([per_row(alpha)] * (d // cols), axis=1) * acc_ref[...] + pv
        m_ref[...] = m_new

    sub = math.gcd(DECODE_GROUP, g_n)
    groups = [slice(a, a + sub) for a in range(0, g_n, sub)]
    s_all = [scores(jnp.concatenate([slab(r) for r in k_refs[gs]], axis=0)) for gs in groups]
    is_last = (st == last).astype(F32)
    s_all[-1] = s_all[-1] + jnp.concatenate(
        [jnp.zeros(((sub - 1) * page, cols), F32), is_last * blast_ref[...]], axis=0)
    for gs, s in zip(groups, s_all):
        update(s, jnp.concatenate([slab(r) for r in v_refs[gs]], axis=0))

    def new_slab(ref):
        return jnp.concatenate([ref[0], jnp.zeros((page - tq, d), F32)], axis=0).astype(BF16)

    @pl.when(st == last)
    def _():
        update(scores(new_slab(knew_ref)) + bnew_ref[...], new_slab(vnew_ref))
        lam = _lambda(lam_ref)
        o_all = acc_ref[...] / jnp.concatenate([per_row(l_ref[...])] * (d // cols), axis=1)
        for h in range(N_HEADS):
            blk = o_all[2 * h * tq:(2 * h + 2) * tq, h * V_HEAD_DIM:(h + 1) * V_HEAD_DIM]
            o = blk[:tq] - lam * blk[tq:]
            o_ref[0, :, h * V_HEAD_DIM:(h + 1) * V_HEAD_DIM] = _sub_norm(o, subg_ref).astype(o_ref.dtype)


def _decode_call(page_table, q3, cache_k3, cache_v3, knew, vnew, blast, bnew, lamv, subg):
    db, n_pages = page_table.shape
    _, tq, d = q3.shape
    prow = cache_k3.shape[1]
    page = prow // N_HEADS
    g_n = math.gcd(PAGES_PER_STEP, n_pages)
    cols = N_HEADS * 2 * tq

    def page_spec(g):
        return pl.BlockSpec((1, prow, V_HEAD_DIM), lambda b, s, pt: (pt[b, s * g_n + g], 0, 0))

    per_b = lambda b, s, pt: (b, 0, 0)
    const2 = lambda b, s, pt: (0, 0)
    tile = pl.BlockSpec((page, cols), const2)
    grid_spec = pltpu.PrefetchScalarGridSpec(
        num_scalar_prefetch=1,
        grid=(db, n_pages // g_n),
        in_specs=([pl.BlockSpec((1, tq, d), per_b)]
                  + [page_spec(g) for g in range(g_n)] + [page_spec(g) for g in range(g_n)]
                  + [pl.BlockSpec((1, tq, d), per_b), pl.BlockSpec((1, tq, d), per_b),
                     tile, tile,
                     pl.BlockSpec((SUBLANES, LANES), const2), pl.BlockSpec((1, V_HEAD_DIM), const2)]),
        out_specs=pl.BlockSpec((1, tq, d), per_b),
        scratch_shapes=[pltpu.VMEM((cols, d), BF16), pltpu.VMEM((cols, d), F32),
                        pltpu.VMEM((1, cols), F32), pltpu.VMEM((1, cols), F32)],
    )
    return pl.pallas_call(
        functools.partial(_decode_kernel, n_pages_step=g_n),
        grid_spec=grid_spec,
        out_shape=jax.ShapeDtypeStruct((db, tq, d), BF16),
        compiler_params=_params(2),
        name="attn_decode",
    )(page_table, q3, *([cache_k3] * g_n), *([cache_v3] * g_n), knew, vnew, blast, bnew, lamv, subg)


def _post_kernel(a_ref, h_ref, wo_ref, gm_ref, w1_ref, w2_ref, gf_ref, o_ref):
    h = h_ref[...] + jnp.dot(a_ref[...], wo_ref[...], preferred_element_type=F32)
    h = _mlp(h, gm_ref, w1_ref, w2_ref)
    o_ref[...] = _rms_unit(h) * gf_ref[...]


def _post_call(a2, h2, wo, gm, w1, w2, gf):
    n, d = h2.shape
    tm = min(TOKEN_TILE, n)
    row = lambda i: (i, 0)
    return pl.pallas_call(
        _post_kernel,
        grid=(n // tm,),
        in_specs=[pl.BlockSpec((tm, d), row), pl.BlockSpec((tm, d), row),
                  _resident((d, d)), _resident((1, d)), _resident(w1.shape), _resident(w2.shape),
                  _resident((1, d))],
        out_specs=pl.BlockSpec((tm, d), row),
        out_shape=jax.ShapeDtypeStruct((n, d), F32),
        compiler_params=_params(1),
        name="post",
    )(a2, h2, wo, gm, w1, w2, gf)


def _row(v):
    return v.reshape(1, -1).astype(F32)


def kernel(x_prompt, x_sample, state_conv, cache_k, cache_v, page_table, norm_mix_g, norm_mlp_g, conv_pw1_w, conv_pw1_b, conv_dw_w, conv_dw_b, conv_ln_g, conv_ln_b, conv_pw2_w, conv_pw2_b, kv_norm_g, w_k, w_v, w_q, lambda_q1, lambda_k1, lambda_q2, lambda_k2, subln_g, w_o, rel_bias_table, mlp_w1, mlp_w2, final_norm_g):
    bsz, seq, d = x_prompt.shape
    db, dseq, _ = x_sample.shape
    n_pool, page = cache_k.shape[0], cache_k.shape[1]
    n_pages = page_table.shape[1]
    past_len = n_pages * page
    hist = CONV_WIDTH - 1

    pw1 = conv_pw1_w[0].astype(BF16)
    pw1b = _row(conv_pw1_b[0])
    dw = jnp.pad(conv_dw_w[0].astype(F32), ((0, HIST_ROWS - CONV_WIDTH), (0, 0)))
    dwb, ln_g, ln_b = _row(conv_dw_b[0]), _row(conv_ln_g[0]), _row(conv_ln_b[0])
    pw2 = conv_pw2_w[0].astype(BF16)
    pw2b = _row(conv_pw2_b[0])
    w1 = mlp_w1.astype(BF16)
    w2 = mlp_w2.astype(BF16)
    wk, wv, wq, wo = w_k.astype(BF16), w_v.astype(BF16), w_q[0].astype(BF16), w_o[0].astype(BF16)
    lamv = jnp.pad(jnp.stack([lambda_q1[0], lambda_k1[0], lambda_q2[0], lambda_k2[0]]).astype(F32),
                   ((0, SUBLANES - 4), (0, LANES - HEAD_DIM)))
    subg = _row(subln_g[0])
    rel = rel_bias_table.astype(F32)

    g_mix0, g_mix1, g_mlp0, g_kv = _row(norm_mix_g[0]), _row(norm_mix_g[1]), _row(norm_mlp_g[0]), _row(kv_norm_g)

    def back(a2, h2):
        return _post_call(a2, h2, wo, _row(norm_mlp_g[1]), w1[1], w2[1], _row(final_norm_g))

    x_p = x_prompt.reshape(bsz * seq, d)
    g_p = _glu_call(x_p, g_mix0, pw1, pw1b)
    c_p = _conv_long_call(g_p.reshape(bsz, seq, d), dw, dwb, ln_g, ln_b).reshape(bsz * seq, d)
    h_p = _pw2_mlp_call(c_p, x_p, pw2, pw2b, g_mlp0, w1[0], w2[0])
    k_p, v_p, q_p, kb_p, va_p = _qkv_call(h_p, g_kv, g_mix1, wk, wv, wq)
    blk = min(ATTN_BLOCK, seq)
    r = jnp.arange(blk, dtype=jnp.int32)[:, None]
    c = jnp.arange(blk, dtype=jnp.int32)[None, :]
    bias_p = jnp.stack([_shifted_bias(rel, r - c, r >= c),
                        _shifted_bias(rel, blk + r - c, jnp.full((blk, blk), True))], axis=1)
    a_p = _attn_call(q_p.reshape(bsz, seq, d), kb_p.reshape(bsz, seq, d),
                     va_p.reshape(bsz, seq, 2 * d), bias_p, lamv, subg)
    y_prompt = back(a_p.reshape(bsz * seq, d), h_p).reshape(bsz, seq, d)
    conv_state_prompt = g_p.reshape(bsz, seq, d)[:, seq - hist:][None]

    st_s = jnp.pad(state_conv[0].astype(F32), ((0, 0), (HIST_ROWS - hist, 0), (0, 0)))
    g_s, h_s, k_s, v_s, q_s = _short_front_call(
        x_sample.reshape(db * dseq, d), st_s, g_mix0, pw1, pw1b, dw, dwb, ln_g, ln_b,
        pw2, pw2b, g_mlp0, w1[0], w2[0], g_kv, g_mix1, wk, wv, wq)
    prow = page * N_HEADS
    n_cols = 2 * N_HEADS * dseq
    assert n_cols == LANES and dseq <= page
    col_q = jnp.tile(jnp.arange(dseq, dtype=jnp.int32), 2 * N_HEADS)[None, :]
    key = jnp.arange(page, dtype=jnp.int32)[:, None]
    pick = lambda b4: jnp.concatenate([b4[h, :, 2 * h * dseq:2 * (h + 1) * dseq] for h in range(N_HEADS)], axis=1)
    blast = pick(_shifted_bias(rel, page + col_q - key, jnp.full((page, n_cols), True)))
    bnew = pick(_shifted_bias(rel, col_q - key, (key <= col_q) & (key < dseq)))
    a_s = _decode_call(page_table.astype(jnp.int32), q_s.reshape(db, dseq, d),
                       cache_k.reshape(n_pool, prow, V_HEAD_DIM), cache_v.reshape(n_pool, prow, V_HEAD_DIM),
                       k_s.reshape(db, dseq, d), v_s.reshape(db, dseq, d), blast, bnew, lamv, subg)
    y_sample = back(a_s.reshape(db * dseq, d), h_s).reshape(db, dseq, d)
    conv_state_sample = jnp.concatenate([state_conv[0].astype(F32), g_s.reshape(db, dseq, d)],
                                        axis=1)[:, dseq:][None]

    kv_shape = (N_HEADS, V_HEAD_DIM)
    return (y_prompt, y_sample, conv_state_prompt, conv_state_sample,
            k_p.reshape(bsz, seq, *kv_shape), v_p.reshape(bsz, seq, *kv_shape),
            k_s.reshape(db, dseq, *kv_shape), v_s.reshape(db, dseq, *kv_shape))
```

```python
import functools
import math

import jax
import jax.numpy as jnp
from jax import lax
from jax.experimental import pallas as pl
from jax.experimental.pallas import tpu as pltpu

F32 = jnp.float32
BF16 = jnp.bfloat16

N_HEADS = 8
HEAD_DIM = 64
V_HEAD_DIM = 2 * HEAD_DIM
CONV_WIDTH = 31
N_BUCKETS = 32
MAX_DISTANCE = 128
NORM_EPS = 1e-6
NEG_INF = -1e30
LAM_INIT = 0.8 - 0.6 * math.exp(-0.3 * 1)
LOG2E = math.log2(math.e)
Q_SCALE = HEAD_DIM ** -0.5 * LOG2E

SUBLANES = 8
LANES = 128
HIST_ROWS = 32
VMEM_LIMIT = 56 * 1024 * 1024

TOKEN_TILE = 512
FF_CHUNK = 512
CONV_ROWS = 32
ATTN_BLOCK = 512
PAGES_PER_STEP = 16
DECODE_GROUP = 4


def _params(n_axes):
    return pltpu.CompilerParams(dimension_semantics=("arbitrary",) * n_axes,
                                vmem_limit_bytes=VMEM_LIMIT)


def _resident(shape):
    nd = len(shape)
    return pl.BlockSpec(shape, lambda *_: (0,) * nd, pipeline_mode=pl.Buffered(1))


def _rms_unit(x):
    return x * lax.rsqrt(jnp.mean(x * x, axis=-1, keepdims=True) + NORM_EPS)


def _mlp(h, g_ref, w1_ref, w2_ref):
    hn = (_rms_unit(h) * g_ref[...]).astype(BF16)
    d_ff = w1_ref.shape[1]
    acc = h
    for c in range(d_ff // FF_CHUNK):
        u = jnp.dot(hn, w1_ref[:, c * FF_CHUNK:(c + 1) * FF_CHUNK], preferred_element_type=F32)
        u = jnp.maximum(u, 0.0)
        acc = acc + jnp.dot((u * u).astype(BF16), w2_ref[c * FF_CHUNK:(c + 1) * FF_CHUNK, :],
                            preferred_element_type=F32)
    return acc


def _glu(x, g_ref, w_ref, b_ref):
    d = x.shape[1]
    hn = (_rms_unit(x) * g_ref[...]).astype(BF16)
    u = jnp.dot(hn, w_ref[...], preferred_element_type=F32) + b_ref[...]
    return u[:, :d] * jax.nn.sigmoid(u[:, d:])


def _glu_kernel(x_ref, g_ref, w_ref, b_ref, o_ref):
    o_ref[...] = _glu(x_ref[...], g_ref, w_ref, b_ref)


def _glu_call(x2, g, w, b):
    n, d = x2.shape
    tm = min(TOKEN_TILE, n)
    return pl.pallas_call(
        _glu_kernel,
        grid=(n // tm,),
        in_specs=[pl.BlockSpec((tm, d), lambda i: (i, 0)),
                  _resident((1, d)), _resident((d, 2 * d)), _resident((1, 2 * d))],
        out_specs=pl.BlockSpec((tm, d), lambda i: (i, 0)),
        out_shape=jax.ShapeDtypeStruct((n, d), F32),
        compiler_params=_params(1),
        name="glu",
    )(x2, g, w, b)


def _ln_swish(c, ln_g_ref, ln_b_ref):
    mu = jnp.mean(c, axis=-1, keepdims=True)
    cc = c - mu
    var = jnp.mean(cc * cc, axis=-1, keepdims=True)
    y = cc * lax.rsqrt(var + NORM_EPS) * ln_g_ref[...] + ln_b_ref[...]
    return y * jax.nn.sigmoid(y)


def _conv_long_kernel(g_ref, hist_ref, dw_ref, dwb_ref, ln_g_ref, ln_b_ref, o_ref, sh_ref):
    tl = g_ref.shape[1]
    first = pl.program_id(1) == 0
    hist = jnp.where(first, 0.0, hist_ref[0])
    sh_ref[0, 0:HIST_ROWS, :] = hist
    sh_ref[0, HIST_ROWS:HIST_ROWS + tl, :] = g_ref[0]
    n_sh = tl + HIST_ROWS - SUBLANES
    for b in range(1, SUBLANES):
        sh_ref[b, 0:n_sh, :] = sh_ref[0, b:b + n_sh, :]

    def chunk(ci, carry):
        r0 = pl.multiple_of(ci * CONV_ROWS, CONV_ROWS)
        acc = jnp.broadcast_to(dwb_ref[...], (CONV_ROWS, dwb_ref.shape[1]))
        for k in range(CONV_WIDTH):
            off = HIST_ROWS - (CONV_WIDTH - 1) + k
            a, b = divmod(off, SUBLANES)
            acc = acc + dw_ref[k:k + 1, :] * sh_ref[b, pl.ds(r0 + a * SUBLANES, CONV_ROWS), :]
        o_ref[0, pl.ds(r0, CONV_ROWS), :] = _ln_swish(acc, ln_g_ref, ln_b_ref).astype(o_ref.dtype)
        return carry

    lax.fori_loop(0, tl // CONV_ROWS, chunk, 0, unroll=4)


def _conv_long_call(g3, dw, dwb, ln_g, ln_b):
    bsz, t, d = g3.shape
    tl = min(TOKEN_TILE, t)
    per = tl // HIST_ROWS
    return pl.pallas_call(
        _conv_long_kernel,
        grid=(bsz, t // tl),
        in_specs=[pl.BlockSpec((1, tl, d), lambda b, j: (b, j, 0)),
                  pl.BlockSpec((1, HIST_ROWS, d), lambda b, j: (b, jnp.maximum(j * per - 1, 0), 0)),
                  _resident((HIST_ROWS, d)), _resident((1, d)), _resident((1, d)), _resident((1, d))],
        out_specs=pl.BlockSpec((1, tl, d), lambda b, j: (b, j, 0)),
        out_shape=jax.ShapeDtypeStruct((bsz, t, d), BF16),
        scratch_shapes=[pltpu.VMEM((SUBLANES, tl + HIST_ROWS, d), F32)],
        compiler_params=_params(2),
        name="conv_long",
    )(g3, g3, dw, dwb, ln_g, ln_b)


def _short_front_kernel(x_ref, st_ref, g0_ref, pw1_ref, pw1b_ref, dw_ref, dwb_ref, ln_g_ref, ln_b_ref,
                        pw2_ref, pw2b_ref, gm_ref, w1_ref, w2_ref, gkv_ref, gq_ref, wk_ref, wv_ref, wq_ref,
                        g_ref, h_ref, k_ref, v_ref, q_ref, ext_ref):
    n, d = x_ref.shape
    s = st_ref.shape[0]
    t = n // s
    x = x_ref[...]
    g = _glu(x, g0_ref, pw1_ref, pw1b_ref)
    g_ref[...] = g
    ext_ref[:, 0:HIST_ROWS, :] = st_ref[...]
    ext_ref[:, HIST_ROWS:HIST_ROWS + t, :] = g.reshape(s, t, d)
    acc = jnp.broadcast_to(dwb_ref[...].reshape(1, 1, d), (s, t, d))
    for k in range(CONV_WIDTH):
        off = HIST_ROWS - (CONV_WIDTH - 1) + k
        acc = acc + dw_ref[k:k + 1, :].reshape(1, 1, d) * ext_ref[:, off:off + t, :]
    c = _ln_swish(acc.reshape(n, d), ln_g_ref, ln_b_ref).astype(BF16)
    h = x + jnp.dot(c, pw2_ref[...], preferred_element_type=F32) + pw2b_ref[...]
    h = _mlp(h, gm_ref, w1_ref, w2_ref)
    h_ref[...] = h
    k_ref[...], v_ref[...], q_ref[...] = _project_qkv(h, gkv_ref, gq_ref, wk_ref, wv_ref, wq_ref)


def _short_front_call(x2, st3, *params):
    n, d = x2.shape
    s = st3.shape[0]
    out = lambda dt: jax.ShapeDtypeStruct((n, d), dt)
    return pl.pallas_call(
        _short_front_kernel,
        grid=(1,),
        in_specs=[_resident(a.shape) for a in (x2, st3) + params],
        out_specs=[pl.BlockSpec((n, d), lambda i: (0, 0))] * 5,
        out_shape=[out(F32), out(F32), out(F32), out(F32), out(BF16)],
        scratch_shapes=[pltpu.VMEM((s, HIST_ROWS + n // s, d), F32)],
        compiler_params=_params(1),
        name="short_front",
    )(x2, st3, *params)


def _pw2_mlp_kernel(c_ref, x_ref, pw2_ref, pw2b_ref, gm_ref, w1_ref, w2_ref, o_ref):
    h = x_ref[...] + jnp.dot(c_ref[...], pw2_ref[...], preferred_element_type=F32) + pw2b_ref[...]
    o_ref[...] = _mlp(h, gm_ref, w1_ref, w2_ref)


def _pw2_mlp_call(c2, x2, pw2, pw2b, gm, w1, w2):
    n, d = x2.shape
    tm = min(TOKEN_TILE, n)
    row = lambda i: (i, 0)
    return pl.pallas_call(
        _pw2_mlp_kernel,
        grid=(n // tm,),
        in_specs=[pl.BlockSpec((tm, d), row), pl.BlockSpec((tm, d), row),
                  _resident((d, d)), _resident((1, d)), _resident((1, d)),
                  _resident(w1.shape), _resident(w2.shape)],
        out_specs=pl.BlockSpec((tm, d), row),
        out_shape=jax.ShapeDtypeStruct((n, d), F32),
        compiler_params=_params(1),
        name="pw2_mlp",
    )(c2, x2, pw2, pw2b, gm, w1, w2)


def _project_qkv(h, gkv_ref, gq_ref, wk_ref, wv_ref, wq_ref):
    y = _rms_unit(h)
    kvn = (y * gkv_ref[...]).astype(BF16)
    hn = (y * gq_ref[...]).astype(BF16)
    k = jnp.dot(kvn, wk_ref[...], preferred_element_type=F32)
    v = jnp.dot(kvn, wv_ref[...], preferred_element_type=F32)
    q = jnp.dot(hn, wq_ref[...], preferred_element_type=F32)
    return k, v, (q * Q_SCALE).astype(BF16)


def _qkv_kernel(h_ref, gkv_ref, gq_ref, wk_ref, wv_ref, wq_ref,
                k_ref, v_ref, q_ref, kb_ref, va_ref):
    tm, d = h_ref.shape
    k, v, q = _project_qkv(h_ref[...], gkv_ref, gq_ref, wk_ref, wv_ref, wq_ref)
    k_ref[...] = k
    v_ref[...] = v
    q_ref[...] = q
    kb_ref[...] = k.astype(BF16)
    ones_col = (lax.broadcasted_iota(jnp.int32, (tm, V_HEAD_DIM), 1) == 0).astype(BF16)
    vb = v.astype(BF16)
    for h in range(N_HEADS):
        va_ref[:, 2 * h * V_HEAD_DIM:(2 * h + 1) * V_HEAD_DIM] = vb[:, h * V_HEAD_DIM:(h + 1) * V_HEAD_DIM]
        va_ref[:, (2 * h + 1) * V_HEAD_DIM:(2 * h + 2) * V_HEAD_DIM] = ones_col


def _qkv_call(h2, gkv, gq, wk, wv, wq):
    n, d = h2.shape
    tm = min(TOKEN_TILE, n)
    row = lambda i: (i, 0)
    return pl.pallas_call(
        _qkv_kernel,
        grid=(n // tm,),
        in_specs=[pl.BlockSpec((tm, d), row), _resident((1, d)), _resident((1, d)),
                  _resident((d, d)), _resident((d, d)), _resident((d, d))],
        out_specs=[pl.BlockSpec((tm, d), row), pl.BlockSpec((tm, d), row), pl.BlockSpec((tm, d), row),
                   pl.BlockSpec((tm, d), row),
                   pl.BlockSpec((tm, 2 * d), row)],
        out_shape=[jax.ShapeDtypeStruct((n, d), F32), jax.ShapeDtypeStruct((n, d), F32),
                   jax.ShapeDtypeStruct((n, d), BF16), jax.ShapeDtypeStruct((n, d), BF16),
                   jax.ShapeDtypeStruct((n, 2 * d), BF16)],
        compiler_params=_params(1),
        name="qkv",
    )(h2, gkv, gq, wk, wv, wq)


def _lambda(lam_ref):
    lv = lam_ref[...]
    s1 = jnp.sum(lv[0:1, :] * lv[1:2, :], axis=-1, keepdims=True)
    s2 = jnp.sum(lv[2:3, :] * lv[3:4, :], axis=-1, keepdims=True)
    return jnp.exp(s1) - jnp.exp(s2) + LAM_INIT


def _sub_norm(o, subg_ref):
    return (_rms_unit(o) * subg_ref[...]) * (1.0 - LAM_INIT)


def _rel_bucket(n):
    max_exact = N_BUCKETS // 2
    nf = jnp.maximum(n, 1).astype(F32)
    large = max_exact + (jnp.log(nf / max_exact) / math.log(MAX_DISTANCE / max_exact)
                         * (N_BUCKETS - max_exact)).astype(jnp.int32)
    large = jnp.minimum(large, N_BUCKETS - 1)
    return jnp.where(n < max_exact, n, large)


def _shifted_bias(rel_table, dist, valid):
    bucket = _rel_bucket(jnp.maximum(dist, 0))[None]
    lead = (N_HEADS,) + (1,) * dist.ndim
    vals = jnp.zeros((N_HEADS,) + dist.shape, F32)
    for n in range(N_BUCKETS - 1):
        vals = jnp.where(bucket == n, (rel_table[n] - rel_table[N_BUCKETS - 1]).reshape(lead), vals)
    return jnp.where(valid[None], vals * LOG2E, NEG_INF)


def _attn_kernel(q_ref, k_ref, va_ref, bias_ref, lam_ref, subg_ref, o_ref, qz_ref, acc_ref, m_ref):
    blk = q_ref.shape[1]
    i = pl.program_id(2)
    q = q_ref[0]
    first_half = lax.broadcasted_iota(jnp.int32, q.shape, 1) < HEAD_DIM
    zero = jnp.zeros_like(q)
    qz_ref[0:blk, :] = jnp.where(first_half, q, zero)
    qz_ref[blk:2 * blk, :] = jnp.where(first_half, zero, q)
    acc_ref[...] = jnp.zeros_like(acc_ref)
    m_ref[...] = jnp.full_like(m_ref, NEG_INF)
    reps = blk // LANES

    nt = (((1,), (1,)), ((), ()))
    halves = [slice(c * blk, (c + 1) * blk) for c in range(2)]

    def scores(j):
        kj = k_ref[0, pl.ds(pl.multiple_of(j * blk, blk), blk), :]
        return [lax.dot_general(qz_ref[h, :], kj, nt, preferred_element_type=F32) for h in halves]

    sub = bias_ref.shape[2]
    nb = blk // sub

    def add_bias(sc, kind):
        if kind == "prev":
            top = jnp.concatenate([sc[:sub, :blk - sub], sc[:sub, blk - sub:] + bias_ref[0, 1]], axis=1)
            return jnp.concatenate([top, sc[sub:, :]], axis=0) if nb > 1 else top
        rows = []
        for r in range(nb):
            tiles = [bias_ref[0, 0] if c == r else bias_ref[0, 1] if c == r - 1 else
                     jnp.full((sub, sub), 0.0 if c < r else NEG_INF, F32) for c in range(nb)]
            rows.append(jnp.concatenate(tiles, axis=1))
        return sc + jnp.concatenate(rows, axis=0)

    def accumulate(j, s, bias):
        va = va_ref[0, pl.ds(pl.multiple_of(j * blk, blk), blk), :]
        if bias is not None:
            s = [add_bias(sc, bias) for sc in s]
        m_prev = [m_ref[h, :] for h in halves]
        m_new = [jnp.maximum(mp, jnp.max(sc, axis=1, keepdims=True)) for mp, sc in zip(m_prev, s)]
        alpha = [jnp.exp2(mp - mn) for mp, mn in zip(m_prev, m_new)]
        p = [jnp.exp2(sc - jnp.concatenate([mn] * reps, axis=1)).astype(BF16) for sc, mn in zip(s, m_new)]
        pv = [jnp.dot(pc, va, preferred_element_type=F32) for pc in p]
        for h, al, pvc, mn in zip(halves, alpha, pv, m_new):
            acc_ref[h, :] = jnp.concatenate([al, al], axis=1) * acc_ref[h, :] + pvc
            m_ref[h, :] = mn

    def process(blocks):
        s_all = [scores(j) for j, _ in blocks]
        for (j, bias), s in zip(blocks, s_all):
            accumulate(j, s, bias)

    n_far = jnp.maximum(i - 1, 0)

    def far_pair(t, carry):
        process([(2 * t, None), (2 * t + 1, None)])
        return carry

    lax.fori_loop(0, n_far // 2, far_pair, 0)
    odd = n_far % 2 == 1
    diag, prev = "diag", "prev"

    @pl.when(i == 0)
    def _():
        process([(i, diag)])

    @pl.when(jnp.logical_and(i > 0, jnp.logical_not(odd)))
    def _():
        process([(i - 1, prev), (i, diag)])

    @pl.when(odd)
    def _():
        process([(i - 2, None), (i - 1, prev), (i, diag)])

    lam = _lambda(lam_ref)
    a1 = acc_ref[0:blk, :]
    a2 = acc_ref[blk:2 * blk, :]
    o = (a1[:, :V_HEAD_DIM] / a1[:, V_HEAD_DIM:V_HEAD_DIM + 1]
         - lam * (a2[:, :V_HEAD_DIM] / a2[:, V_HEAD_DIM:V_HEAD_DIM + 1]))
    o_ref[0] = _sub_norm(o, subg_ref).astype(o_ref.dtype)


def _attn_call(q3, k3, va3, bias, lamv, subg):
    bsz, t, d = q3.shape
    blk = min(ATTN_BLOCK, t)
    return pl.pallas_call(
        _attn_kernel,
        grid=(bsz, N_HEADS, t // blk),
        in_specs=[pl.BlockSpec((1, blk, V_HEAD_DIM), lambda b, h, i: (b, i, h)),
                  pl.BlockSpec((1, t, V_HEAD_DIM), lambda b, h, i: (b, 0, h)),
                  pl.BlockSpec((1, t, 2 * V_HEAD_DIM), lambda b, h, i: (b, 0, h)),
                  pl.BlockSpec((1, 2, MAX_DISTANCE, MAX_DISTANCE), lambda b, h, i: (h, 0, 0, 0)),
                  _resident((SUBLANES, LANES)), _resident((1, V_HEAD_DIM))],
        out_specs=pl.BlockSpec((1, blk, V_HEAD_DIM), lambda b, h, i: (b, i, h)),
        out_shape=jax.ShapeDtypeStruct((bsz, t, d), BF16),
        scratch_shapes=[pltpu.VMEM((2 * blk, V_HEAD_DIM), BF16),
                        pltpu.VMEM((2 * blk, 2 * V_HEAD_DIM), F32),
                        pltpu.VMEM((2 * blk, LANES), F32)],
        compiler_params=_params(3),
        name="attn_prompt",
    )(q3, k3, va3, bias, lamv, subg)


def _decode_kernel(pt_ref, q_ref, *refs, n_pages_step):
    g_n = n_pages_step
    k_refs = refs[:g_n]
    v_refs = refs[g_n:2 * g_n]
    (knew_ref, vnew_ref, blast_ref, bnew_ref, lam_ref, subg_ref,
     o_ref, qbd_ref, acc_ref, m_ref, l_ref) = refs[2 * g_n:]
    st = pl.program_id(1)
    last = pl.num_programs(1) - 1
    tq = q_ref.shape[1]
    cols, d = qbd_ref.shape
    page = blast_ref.shape[0]

    @pl.when(st == 0)
    def _():
        qt = jnp.concatenate([q_ref[0]] * (cols // tq), axis=0)
        row = lax.broadcasted_iota(jnp.int32, (cols, d), 0)
        lane = lax.broadcasted_iota(jnp.int32, (cols, d), 1)
        qbd_ref[...] = jnp.where(row // tq == lane // HEAD_DIM, qt, jnp.zeros_like(qt))
        acc_ref[...] = jnp.zeros_like(acc_ref)
        m_ref[...] = jnp.full_like(m_ref, NEG_INF)
        l_ref[...] = jnp.zeros_like(l_ref)

    def slab(ref):
        return jnp.concatenate([ref[0, pl.ds(h, page, stride=N_HEADS), :] for h in range(N_HEADS)],
                               axis=1).astype(BF16)

    def per_row(v):
        return jnp.broadcast_to(v, (cols, cols)).T

    nt = (((1,), (1,)), ((), ()))
    tn = (((0,), (0,)), ((), ()))

    def scores(k):
        return lax.dot_general(k, qbd_ref[...], nt, preferred_element_type=F32)

    def update(s, v):
        m_prev = m_ref[...]
        m_new = jnp.maximum(m_prev, jnp.max(s, axis=0, keepdims=True))
        alpha = jnp.exp2(m_prev - m_new)
        p = jnp.exp2(s - m_new)
        l_ref[...] = alpha * l_ref[...] + jnp.sum(p, axis=0, keepdims=True)
        pv = lax.dot_general(p.astype(BF16), v, tn, preferred_element_type=F32)
        acc_ref[...] = jnp.concatenate([per_row(alpha)] * (d // cols), axis=1) * acc_ref[...] + pv
        m_ref[...] = m_new

    sub = math.gcd(DECODE_GROUP, g_n)
    groups = [slice(a, a + sub) for a in range(0, g_n, sub)]
    s_all = [scores(jnp.concatenate([slab(r) for r in k_refs[gs]], axis=0)) for gs in groups]
    is_last = (st == last).astype(F32)
    s_all[-1] = s_all[-1] + jnp.concatenate(
        [jnp.zeros(((sub - 1) * page, cols), F32), is_last * blast_ref[...]], axis=0)
    for gs, s in zip(groups, s_all):
        update(s, jnp.concatenate([slab(r) for r in v_refs[gs]], axis=0))

    def new_slab(ref):
        return jnp.concatenate([ref[0], jnp.zeros((page - tq, d), F32)], axis=0).astype(BF16)

    @pl.when(st == last)
    def _():
        update(scores(new_slab(knew_ref)) + bnew_ref[...], new_slab(vnew_ref))
        lam = _lambda(lam_ref)
        o_all = acc_ref[...] / jnp.concatenate([per_row(l_ref[...])] * (d // cols), axis=1)
        for h in range(N_HEADS):
            blk = o_all[2 * h * tq:(2 * h + 2) * tq, h * V_HEAD_DIM:(h + 1) * V_HEAD_DIM]
            o = blk[:tq] - lam * blk[tq:]
            o_ref[0, :, h * V_HEAD_DIM:(h + 1) * V_HEAD_DIM] = _sub_norm(o, subg_ref).astype(o_ref.dtype)


def _decode_call(page_table, q3, cache_k3, cache_v3, knew, vnew, blast, bnew, lamv, subg):
    db, n_pages = page_table.shape
    _, tq, d = q3.shape
    prow = cache_k3.shape[1]
    page = prow // N_HEADS
    g_n = math.gcd(PAGES_PER_STEP, n_pages)
    cols = N_HEADS * 2 * tq

    def page_spec(g):
        return pl.BlockSpec((1, prow, V_HEAD_DIM), lambda b, s, pt: (pt[b, s * g_n + g], 0, 0))

    per_b = lambda b, s, pt: (b, 0, 0)
    const2 = lambda b, s, pt: (0, 0)
    tile = pl.BlockSpec((page, cols), const2)
    grid_spec = pltpu.PrefetchScalarGridSpec(
        num_scalar_prefetch=1,
        grid=(db, n_pages // g_n),
        in_specs=([pl.BlockSpec((1, tq, d), per_b)]
                  + [page_spec(g) for g in range(g_n)] + [page_spec(g) for g in range(g_n)]
                  + [pl.BlockSpec((1, tq, d), per_b), pl.BlockSpec((1, tq, d), per_b),
                     tile, tile,
                     pl.BlockSpec((SUBLANES, LANES), const2), pl.BlockSpec((1, V_HEAD_DIM), const2)]),
        out_specs=pl.BlockSpec((1, tq, d), per_b),
        scratch_shapes=[pltpu.VMEM((cols, d), BF16), pltpu.VMEM((cols, d), F32),
                        pltpu.VMEM((1, cols), F32), pltpu.VMEM((1, cols), F32)],
    )
    return pl.pallas_call(
        functools.partial(_decode_kernel, n_pages_step=g_n),
        grid_spec=grid_spec,
        out_shape=jax.ShapeDtypeStruct((db, tq, d), BF16),
        compiler_params=_params(2),
        name="attn_decode",
    )(page_table, q3, *([cache_k3] * g_n), *([cache_v3] * g_n), knew, vnew, blast, bnew, lamv, subg)


def _post_kernel(a_ref, h_ref, wo_ref, gm_ref, w1_ref, w2_ref, gf_ref, o_ref):
    h = h_ref[...] + jnp.dot(a_ref[...], wo_ref[...], preferred_element_type=F32)
    h = _mlp(h, gm_ref, w1_ref, w2_ref)
    o_ref[...] = _rms_unit(h) * gf_ref[...]


def _post_call(a2, h2, wo, gm, w1, w2, gf):
    n, d = h2.shape
    tm = min(TOKEN_TILE, n)
    row = lambda i: (i, 0)
    return pl.pallas_call(
        _post_kernel,
        grid=(n // tm,),
        in_specs=[pl.BlockSpec((tm, d), row), pl.BlockSpec((tm, d), row),
                  _resident((d, d)), _resident((1, d)), _resident(w1.shape), _resident(w2.shape),
                  _resident((1, d))],
        out_specs=pl.BlockSpec((tm, d), row),
        out_shape=jax.ShapeDtypeStruct((n, d), F32),
        compiler_params=_params(1),
        name="post",
    )(a2, h2, wo, gm, w1, w2, gf)


def _row(v):
    return v.reshape(1, -1).astype(F32)


def kernel(x_prompt, x_sample, state_conv, cache_k, cache_v, page_table, norm_mix_g, norm_mlp_g, conv_pw1_w, conv_pw1_b, conv_dw_w, conv_dw_b, conv_ln_g, conv_ln_b, conv_pw2_w, conv_pw2_b, kv_norm_g, w_k, w_v, w_q, lambda_q1, lambda_k1, lambda_q2, lambda_k2, subln_g, w_o, rel_bias_table, mlp_w1, mlp_w2, final_norm_g):
    bsz, seq, d = x_prompt.shape
    db, dseq, _ = x_sample.shape
    n_pool, page = cache_k.shape[0], cache_k.shape[1]
    n_pages = page_table.shape[1]
    past_len = n_pages * page
    hist = CONV_WIDTH - 1

    pw1 = conv_pw1_w[0].astype(BF16)
    pw1b = _row(conv_pw1_b[0])
    dw = jnp.pad(conv_dw_w[0].astype(F32), ((0, HIST_ROWS - CONV_WIDTH), (0, 0)))
    dwb, ln_g, ln_b = _row(conv_dw_b[0]), _row(conv_ln_g[0]), _row(conv_ln_b[0])
    pw2 = conv_pw2_w[0].astype(BF16)
    pw2b = _row(conv_pw2_b[0])
    w1 = [mlp_w1[l].astype(BF16) for l in range(mlp_w1.shape[0])]
    w2 = [mlp_w2[l].astype(BF16) for l in range(mlp_w2.shape[0])]
    wk, wv, wq, wo = w_k.astype(BF16), w_v.astype(BF16), w_q[0].astype(BF16), w_o[0].astype(BF16)
    lamv = jnp.pad(jnp.stack([lambda_q1[0], lambda_k1[0], lambda_q2[0], lambda_k2[0]]).astype(F32),
                   ((0, SUBLANES - 4), (0, LANES - HEAD_DIM)))
    subg = _row(subln_g[0])
    rel = rel_bias_table.astype(F32)

    g_mix0, g_mix1, g_mlp0, g_kv = _row(norm_mix_g[0]), _row(norm_mix_g[1]), _row(norm_mlp_g[0]), _row(kv_norm_g)

    def back(a2, h2):
        return _post_call(a2, h2, wo, _row(norm_mlp_g[1]), w1[1], w2[1], _row(final_norm_g))

    x_p = x_prompt.reshape(bsz * seq, d)
    g_p = _glu_call(x_p, g_mix0, pw1, pw1b)
    c_p = _conv_long_call(g_p.reshape(bsz, seq, d), dw, dwb, ln_g, ln_b).reshape(bsz * seq, d)
    h_p = _pw2_mlp_call(c_p, x_p, pw2, pw2b, g_mlp0, w1[0], w2[0])
    k_p, v_p, q_p, kb_p, va_p = _qkv_call(h_p, g_kv, g_mix1, wk, wv, wq)
    assert min(ATTN_BLOCK, seq) % MAX_DISTANCE == 0
    r = jnp.arange(MAX_DISTANCE, dtype=jnp.int32)[:, None]
    c = jnp.arange(MAX_DISTANCE, dtype=jnp.int32)[None, :]
    bias_p = jnp.stack([_shifted_bias(rel, r - c, r >= c),
                        _shifted_bias(rel, MAX_DISTANCE + r - c, jnp.full((MAX_DISTANCE,) * 2, True))], axis=1)
    a_p = _attn_call(q_p.reshape(bsz, seq, d), kb_p.reshape(bsz, seq, d),
                     va_p.reshape(bsz, seq, 2 * d), bias_p, lamv, subg)
    y_prompt = back(a_p.reshape(bsz * seq, d), h_p).reshape(bsz, seq, d)
    conv_state_prompt = g_p.reshape(bsz, seq, d)[:, seq - hist:][None]

    st_s = jnp.pad(state_conv[0].astype(F32), ((0, 0), (HIST_ROWS - hist, 0), (0, 0)))
    g_s, h_s, k_s, v_s, q_s = _short_front_call(
        x_sample.reshape(db * dseq, d), st_s, g_mix0, pw1, pw1b, dw, dwb, ln_g, ln_b,
        pw2, pw2b, g_mlp0, w1[0], w2[0], g_kv, g_mix1, wk, wv, wq)
    prow = page * N_HEADS
    n_cols = 2 * N_HEADS * dseq
    assert n_cols == LANES and dseq <= page
    col_q = jnp.tile(jnp.arange(dseq, dtype=jnp.int32), 2 * N_HEADS)[None, :]
    key = jnp.arange(page, dtype=jnp.int32)[:, None]
    pick = lambda b4: jnp.concatenate([b4[h, :, 2 * h * dseq:2 * (h + 1) * dseq] for h in range(N_HEADS)], axis=1)
    blast = pick(_shifted_bias(rel, page + col_q - key, jnp.full((page, n_cols), True)))
    bnew = pick(_shifted_bias(rel, col_q - key, (key <= col_q) & (key < dseq)))
    a_s = _decode_call(page_table.astype(jnp.int32), q_s.reshape(db, dseq, d),
                       cache_k.reshape(n_pool, prow, V_HEAD_DIM), cache_v.reshape(n_pool, prow, V_HEAD_DIM),
                       k_s.reshape(db, dseq, d), v_s.reshape(db, dseq, d), blast, bnew, lamv, subg)
    y_sample = back(a_s.reshape(db * dseq, d), h_s).reshape(db, dseq, d)
    conv_state_sample = jnp.concatenate([state_conv[0].astype(F32), g_s.reshape(db, dseq, d)],
                                        axis=1)[:, dseq:][None]

    kv_shape = (N_HEADS, V_HEAD_DIM)
    return (y_prompt, y_sample, conv_state_prompt, conv_state_sample,
            k_p.reshape(bsz, seq, *kv_shape), v_p.reshape(bsz, seq, *kv_shape),
            k_s.reshape(db, dseq, *kv_shape), v_s.reshape(db, dseq, *kv_shape))
```

```python
import functools
import math

import jax
import jax.numpy as jnp
from jax import lax
from jax.experimental import pallas as pl
from jax.experimental.pallas import tpu as pltpu

F32 = jnp.float32
BF16 = jnp.bfloat16

N_HEADS = 8
HEAD_DIM = 64
V_HEAD_DIM = 2 * HEAD_DIM
CONV_WIDTH = 31
N_BUCKETS = 32
MAX_DISTANCE = 128
NORM_EPS = 1e-6
NEG_INF = -1e30
LAM_INIT = 0.8 - 0.6 * math.exp(-0.3 * 1)
LOG2E = math.log2(math.e)
Q_SCALE = HEAD_DIM ** -0.5 * LOG2E

SUBLANES = 8
LANES = 128
HIST_ROWS = 32
VMEM_LIMIT = 56 * 1024 * 1024

TOKEN_TILE = 512
FF_CHUNK = 512
CONV_ROWS = 32
ATTN_BLOCK = 512
ATTN_HEADS = 2
PAGES_PER_STEP = 16
DECODE_GROUP = 4


def _params(n_axes):
    return pltpu.CompilerParams(dimension_semantics=("arbitrary",) * n_axes,
                                vmem_limit_bytes=VMEM_LIMIT)


def _resident(shape):
    nd = len(shape)
    return pl.BlockSpec(shape, lambda *_: (0,) * nd, pipeline_mode=pl.Buffered(1))


class _Layer:
    def __init__(self, stack, index):
        self.stack, self.index, self.shape = stack, index, stack.shape[1:]

    def spec(self):
        idx = (self.index,) + (0,) * len(self.shape)
        return pl.BlockSpec((None,) + self.shape, lambda *_: idx, pipeline_mode=pl.Buffered(1))


def _spec_of(a):
    return a.spec() if isinstance(a, _Layer) else _resident(a.shape)


def _array_of(a):
    return a.stack if isinstance(a, _Layer) else a


def _rms_unit(x):
    return x * lax.rsqrt(jnp.mean(x * x, axis=-1, keepdims=True) + NORM_EPS)


def _mlp(h, g_ref, w1_ref, w2_ref):
    hn = (_rms_unit(h) * g_ref[...]).astype(BF16)
    d_ff = w1_ref.shape[1]
    acc = h
    for c in range(d_ff // FF_CHUNK):
        u = jnp.dot(hn, w1_ref[:, c * FF_CHUNK:(c + 1) * FF_CHUNK], preferred_element_type=F32)
        u = jnp.maximum(u, 0.0)
        acc = acc + jnp.dot((u * u).astype(BF16), w2_ref[c * FF_CHUNK:(c + 1) * FF_CHUNK, :],
                            preferred_element_type=F32)
    return acc


def _glu(x, g_ref, w_ref, b_ref):
    d = x.shape[1]
    hn = (_rms_unit(x) * g_ref[...]).astype(BF16)
    u = jnp.dot(hn, w_ref[...], preferred_element_type=F32) + b_ref[...]
    return u[:, :d] * jax.nn.sigmoid(u[:, d:])


def _glu_kernel(x_ref, g_ref, w_ref, b_ref, o_ref):
    o_ref[...] = _glu(x_ref[...], g_ref, w_ref, b_ref)


def _glu_call(x2, g, w, b):
    n, d = x2.shape
    tm = min(TOKEN_TILE, n)
    return pl.pallas_call(
        _glu_kernel,
        grid=(n // tm,),
        in_specs=[pl.BlockSpec((tm, d), lambda i: (i, 0)),
                  _resident((1, d)), _resident((d, 2 * d)), _resident((1, 2 * d))],
        out_specs=pl.BlockSpec((tm, d), lambda i: (i, 0)),
        out_shape=jax.ShapeDtypeStruct((n, d), F32),
        compiler_params=_params(1),
        name="glu",
    )(x2, g, w, b)


def _ln_swish(c, ln_g_ref, ln_b_ref):
    mu = jnp.mean(c, axis=-1, keepdims=True)
    cc = c - mu
    var = jnp.mean(cc * cc, axis=-1, keepdims=True)
    y = cc * lax.rsqrt(var + NORM_EPS) * ln_g_ref[...] + ln_b_ref[...]
    return y * jax.nn.sigmoid(y)


def _conv_long_kernel(g_ref, hist_ref, dw_ref, dwb_ref, ln_g_ref, ln_b_ref, o_ref, sh_ref):
    tl = g_ref.shape[1]
    first = pl.program_id(1) == 0
    hist = jnp.where(first, 0.0, hist_ref[0])
    sh_ref[0, 0:HIST_ROWS, :] = hist
    sh_ref[0, HIST_ROWS:HIST_ROWS + tl, :] = g_ref[0]
    n_sh = tl + HIST_ROWS - SUBLANES
    for b in range(1, SUBLANES):
        sh_ref[b, 0:n_sh, :] = sh_ref[0, b:b + n_sh, :]

    def chunk(ci, carry):
        r0 = pl.multiple_of(ci * CONV_ROWS, CONV_ROWS)
        acc = jnp.broadcast_to(dwb_ref[...], (CONV_ROWS, dwb_ref.shape[1]))
        for k in range(CONV_WIDTH):
            off = HIST_ROWS - (CONV_WIDTH - 1) + k
            a, b = divmod(off, SUBLANES)
            acc = acc + dw_ref[k:k + 1, :] * sh_ref[b, pl.ds(r0 + a * SUBLANES, CONV_ROWS), :]
        o_ref[0, pl.ds(r0, CONV_ROWS), :] = _ln_swish(acc, ln_g_ref, ln_b_ref).astype(o_ref.dtype)
        return carry

    lax.fori_loop(0, tl // CONV_ROWS, chunk, 0, unroll=4)


def _conv_long_call(g3, dw, dwb, ln_g, ln_b):
    bsz, t, d = g3.shape
    tl = min(TOKEN_TILE, t)
    per = tl // HIST_ROWS
    return pl.pallas_call(
        _conv_long_kernel,
        grid=(bsz, t // tl),
        in_specs=[pl.BlockSpec((1, tl, d), lambda b, j: (b, j, 0)),
                  pl.BlockSpec((1, HIST_ROWS, d), lambda b, j: (b, jnp.maximum(j * per - 1, 0), 0)),
                  _resident((HIST_ROWS, d)), _resident((1, d)), _resident((1, d)), _resident((1, d))],
        out_specs=pl.BlockSpec((1, tl, d), lambda b, j: (b, j, 0)),
        out_shape=jax.ShapeDtypeStruct((bsz, t, d), BF16),
        scratch_shapes=[pltpu.VMEM((SUBLANES, tl + HIST_ROWS, d), F32)],
        compiler_params=_params(2),
        name="conv_long",
    )(g3, g3, dw, dwb, ln_g, ln_b)


def _short_front_kernel(x_ref, st_ref, g0_ref, pw1_ref, pw1b_ref, dw_ref, dwb_ref, ln_g_ref, ln_b_ref,
                        pw2_ref, pw2b_ref, gm_ref, w1_ref, w2_ref, gkv_ref, gq_ref, wk_ref, wv_ref, wq_ref,
                        g_ref, h_ref, k_ref, v_ref, q_ref, ext_ref):
    n, d = x_ref.shape
    s = st_ref.shape[0]
    t = n // s
    x = x_ref[...]
    g = _glu(x, g0_ref, pw1_ref, pw1b_ref)
    g_ref[...] = g
    ext_ref[:, 0:HIST_ROWS, :] = st_ref[...]
    ext_ref[:, HIST_ROWS:HIST_ROWS + t, :] = g.reshape(s, t, d)
    acc = jnp.broadcast_to(dwb_ref[...].reshape(1, 1, d), (s, t, d))
    for k in range(CONV_WIDTH):
        off = HIST_ROWS - (CONV_WIDTH - 1) + k
        acc = acc + dw_ref[k:k + 1, :].reshape(1, 1, d) * ext_ref[:, off:off + t, :]
    c = _ln_swish(acc.reshape(n, d), ln_g_ref, ln_b_ref).astype(BF16)
    h = x + jnp.dot(c, pw2_ref[...], preferred_element_type=F32) + pw2b_ref[...]
    h = _mlp(h, gm_ref, w1_ref, w2_ref)
    h_ref[...] = h
    k_ref[...], v_ref[...], q_ref[...] = _project_qkv(h, gkv_ref, gq_ref, wk_ref, wv_ref, wq_ref)


def _short_front_call(x2, st3, *params):
    n, d = x2.shape
    s = st3.shape[0]
    out = lambda dt: jax.ShapeDtypeStruct((n, d), dt)
    return pl.pallas_call(
        _short_front_kernel,
        grid=(1,),
        in_specs=[_spec_of(a) for a in (x2, st3) + params],
        out_specs=[pl.BlockSpec((n, d), lambda i: (0, 0))] * 5,
        out_shape=[out(F32), out(F32), out(F32), out(F32), out(BF16)],
        scratch_shapes=[pltpu.VMEM((s, HIST_ROWS + n // s, d), F32)],
        compiler_params=_params(1),
        name="short_front",
    )(x2, st3, *[_array_of(a) for a in params])


def _pw2_mlp_kernel(c_ref, x_ref, pw2_ref, pw2b_ref, gm_ref, w1_ref, w2_ref, o_ref):
    h = x_ref[...] + jnp.dot(c_ref[...], pw2_ref[...], preferred_element_type=F32) + pw2b_ref[...]
    o_ref[...] = _mlp(h, gm_ref, w1_ref, w2_ref)


def _pw2_mlp_call(c2, x2, pw2, pw2b, gm, w1, w2):
    n, d = x2.shape
    tm = min(TOKEN_TILE, n)
    row = lambda i: (i, 0)
    return pl.pallas_call(
        _pw2_mlp_kernel,
        grid=(n // tm,),
        in_specs=[pl.BlockSpec((tm, d), row), pl.BlockSpec((tm, d), row),
                  _resident((d, d)), _resident((1, d)), _resident((1, d)),
                  _spec_of(w1), _spec_of(w2)],
        out_specs=pl.BlockSpec((tm, d), row),
        out_shape=jax.ShapeDtypeStruct((n, d), F32),
        compiler_params=_params(1),
        name="pw2_mlp",
    )(c2, x2, pw2, pw2b, gm, _array_of(w1), _array_of(w2))


def _project_qkv(h, gkv_ref, gq_ref, wk_ref, wv_ref, wq_ref):
    y = _rms_unit(h)
    kvn = (y * gkv_ref[...]).astype(BF16)
    hn = (y * gq_ref[...]).astype(BF16)
    k = jnp.dot(kvn, wk_ref[...], preferred_element_type=F32)
    v = jnp.dot(kvn, wv_ref[...], preferred_element_type=F32)
    q = jnp.dot(hn, wq_ref[...], preferred_element_type=F32)
    return k, v, (q * Q_SCALE).astype(BF16)


def _qkv_kernel(h_ref, gkv_ref, gq_ref, wk_ref, wv_ref, wq_ref,
                k_ref, v_ref, q_ref, kb_ref, va_ref):
    tm, d = h_ref.shape
    k, v, q = _project_qkv(h_ref[...], gkv_ref, gq_ref, wk_ref, wv_ref, wq_ref)
    k_ref[...] = k
    v_ref[...] = v
    q_ref[...] = q
    kb_ref[...] = k.astype(BF16)
    ones_col = (lax.broadcasted_iota(jnp.int32, (tm, V_HEAD_DIM), 1) == 0).astype(BF16)
    vb = v.astype(BF16)
    for h in range(N_HEADS):
        va_ref[:, 2 * h * V_HEAD_DIM:(2 * h + 1) * V_HEAD_DIM] = vb[:, h * V_HEAD_DIM:(h + 1) * V_HEAD_DIM]
        va_ref[:, (2 * h + 1) * V_HEAD_DIM:(2 * h + 2) * V_HEAD_DIM] = ones_col


def _qkv_call(h2, gkv, gq, wk, wv, wq):
    n, d = h2.shape
    tm = min(TOKEN_TILE, n)
    row = lambda i: (i, 0)
    return pl.pallas_call(
        _qkv_kernel,
        grid=(n // tm,),
        in_specs=[pl.BlockSpec((tm, d), row), _resident((1, d)), _resident((1, d)),
                  _resident((d, d)), _resident((d, d)), _resident((d, d))],
        out_specs=[pl.BlockSpec((tm, d), row), pl.BlockSpec((tm, d), row), pl.BlockSpec((tm, d), row),
                   pl.BlockSpec((tm, d), row),
                   pl.BlockSpec((tm, 2 * d), row)],
        out_shape=[jax.ShapeDtypeStruct((n, d), F32), jax.ShapeDtypeStruct((n, d), F32),
                   jax.ShapeDtypeStruct((n, d), BF16), jax.ShapeDtypeStruct((n, d), BF16),
                   jax.ShapeDtypeStruct((n, 2 * d), BF16)],
        compiler_params=_params(1),
        name="qkv",
    )(h2, gkv, gq, wk, wv, wq)


def _lambda(lam_ref):
    lv = lam_ref[...]
    s1 = jnp.sum(lv[0:1, :] * lv[1:2, :], axis=-1, keepdims=True)
    s2 = jnp.sum(lv[2:3, :] * lv[3:4, :], axis=-1, keepdims=True)
    return jnp.exp(s1) - jnp.exp(s2) + LAM_INIT


def _sub_norm(o, subg_ref):
    return (_rms_unit(o) * subg_ref[...]) * (1.0 - LAM_INIT)


def _rel_bucket(n):
    max_exact = N_BUCKETS // 2
    nf = jnp.maximum(n, 1).astype(F32)
    large = max_exact + (jnp.log(nf / max_exact) / math.log(MAX_DISTANCE / max_exact)
                         * (N_BUCKETS - max_exact)).astype(jnp.int32)
    large = jnp.minimum(large, N_BUCKETS - 1)
    return jnp.where(n < max_exact, n, large)


def _shifted_bias(rel_table, dist, valid):
    bucket = _rel_bucket(jnp.maximum(dist, 0))[None]
    lead = (N_HEADS,) + (1,) * dist.ndim
    vals = jnp.zeros((N_HEADS,) + dist.shape, F32)
    for n in range(N_BUCKETS - 1):
        vals = jnp.where(bucket == n, (rel_table[n] - rel_table[N_BUCKETS - 1]).reshape(lead), vals)
    return jnp.where(valid[None], vals * LOG2E, NEG_INF)


def _attn_kernel(q_ref, k_ref, va_ref, bias_ref, lam_ref, subg_ref, o_ref, qz_ref, acc_ref, m_ref):
    blk = q_ref.shape[1]
    n_hd = q_ref.shape[2] // V_HEAD_DIM
    i = pl.program_id(2)
    for hd in range(n_hd):
        q = q_ref[0, :, hd * V_HEAD_DIM:(hd + 1) * V_HEAD_DIM]
        first_half = lax.broadcasted_iota(jnp.int32, q.shape, 1) < HEAD_DIM
        zero = jnp.zeros_like(q)
        qz_ref[2 * hd * blk:(2 * hd + 1) * blk, :] = jnp.where(first_half, q, zero)
        qz_ref[(2 * hd + 1) * blk:(2 * hd + 2) * blk, :] = jnp.where(first_half, zero, q)
    acc_ref[...] = jnp.zeros_like(acc_ref)
    m_ref[...] = jnp.full_like(m_ref, NEG_INF)
    reps = blk // LANES

    nt = (((1,), (1,)), ((), ()))
    chains = [(hd, slice((2 * hd + c) * blk, (2 * hd + c + 1) * blk)) for hd in range(n_hd) for c in range(2)]

    def scores(j):
        r0 = pl.multiple_of(j * blk, blk)
        return [lax.dot_general(qz_ref[rows, :], k_ref[0, pl.ds(r0, blk), hd * V_HEAD_DIM:(hd + 1) * V_HEAD_DIM],
                                nt, preferred_element_type=F32) for hd, rows in chains]

    sub = bias_ref.shape[2]
    nb = blk // sub

    def add_bias(sc, kind, hd):
        if kind == "prev":
            top = jnp.concatenate([sc[:sub, :blk - sub], sc[:sub, blk - sub:] + bias_ref[hd, 1]], axis=1)
            return jnp.concatenate([top, sc[sub:, :]], axis=0) if nb > 1 else top
        rows = []
        for r in range(nb):
            tiles = [bias_ref[hd, 0] if c == r else bias_ref[hd, 1] if c == r - 1 else
                     jnp.full((sub, sub), 0.0 if c < r else NEG_INF, F32) for c in range(nb)]
            rows.append(jnp.concatenate(tiles, axis=1))
        return sc + jnp.concatenate(rows, axis=0)

    def accumulate(j, s, bias):
        r0 = pl.multiple_of(j * blk, blk)
        if bias is not None:
            s = [add_bias(sc, bias, hd) for sc, (hd, _) in zip(s, chains)]
        m_prev = [m_ref[rows, :] for _, rows in chains]
        m_new = [jnp.maximum(mp, jnp.max(sc, axis=1, keepdims=True)) for mp, sc in zip(m_prev, s)]
        alpha = [jnp.exp2(mp - mn) for mp, mn in zip(m_prev, m_new)]
        p = [jnp.exp2(sc - jnp.concatenate([mn] * reps, axis=1)).astype(BF16) for sc, mn in zip(s, m_new)]
        pv = [jnp.dot(pc, va_ref[0, pl.ds(r0, blk), 2 * hd * V_HEAD_DIM:(2 * hd + 2) * V_HEAD_DIM],
                      preferred_element_type=F32) for pc, (hd, _) in zip(p, chains)]
        for (_, rows), al, pvc, mn in zip(chains, alpha, pv, m_new):
            acc_ref[rows, :] = jnp.concatenate([al, al], axis=1) * acc_ref[rows, :] + pvc
            m_ref[rows, :] = mn

    def process(blocks):
        s_all = [scores(j) for j, _ in blocks]
        for (j, bias), s in zip(blocks, s_all):
            accumulate(j, s, bias)

    n_far = jnp.maximum(i - 1, 0)

    def far_pair(t, carry):
        process([(2 * t, None), (2 * t + 1, None)])
        return carry

    lax.fori_loop(0, n_far // 2, far_pair, 0)
    odd = n_far % 2 == 1
    diag, prev = "diag", "prev"

    @pl.when(i == 0)
    def _():
        process([(i, diag)])

    @pl.when(jnp.logical_and(i > 0, jnp.logical_not(odd)))
    def _():
        process([(i - 1, prev), (i, diag)])

    @pl.when(odd)
    def _():
        process([(i - 2, None), (i - 1, prev), (i, diag)])

    lam = _lambda(lam_ref)
    for hd in range(n_hd):
        a1 = acc_ref[2 * hd * blk:(2 * hd + 1) * blk, :]
        a2 = acc_ref[(2 * hd + 1) * blk:(2 * hd + 2) * blk, :]
        o = (a1[:, :V_HEAD_DIM] / a1[:, V_HEAD_DIM:V_HEAD_DIM + 1]
             - lam * (a2[:, :V_HEAD_DIM] / a2[:, V_HEAD_DIM:V_HEAD_DIM + 1]))
        o_ref[0, :, hd * V_HEAD_DIM:(hd + 1) * V_HEAD_DIM] = _sub_norm(o, subg_ref).astype(o_ref.dtype)


def _attn_call(q3, k3, va3, bias, lamv, subg):
    bsz, t, d = q3.shape
    blk = min(ATTN_BLOCK, t)
    n_hd = ATTN_HEADS
    rows = 2 * n_hd * blk
    return pl.pallas_call(
        _attn_kernel,
        grid=(bsz, N_HEADS // n_hd, t // blk),
        in_specs=[pl.BlockSpec((1, blk, n_hd * V_HEAD_DIM), lambda b, h, i: (b, i, h)),
                  pl.BlockSpec((1, t, n_hd * V_HEAD_DIM), lambda b, h, i: (b, 0, h)),
                  pl.BlockSpec((1, t, 2 * n_hd * V_HEAD_DIM), lambda b, h, i: (b, 0, h)),
                  pl.BlockSpec((n_hd, 2, MAX_DISTANCE, MAX_DISTANCE), lambda b, h, i: (h, 0, 0, 0)),
                  _resident((SUBLANES, LANES)), _resident((1, V_HEAD_DIM))],
        out_specs=pl.BlockSpec((1, blk, n_hd * V_HEAD_DIM), lambda b, h, i: (b, i, h)),
        out_shape=jax.ShapeDtypeStruct((bsz, t, d), BF16),
        scratch_shapes=[pltpu.VMEM((rows, V_HEAD_DIM), BF16),
                        pltpu.VMEM((rows, 2 * V_HEAD_DIM), F32),
                        pltpu.VMEM((rows, LANES), F32)],
        compiler_params=_params(3),
        name="attn_prompt",
    )(q3, k3, va3, bias, lamv, subg)


def _decode_kernel(pt_ref, q_ref, *refs, n_pages_step):
    g_n = n_pages_step
    k_refs = refs[:g_n]
    v_refs = refs[g_n:2 * g_n]
    (knew_ref, vnew_ref, blast_ref, bnew_ref, lam_ref, subg_ref,
     o_ref, qbd_ref, acc_ref, m_ref, l_ref) = refs[2 * g_n:]
    st = pl.program_id(1)
    last = pl.num_programs(1) - 1
    tq = q_ref.shape[1]
    cols, d = qbd_ref.shape
    page = blast_ref.shape[0]

    @pl.when(st == 0)
    def _():
        qt = jnp.concatenate([q_ref[0]] * (cols // tq), axis=0)
        row = lax.broadcasted_iota(jnp.int32, (cols, d), 0)
        lane = lax.broadcasted_iota(jnp.int32, (cols, d), 1)
        qbd_ref[...] = jnp.where(row // tq == lane // HEAD_DIM, qt, jnp.zeros_like(qt))
        acc_ref[...] = jnp.zeros_like(acc_ref)
        m_ref[...] = jnp.full_like(m_ref, NEG_INF)
        l_ref[...] = jnp.zeros_like(l_ref)

    def slab(ref):
        return jnp.concatenate([ref[0, pl.ds(h, page, stride=N_HEADS), :] for h in range(N_HEADS)],
                               axis=1).astype(BF16)

    def per_row(v):
        return jnp.broadcast_to(v, (cols, cols)).T

    nt = (((1,), (1,)), ((), ()))
    tn = (((0,), (0,)), ((), ()))

    def scores(k):
        return lax.dot_general(k, qbd_ref[...], nt, preferred_element_type=F32)

    def update(s, v):
        m_prev = m_ref[...]
        m_new = jnp.maximum(m_prev, jnp.max(s, axis=0, keepdims=True))
        alpha = jnp.exp2(m_prev - m_new)
        p = jnp.exp2(s - m_new)
        l_ref[...] = alpha * l_ref[...] + jnp.sum(p, axis=0, keepdims=True)
        pv = lax.dot_general(p.astype(BF16), v, tn, preferred_element_type=F32)
        acc_ref[...] = jnp.concatenate([per_row(alpha)] * (d // cols), axis=1) * acc_ref[...] + pv
        m_ref[...] = m_new

    sub = math.gcd(DECODE_GROUP, g_n)
    groups = [slice(a, a + sub) for a in range(0, g_n, sub)]
    s_all = [scores(jnp.concatenate([slab(r) for r in k_refs[gs]], axis=0)) for gs in groups]
    is_last = (st == last).astype(F32)
    s_all[-1] = s_all[-1] + jnp.concatenate(
        [jnp.zeros(((sub - 1) * page, cols), F32), is_last * blast_ref[...]], axis=0)
    for gs, s in zip(groups, s_all):
        update(s, jnp.concatenate([slab(r) for r in v_refs[gs]], axis=0))

    def new_slab(ref):
        return jnp.concatenate([ref[0], jnp.zeros((page - tq, d), F32)], axis=0).astype(BF16)

    @pl.when(st == last)
    def _():
        update(scores(new_slab(knew_ref)) + bnew_ref[...], new_slab(vnew_ref))
        lam = _lambda(lam_ref)
        o_all = acc_ref[...] / jnp.concatenate([per_row(l_ref[...])] * (d // cols), axis=1)
        for h in range(N_HEADS):
            blk = o_all[2 * h * tq:(2 * h + 2) * tq, h * V_HEAD_DIM:(h + 1) * V_HEAD_DIM]
            o = blk[:tq] - lam * blk[tq:]
            o_ref[0, :, h * V_HEAD_DIM:(h + 1) * V_HEAD_DIM] = _sub_norm(o, subg_ref).astype(o_ref.dtype)


def _decode_call(page_table, q3, cache_k3, cache_v3, knew, vnew, blast, bnew, lamv, subg):
    db, n_pages = page_table.shape
    _, tq, d = q3.shape
    prow = cache_k3.shape[1]
    page = prow // N_HEADS
    g_n = math.gcd(PAGES_PER_STEP, n_pages)
    cols = N_HEADS * 2 * tq

    def page_spec(g):
        return pl.BlockSpec((1, prow, V_HEAD_DIM), lambda b, s, pt: (pt[b, s * g_n + g], 0, 0))

    per_b = lambda b, s, pt: (b, 0, 0)
    const2 = lambda b, s, pt: (0, 0)
    tile = pl.BlockSpec((page, cols), const2)
    grid_spec = pltpu.PrefetchScalarGridSpec(
        num_scalar_prefetch=1,
        grid=(db, n_pages // g_n),
        in_specs=([pl.BlockSpec((1, tq, d), per_b)]
                  + [page_spec(g) for g in range(g_n)] + [page_spec(g) for g in range(g_n)]
                  + [pl.BlockSpec((1, tq, d), per_b), pl.BlockSpec((1, tq, d), per_b),
                     tile, tile,
                     pl.BlockSpec((SUBLANES, LANES), const2), pl.BlockSpec((1, V_HEAD_DIM), const2)]),
        out_specs=pl.BlockSpec((1, tq, d), per_b),
        scratch_shapes=[pltpu.VMEM((cols, d), BF16), pltpu.VMEM((cols, d), F32),
                        pltpu.VMEM((1, cols), F32), pltpu.VMEM((1, cols), F32)],
    )
    return pl.pallas_call(
        functools.partial(_decode_kernel, n_pages_step=g_n),
        grid_spec=grid_spec,
        out_shape=jax.ShapeDtypeStruct((db, tq, d), BF16),
        compiler_params=_params(2),
        name="attn_decode",
    )(page_table, q3, *([cache_k3] * g_n), *([cache_v3] * g_n), knew, vnew, blast, bnew, lamv, subg)


def _post_kernel(a_ref, h_ref, wo_ref, gm_ref, w1_ref, w2_ref, gf_ref, o_ref):
    h = h_ref[...] + jnp.dot(a_ref[...], wo_ref[...], preferred_element_type=F32)
    h = _mlp(h, gm_ref, w1_ref, w2_ref)
    o_ref[...] = _rms_unit(h) * gf_ref[...]


def _post_call(a2, h2, wo, gm, w1, w2, gf):
    n, d = h2.shape
    tm = min(TOKEN_TILE, n)
    row = lambda i: (i, 0)
    return pl.pallas_call(
        _post_kernel,
        grid=(n // tm,),
        in_specs=[pl.BlockSpec((tm, d), row), pl.BlockSpec((tm, d), row),
                  _resident((d, d)), _resident((1, d)), _spec_of(w1), _spec_of(w2),
                  _resident((1, d))],
        out_specs=pl.BlockSpec((tm, d), row),
        out_shape=jax.ShapeDtypeStruct((n, d), F32),
        compiler_params=_params(1),
        name="post",
    )(a2, h2, wo, gm, _array_of(w1), _array_of(w2), gf)


def _row(v):
    return v.reshape(1, -1).astype(F32)


def kernel(x_prompt, x_sample, state_conv, cache_k, cache_v, page_table, norm_mix_g, norm_mlp_g, conv_pw1_w, conv_pw1_b, conv_dw_w, conv_dw_b, conv_ln_g, conv_ln_b, conv_pw2_w, conv_pw2_b, kv_norm_g, w_k, w_v, w_q, lambda_q1, lambda_k1, lambda_q2, lambda_k2, subln_g, w_o, rel_bias_table, mlp_w1, mlp_w2, final_norm_g):
    bsz, seq, d = x_prompt.shape
    db, dseq, _ = x_sample.shape
    n_pool, page = cache_k.shape[0], cache_k.shape[1]
    n_pages = page_table.shape[1]
    past_len = n_pages * page
    hist = CONV_WIDTH - 1

    pw1 = conv_pw1_w[0].astype(BF16)
    pw1b = _row(conv_pw1_b[0])
    dw = jnp.pad(conv_dw_w[0].astype(F32), ((0, HIST_ROWS - CONV_WIDTH), (0, 0)))
    dwb, ln_g, ln_b = _row(conv_dw_b[0]), _row(conv_ln_g[0]), _row(conv_ln_b[0])
    pw2 = conv_pw2_w[0].astype(BF16)
    pw2b = _row(conv_pw2_b[0])
    w1s, w2s = mlp_w1.astype(BF16), mlp_w2.astype(BF16)
    w1 = [_Layer(w1s, l) for l in range(mlp_w1.shape[0])]
    w2 = [_Layer(w2s, l) for l in range(mlp_w2.shape[0])]
    wk, wv, wq, wo = w_k.astype(BF16), w_v.astype(BF16), w_q[0].astype(BF16), w_o[0].astype(BF16)
    lamv = jnp.pad(jnp.stack([lambda_q1[0], lambda_k1[0], lambda_q2[0], lambda_k2[0]]).astype(F32),
                   ((0, SUBLANES - 4), (0, LANES - HEAD_DIM)))
    subg = _row(subln_g[0])
    rel = rel_bias_table.astype(F32)

    g_mix0, g_mix1, g_mlp0, g_kv = _row(norm_mix_g[0]), _row(norm_mix_g[1]), _row(norm_mlp_g[0]), _row(kv_norm_g)

    def back(a2, h2):
        return _post_call(a2, h2, wo, _row(norm_mlp_g[1]), w1[1], w2[1], _row(final_norm_g))

    x_p = x_prompt.reshape(bsz * seq, d)
    g_p = _glu_call(x_p, g_mix0, pw1, pw1b)
    c_p = _conv_long_call(g_p.reshape(bsz, seq, d), dw, dwb, ln_g, ln_b).reshape(bsz * seq, d)
    h_p = _pw2_mlp_call(c_p, x_p, pw2, pw2b, g_mlp0, w1[0], w2[0])
    k_p, v_p, q_p, kb_p, va_p = _qkv_call(h_p, g_kv, g_mix1, wk, wv, wq)
    assert min(ATTN_BLOCK, seq) % MAX_DISTANCE == 0
    r = jnp.arange(MAX_DISTANCE, dtype=jnp.int32)[:, None]
    c = jnp.arange(MAX_DISTANCE, dtype=jnp.int32)[None, :]
    bias_p = jnp.stack([_shifted_bias(rel, r - c, r >= c),
                        _shifted_bias(rel, MAX_DISTANCE + r - c, jnp.full((MAX_DISTANCE,) * 2, True))], axis=1)
    a_p = _attn_call(q_p.reshape(bsz, seq, d), kb_p.reshape(bsz, seq, d),
                     va_p.reshape(bsz, seq, 2 * d), bias_p, lamv, subg)
    y_prompt = back(a_p.reshape(bsz * seq, d), h_p).reshape(bsz, seq, d)
    conv_state_prompt = g_p.reshape(bsz, seq, d)[:, seq - hist:][None]

    st_s = jnp.pad(state_conv[0].astype(F32), ((0, 0), (HIST_ROWS - hist, 0), (0, 0)))
    g_s, h_s, k_s, v_s, q_s = _short_front_call(
        x_sample.reshape(db * dseq, d), st_s, g_mix0, pw1, pw1b, dw, dwb, ln_g, ln_b,
        pw2, pw2b, g_mlp0, w1[0], w2[0], g_kv, g_mix1, wk, wv, wq)
    prow = page * N_HEADS
    n_cols = 2 * N_HEADS * dseq
    assert n_cols == LANES and dseq <= page
    col_q = jnp.tile(jnp.arange(dseq, dtype=jnp.int32), 2 * N_HEADS)[None, :]
    key = jnp.arange(page, dtype=jnp.int32)[:, None]
    pick = lambda b4: jnp.concatenate([b4[h, :, 2 * h * dseq:2 * (h + 1) * dseq] for h in range(N_HEADS)], axis=1)
    blast = pick(_shifted_bias(rel, page + col_q - key, jnp.full((page, n_cols), True)))
    bnew = pick(_shifted_bias(rel, col_q - key, (key <= col_q) & (key < dseq)))
    a_s = _decode_call(page_table.astype(jnp.int32), q_s.reshape(db, dseq, d),
                       cache_k.reshape(n_pool, prow, V_HEAD_DIM), cache_v.reshape(n_pool, prow, V_HEAD_DIM),
                       k_s.reshape(db, dseq, d), v_s.reshape(db, dseq, d), blast, bnew, lamv, subg)
    y_sample = back(a_s.reshape(db * dseq, d), h_s).reshape(db, dseq, d)
    conv_state_sample = jnp.concatenate([state_conv[0].astype(F32), g_s.reshape(db, dseq, d)],
                                        axis=1)[:, dseq:][None]

    kv_shape = (N_HEADS, V_HEAD_DIM)
    return (y_prompt, y_sample, conv_state_prompt, conv_state_sample,
            k_p.reshape(bsz, seq, *kv_shape), v_p.reshape(bsz, seq, *kv_shape),
            k_s.reshape(db, dseq, *kv_shape), v_s.reshape(db, dseq, *kv_shape))
```

```python
import functools
import math

import jax
import jax.numpy as jnp
from jax import lax
from jax.experimental import pallas as pl
from jax.experimental.pallas import tpu as pltpu

F32 = jnp.float32
BF16 = jnp.bfloat16

N_HEADS = 8
HEAD_DIM = 64
V_HEAD_DIM = 2 * HEAD_DIM
CONV_WIDTH = 31
N_BUCKETS = 32
MAX_DISTANCE = 128
NORM_EPS = 1e-6
NEG_INF = -1e30
LAM_INIT = 0.8 - 0.6 * math.exp(-0.3 * 1)
LOG2E = math.log2(math.e)
Q_SCALE = HEAD_DIM ** -0.5 * LOG2E

SUBLANES = 8
LANES = 128
HIST_ROWS = 32
VMEM_LIMIT = 56 * 1024 * 1024

TOKEN_TILE = 512
WIDE_TILE = 1024
FF_CHUNK = 512
CONV_ROWS = 32
ATTN_BLOCK = 512
ATTN_HEADS = 2
PAGES_PER_STEP = 16
DECODE_GROUP = 4


def _params(n_axes):
    return pltpu.CompilerParams(dimension_semantics=("arbitrary",) * n_axes,
                                vmem_limit_bytes=VMEM_LIMIT)


def _resident(shape):
    nd = len(shape)
    return pl.BlockSpec(shape, lambda *_: (0,) * nd, pipeline_mode=pl.Buffered(1))


class _Layer:
    def __init__(self, stack, index):
        self.stack, self.index, self.shape = stack, index, stack.shape[1:]

    def spec(self):
        idx = (self.index,) + (0,) * len(self.shape)
        return pl.BlockSpec((None,) + self.shape, lambda *_: idx, pipeline_mode=pl.Buffered(1))


def _spec_of(a):
    return a.spec() if isinstance(a, _Layer) else _resident(a.shape)


def _array_of(a):
    return a.stack if isinstance(a, _Layer) else a


def _rms_unit(x):
    return x * lax.rsqrt(jnp.mean(x * x, axis=-1, keepdims=True) + NORM_EPS)


def _mlp(h, g_ref, w1_ref, w2_ref):
    hn = (_rms_unit(h) * g_ref[...]).astype(BF16)
    d_ff = w1_ref.shape[1]
    acc = h
    for c in range(d_ff // FF_CHUNK):
        u = jnp.dot(hn, w1_ref[:, c * FF_CHUNK:(c + 1) * FF_CHUNK], preferred_element_type=F32)
        u = jnp.maximum(u, 0.0)
        acc = acc + jnp.dot((u * u).astype(BF16), w2_ref[c * FF_CHUNK:(c + 1) * FF_CHUNK, :],
                            preferred_element_type=F32)
    return acc


def _glu(x, g_ref, w_ref, b_ref):
    d = x.shape[1]
    hn = (_rms_unit(x) * g_ref[...]).astype(BF16)
    u = jnp.dot(hn, w_ref[...], preferred_element_type=F32) + b_ref[...]
    return u[:, :d] * jax.nn.sigmoid(u[:, d:])


def _glu_kernel(x_ref, g_ref, w_ref, b_ref, o_ref):
    o_ref[...] = _glu(x_ref[...], g_ref, w_ref, b_ref)


def _glu_call(x2, g, w, b):
    n, d = x2.shape
    tm = min(WIDE_TILE, n)
    return pl.pallas_call(
        _glu_kernel,
        grid=(n // tm,),
        in_specs=[pl.BlockSpec((tm, d), lambda i: (i, 0)),
                  _resident((1, d)), _resident((d, 2 * d)), _resident((1, 2 * d))],
        out_specs=pl.BlockSpec((tm, d), lambda i: (i, 0)),
        out_shape=jax.ShapeDtypeStruct((n, d), F32),
        compiler_params=_params(1),
        name="glu",
    )(x2, g, w, b)


def _ln_swish(c, ln_g_ref, ln_b_ref):
    mu = jnp.mean(c, axis=-1, keepdims=True)
    cc = c - mu
    var = jnp.mean(cc * cc, axis=-1, keepdims=True)
    y = cc * lax.rsqrt(var + NORM_EPS) * ln_g_ref[...] + ln_b_ref[...]
    return y * jax.nn.sigmoid(y)


def _conv_long_kernel(g_ref, hist_ref, dw_ref, dwb_ref, ln_g_ref, ln_b_ref, o_ref, sh_ref):
    tl = g_ref.shape[1]
    first = pl.program_id(1) == 0
    hist = jnp.where(first, 0.0, hist_ref[0])
    sh_ref[0, 0:HIST_ROWS, :] = hist
    sh_ref[0, HIST_ROWS:HIST_ROWS + tl, :] = g_ref[0]
    n_sh = tl + HIST_ROWS - SUBLANES
    for b in range(1, SUBLANES):
        sh_ref[b, 0:n_sh, :] = sh_ref[0, b:b + n_sh, :]

    def chunk(ci, carry):
        r0 = pl.multiple_of(ci * CONV_ROWS, CONV_ROWS)
        acc = jnp.broadcast_to(dwb_ref[...], (CONV_ROWS, dwb_ref.shape[1]))
        for k in range(CONV_WIDTH):
            off = HIST_ROWS - (CONV_WIDTH - 1) + k
            a, b = divmod(off, SUBLANES)
            acc = acc + dw_ref[k:k + 1, :] * sh_ref[b, pl.ds(r0 + a * SUBLANES, CONV_ROWS), :]
        o_ref[0, pl.ds(r0, CONV_ROWS), :] = _ln_swish(acc, ln_g_ref, ln_b_ref).astype(o_ref.dtype)
        return carry

    lax.fori_loop(0, tl // CONV_ROWS, chunk, 0, unroll=4)


def _conv_long_call(g3, dw, dwb, ln_g, ln_b):
    bsz, t, d = g3.shape
    tl = min(TOKEN_TILE, t)
    per = tl // HIST_ROWS
    return pl.pallas_call(
        _conv_long_kernel,
        grid=(bsz, t // tl),
        in_specs=[pl.BlockSpec((1, tl, d), lambda b, j: (b, j, 0)),
                  pl.BlockSpec((1, HIST_ROWS, d), lambda b, j: (b, jnp.maximum(j * per - 1, 0), 0)),
                  _resident((HIST_ROWS, d)), _resident((1, d)), _resident((1, d)), _resident((1, d))],
        out_specs=pl.BlockSpec((1, tl, d), lambda b, j: (b, j, 0)),
        out_shape=jax.ShapeDtypeStruct((bsz, t, d), BF16),
        scratch_shapes=[pltpu.VMEM((SUBLANES, tl + HIST_ROWS, d), F32)],
        compiler_params=_params(2),
        name="conv_long",
    )(g3, g3, dw, dwb, ln_g, ln_b)


def _short_front_kernel(x_ref, st_ref, g0_ref, pw1_ref, pw1b_ref, dw_ref, dwb_ref, ln_g_ref, ln_b_ref,
                        pw2_ref, pw2b_ref, gm_ref, w1_ref, w2_ref, gkv_ref, gq_ref, wk_ref, wv_ref, wq_ref,
                        g_ref, h_ref, k_ref, v_ref, q_ref, ext_ref):
    n, d = x_ref.shape
    s = st_ref.shape[0]
    t = n // s
    x = x_ref[...]
    g = _glu(x, g0_ref, pw1_ref, pw1b_ref)
    g_ref[...] = g
    ext_ref[:, 0:HIST_ROWS, :] = st_ref[...]
    ext_ref[:, HIST_ROWS:HIST_ROWS + t, :] = g.reshape(s, t, d)
    acc = jnp.broadcast_to(dwb_ref[...].reshape(1, 1, d), (s, t, d))
    for k in range(CONV_WIDTH):
        off = HIST_ROWS - (CONV_WIDTH - 1) + k
        acc = acc + dw_ref[k:k + 1, :].reshape(1, 1, d) * ext_ref[:, off:off + t, :]
    c = _ln_swish(acc.reshape(n, d), ln_g_ref, ln_b_ref).astype(BF16)
    h = x + jnp.dot(c, pw2_ref[...], preferred_element_type=F32) + pw2b_ref[...]
    h = _mlp(h, gm_ref, w1_ref, w2_ref)
    h_ref[...] = h
    k_ref[...], v_ref[...], q_ref[...] = _project_qkv(h, gkv_ref, gq_ref, wk_ref, wv_ref, wq_ref)


def _short_front_call(x2, st3, *params):
    n, d = x2.shape
    s = st3.shape[0]
    out = lambda dt: jax.ShapeDtypeStruct((n, d), dt)
    return pl.pallas_call(
        _short_front_kernel,
        grid=(1,),
        in_specs=[_spec_of(a) for a in (x2, st3) + params],
        out_specs=[pl.BlockSpec((n, d), lambda i: (0, 0))] * 5,
        out_shape=[out(F32), out(F32), out(F32), out(F32), out(BF16)],
        scratch_shapes=[pltpu.VMEM((s, HIST_ROWS + n // s, d), F32)],
        compiler_params=_params(1),
        name="short_front",
    )(x2, st3, *[_array_of(a) for a in params])


def _pw2_mlp_kernel(c_ref, x_ref, pw2_ref, pw2b_ref, gm_ref, w1_ref, w2_ref, o_ref):
    h = x_ref[...] + jnp.dot(c_ref[...], pw2_ref[...], preferred_element_type=F32) + pw2b_ref[...]
    o_ref[...] = _mlp(h, gm_ref, w1_ref, w2_ref)


def _pw2_mlp_call(c2, x2, pw2, pw2b, gm, w1, w2):
    n, d = x2.shape
    tm = min(WIDE_TILE, n)
    row = lambda i: (i, 0)
    return pl.pallas_call(
        _pw2_mlp_kernel,
        grid=(n // tm,),
        in_specs=[pl.BlockSpec((tm, d), row), pl.BlockSpec((tm, d), row),
                  _resident((d, d)), _resident((1, d)), _resident((1, d)),
                  _spec_of(w1), _spec_of(w2)],
        out_specs=pl.BlockSpec((tm, d), row),
        out_shape=jax.ShapeDtypeStruct((n, d), F32),
        compiler_params=_params(1),
        name="pw2_mlp",
    )(c2, x2, pw2, pw2b, gm, _array_of(w1), _array_of(w2))


def _project_qkv(h, gkv_ref, gq_ref, wk_ref, wv_ref, wq_ref):
    y = _rms_unit(h)
    kvn = (y * gkv_ref[...]).astype(BF16)
    hn = (y * gq_ref[...]).astype(BF16)
    k = jnp.dot(kvn, wk_ref[...], preferred_element_type=F32)
    v = jnp.dot(kvn, wv_ref[...], preferred_element_type=F32)
    q = jnp.dot(hn, wq_ref[...], preferred_element_type=F32)
    return k, v, (q * Q_SCALE).astype(BF16)


def _qkv_kernel(h_ref, gkv_ref, gq_ref, wk_ref, wv_ref, wq_ref,
                k_ref, v_ref, q_ref, kb_ref, va_ref):
    tm, d = h_ref.shape
    k, v, q = _project_qkv(h_ref[...], gkv_ref, gq_ref, wk_ref, wv_ref, wq_ref)
    k_ref[...] = k
    v_ref[...] = v
    q_ref[...] = q
    kb_ref[...] = k.astype(BF16)
    ones_col = (lax.broadcasted_iota(jnp.int32, (tm, V_HEAD_DIM), 1) == 0).astype(BF16)
    vb = v.astype(BF16)
    for h in range(N_HEADS):
        va_ref[:, 2 * h * V_HEAD_DIM:(2 * h + 1) * V_HEAD_DIM] = vb[:, h * V_HEAD_DIM:(h + 1) * V_HEAD_DIM]
        va_ref[:, (2 * h + 1) * V_HEAD_DIM:(2 * h + 2) * V_HEAD_DIM] = ones_col


def _qkv_call(h2, gkv, gq, wk, wv, wq):
    n, d = h2.shape
    tm = min(TOKEN_TILE, n)
    row = lambda i: (i, 0)
    return pl.pallas_call(
        _qkv_kernel,
        grid=(n // tm,),
        in_specs=[pl.BlockSpec((tm, d), row), _resident((1, d)), _resident((1, d)),
                  _resident((d, d)), _resident((d, d)), _resident((d, d))],
        out_specs=[pl.BlockSpec((tm, d), row), pl.BlockSpec((tm, d), row), pl.BlockSpec((tm, d), row),
                   pl.BlockSpec((tm, d), row),
                   pl.BlockSpec((tm, 2 * d), row)],
        out_shape=[jax.ShapeDtypeStruct((n, d), F32), jax.ShapeDtypeStruct((n, d), F32),
                   jax.ShapeDtypeStruct((n, d), BF16), jax.ShapeDtypeStruct((n, d), BF16),
                   jax.ShapeDtypeStruct((n, 2 * d), BF16)],
        compiler_params=_params(1),
        name="qkv",
    )(h2, gkv, gq, wk, wv, wq)


def _lambda(lam_ref):
    lv = lam_ref[...]
    s1 = jnp.sum(lv[0:1, :] * lv[1:2, :], axis=-1, keepdims=True)
    s2 = jnp.sum(lv[2:3, :] * lv[3:4, :], axis=-1, keepdims=True)
    return jnp.exp(s1) - jnp.exp(s2) + LAM_INIT


def _sub_norm(o, subg_ref):
    return (_rms_unit(o) * subg_ref[...]) * (1.0 - LAM_INIT)


def _rel_bucket(n):
    max_exact = N_BUCKETS // 2
    nf = jnp.maximum(n, 1).astype(F32)
    large = max_exact + (jnp.log(nf / max_exact) / math.log(MAX_DISTANCE / max_exact)
                         * (N_BUCKETS - max_exact)).astype(jnp.int32)
    large = jnp.minimum(large, N_BUCKETS - 1)
    return jnp.where(n < max_exact, n, large)


def _shifted_bias(rel_table, dist, valid):
    bucket = _rel_bucket(jnp.maximum(dist, 0))[None]
    lead = (N_HEADS,) + (1,) * dist.ndim
    vals = jnp.zeros((N_HEADS,) + dist.shape, F32)
    for n in range(N_BUCKETS - 1):
        vals = jnp.where(bucket == n, (rel_table[n] - rel_table[N_BUCKETS - 1]).reshape(lead), vals)
    return jnp.where(valid[None], vals * LOG2E, NEG_INF)


def _attn_kernel(q_ref, k_ref, va_ref, bias_ref, lam_ref, subg_ref, o_ref, qz_ref, acc_ref, m_ref):
    blk = q_ref.shape[1]
    n_hd = q_ref.shape[2] // V_HEAD_DIM
    i = pl.program_id(2)
    for hd in range(n_hd):
        q = q_ref[0, :, hd * V_HEAD_DIM:(hd + 1) * V_HEAD_DIM]
        first_half = lax.broadcasted_iota(jnp.int32, q.shape, 1) < HEAD_DIM
        zero = jnp.zeros_like(q)
        qz_ref[2 * hd * blk:(2 * hd + 1) * blk, :] = jnp.where(first_half, q, zero)
        qz_ref[(2 * hd + 1) * blk:(2 * hd + 2) * blk, :] = jnp.where(first_half, zero, q)
    acc_ref[...] = jnp.zeros_like(acc_ref)
    m_ref[...] = jnp.full_like(m_ref, NEG_INF)
    reps = blk // LANES

    nt = (((1,), (1,)), ((), ()))
    chains = [(hd, slice((2 * hd + c) * blk, (2 * hd + c + 1) * blk)) for hd in range(n_hd) for c in range(2)]

    def scores(j):
        r0 = pl.multiple_of(j * blk, blk)
        return [lax.dot_general(qz_ref[rows, :], k_ref[0, pl.ds(r0, blk), hd * V_HEAD_DIM:(hd + 1) * V_HEAD_DIM],
                                nt, preferred_element_type=F32) for hd, rows in chains]

    sub = bias_ref.shape[2]
    nb = blk // sub

    def add_bias(sc, kind, hd):
        if kind == "prev":
            top = jnp.concatenate([sc[:sub, :blk - sub], sc[:sub, blk - sub:] + bias_ref[hd, 1]], axis=1)
            return jnp.concatenate([top, sc[sub:, :]], axis=0) if nb > 1 else top
        rows = []
        for r in range(nb):
            tiles = [bias_ref[hd, 0] if c == r else bias_ref[hd, 1] if c == r - 1 else
                     jnp.full((sub, sub), 0.0 if c < r else NEG_INF, F32) for c in range(nb)]
            rows.append(jnp.concatenate(tiles, axis=1))
        return sc + jnp.concatenate(rows, axis=0)

    def accumulate(j, s, bias):
        r0 = pl.multiple_of(j * blk, blk)
        if bias is not None:
            s = [add_bias(sc, bias, hd) for sc, (hd, _) in zip(s, chains)]
        m_prev = [m_ref[rows, :] for _, rows in chains]
        m_new = [jnp.maximum(mp, jnp.max(sc, axis=1, keepdims=True)) for mp, sc in zip(m_prev, s)]
        alpha = [jnp.exp2(mp - mn) for mp, mn in zip(m_prev, m_new)]
        p = [jnp.exp2(sc - jnp.concatenate([mn] * reps, axis=1)).astype(BF16) for sc, mn in zip(s, m_new)]
        pv = [jnp.dot(pc, va_ref[0, pl.ds(r0, blk), 2 * hd * V_HEAD_DIM:(2 * hd + 2) * V_HEAD_DIM],
                      preferred_element_type=F32) for pc, (hd, _) in zip(p, chains)]
        for (_, rows), al, pvc, mn in zip(chains, alpha, pv, m_new):
            acc_ref[rows, :] = jnp.concatenate([al, al], axis=1) * acc_ref[rows, :] + pvc
            m_ref[rows, :] = mn

    def process(blocks):
        s_all = [scores(j) for j, _ in blocks]
        for (j, bias), s in zip(blocks, s_all):
            accumulate(j, s, bias)

    n_far = jnp.maximum(i - 1, 0)

    def far_pair(t, carry):
        process([(2 * t, None), (2 * t + 1, None)])
        return carry

    lax.fori_loop(0, n_far // 2, far_pair, 0)
    odd = n_far % 2 == 1
    diag, prev = "diag", "prev"

    @pl.when(i == 0)
    def _():
        process([(i, diag)])

    @pl.when(jnp.logical_and(i > 0, jnp.logical_not(odd)))
    def _():
        process([(i - 1, prev), (i, diag)])

    @pl.when(odd)
    def _():
        process([(i - 2, None), (i - 1, prev), (i, diag)])

    lam = _lambda(lam_ref)
    for hd in range(n_hd):
        a1 = acc_ref[2 * hd * blk:(2 * hd + 1) * blk, :]
        a2 = acc_ref[(2 * hd + 1) * blk:(2 * hd + 2) * blk, :]
        o = (a1[:, :V_HEAD_DIM] / a1[:, V_HEAD_DIM:V_HEAD_DIM + 1]
             - lam * (a2[:, :V_HEAD_DIM] / a2[:, V_HEAD_DIM:V_HEAD_DIM + 1]))
        o_ref[0, :, hd * V_HEAD_DIM:(hd + 1) * V_HEAD_DIM] = _sub_norm(o, subg_ref).astype(o_ref.dtype)


def _attn_call(q3, k3, va3, bias, lamv, subg):
    bsz, t, d = q3.shape
    blk = min(ATTN_BLOCK, t)
    n_hd = ATTN_HEADS
    rows = 2 * n_hd * blk
    return pl.pallas_call(
        _attn_kernel,
        grid=(bsz, N_HEADS // n_hd, t // blk),
        in_specs=[pl.BlockSpec((1, blk, n_hd * V_HEAD_DIM), lambda b, h, i: (b, i, h)),
                  pl.BlockSpec((1, t, n_hd * V_HEAD_DIM), lambda b, h, i: (b, 0, h)),
                  pl.BlockSpec((1, t, 2 * n_hd * V_HEAD_DIM), lambda b, h, i: (b, 0, h)),
                  pl.BlockSpec((n_hd, 2, MAX_DISTANCE, MAX_DISTANCE), lambda b, h, i: (h, 0, 0, 0)),
                  _resident((SUBLANES, LANES)), _resident((1, V_HEAD_DIM))],
        out_specs=pl.BlockSpec((1, blk, n_hd * V_HEAD_DIM), lambda b, h, i: (b, i, h)),
        out_shape=jax.ShapeDtypeStruct((bsz, t, d), BF16),
        scratch_shapes=[pltpu.VMEM((rows, V_HEAD_DIM), BF16),
                        pltpu.VMEM((rows, 2 * V_HEAD_DIM), F32),
                        pltpu.VMEM((rows, LANES), F32)],
        compiler_params=_params(3),
        name="attn_prompt",
    )(q3, k3, va3, bias, lamv, subg)


def _decode_kernel(pt_ref, q_ref, *refs, n_pages_step):
    g_n = n_pages_step
    k_refs = refs[:g_n]
    v_refs = refs[g_n:2 * g_n]
    (knew_ref, vnew_ref, blast_ref, bnew_ref, lam_ref, subg_ref,
     o_ref, qbd_ref, acc_ref, m_ref, l_ref) = refs[2 * g_n:]
    st = pl.program_id(1)
    last = pl.num_programs(1) - 1
    tq = q_ref.shape[1]
    cols, d = qbd_ref.shape
    page = blast_ref.shape[0]

    @pl.when(st == 0)
    def _():
        qt = jnp.concatenate([q_ref[0]] * (cols // tq), axis=0)
        row = lax.broadcasted_iota(jnp.int32, (cols, d), 0)
        lane = lax.broadcasted_iota(jnp.int32, (cols, d), 1)
        qbd_ref[...] = jnp.where(row // tq == lane // HEAD_DIM, qt, jnp.zeros_like(qt))
        acc_ref[...] = jnp.zeros_like(acc_ref)
        m_ref[...] = jnp.full_like(m_ref, NEG_INF)
        l_ref[...] = jnp.zeros_like(l_ref)

    def slab(ref):
        return jnp.concatenate([ref[0, pl.ds(h, page, stride=N_HEADS), :] for h in range(N_HEADS)],
                               axis=1).astype(BF16)

    def per_row(v):
        return jnp.broadcast_to(v, (cols, cols)).T

    nt = (((1,), (1,)), ((), ()))
    tn = (((0,), (0,)), ((), ()))

    def scores(k):
        return lax.dot_general(k, qbd_ref[...], nt, preferred_element_type=F32)

    def update(s, v):
        m_prev = m_ref[...]
        m_new = jnp.maximum(m_prev, jnp.max(s, axis=0, keepdims=True))
        alpha = jnp.exp2(m_prev - m_new)
        p = jnp.exp2(s - m_new)
        l_ref[...] = alpha * l_ref[...] + jnp.sum(p, axis=0, keepdims=True)
        pv = lax.dot_general(p.astype(BF16), v, tn, preferred_element_type=F32)
        acc_ref[...] = jnp.concatenate([per_row(alpha)] * (d // cols), axis=1) * acc_ref[...] + pv
        m_ref[...] = m_new

    sub = math.gcd(DECODE_GROUP, g_n)
    groups = [slice(a, a + sub) for a in range(0, g_n, sub)]
    s_all = [scores(jnp.concatenate([slab(r) for r in k_refs[gs]], axis=0)) for gs in groups]
    is_last = (st == last).astype(F32)
    s_all[-1] = s_all[-1] + jnp.concatenate(
        [jnp.zeros(((sub - 1) * page, cols), F32), is_last * blast_ref[...]], axis=0)
    for gs, s in zip(groups, s_all):
        update(s, jnp.concatenate([slab(r) for r in v_refs[gs]], axis=0))

    def new_slab(ref):
        return jnp.concatenate([ref[0], jnp.zeros((page - tq, d), F32)], axis=0).astype(BF16)

    @pl.when(st == last)
    def _():
        update(scores(new_slab(knew_ref)) + bnew_ref[...], new_slab(vnew_ref))
        lam = _lambda(lam_ref)
        o_all = acc_ref[...] / jnp.concatenate([per_row(l_ref[...])] * (d // cols), axis=1)
        for h in range(N_HEADS):
            blk = o_all[2 * h * tq:(2 * h + 2) * tq, h * V_HEAD_DIM:(h + 1) * V_HEAD_DIM]
            o = blk[:tq] - lam * blk[tq:]
            o_ref[0, :, h * V_HEAD_DIM:(h + 1) * V_HEAD_DIM] = _sub_norm(o, subg_ref).astype(o_ref.dtype)


def _decode_call(page_table, q3, cache_k3, cache_v3, knew, vnew, blast, bnew, lamv, subg):
    db, n_pages = page_table.shape
    _, tq, d = q3.shape
    prow = cache_k3.shape[1]
    page = prow // N_HEADS
    g_n = math.gcd(PAGES_PER_STEP, n_pages)
    cols = N_HEADS * 2 * tq

    def page_spec(g):
        return pl.BlockSpec((1, prow, V_HEAD_DIM), lambda b, s, pt: (pt[b, s * g_n + g], 0, 0))

    per_b = lambda b, s, pt: (b, 0, 0)
    const2 = lambda b, s, pt: (0, 0)
    tile = pl.BlockSpec((page, cols), const2)
    grid_spec = pltpu.PrefetchScalarGridSpec(
        num_scalar_prefetch=1,
        grid=(db, n_pages // g_n),
        in_specs=([pl.BlockSpec((1, tq, d), per_b)]
                  + [page_spec(g) for g in range(g_n)] + [page_spec(g) for g in range(g_n)]
                  + [pl.BlockSpec((1, tq, d), per_b), pl.BlockSpec((1, tq, d), per_b),
                     tile, tile,
                     pl.BlockSpec((SUBLANES, LANES), const2), pl.BlockSpec((1, V_HEAD_DIM), const2)]),
        out_specs=pl.BlockSpec((1, tq, d), per_b),
        scratch_shapes=[pltpu.VMEM((cols, d), BF16), pltpu.VMEM((cols, d), F32),
                        pltpu.VMEM((1, cols), F32), pltpu.VMEM((1, cols), F32)],
    )
    return pl.pallas_call(
        functools.partial(_decode_kernel, n_pages_step=g_n),
        grid_spec=grid_spec,
        out_shape=jax.ShapeDtypeStruct((db, tq, d), BF16),
        compiler_params=_params(2),
        name="attn_decode",
    )(page_table, q3, *([cache_k3] * g_n), *([cache_v3] * g_n), knew, vnew, blast, bnew, lamv, subg)


def _post_kernel(a_ref, h_ref, wo_ref, gm_ref, w1_ref, w2_ref, gf_ref, o_ref):
    h = h_ref[...] + jnp.dot(a_ref[...], wo_ref[...], preferred_element_type=F32)
    h = _mlp(h, gm_ref, w1_ref, w2_ref)
    o_ref[...] = _rms_unit(h) * gf_ref[...]


def _post_call(a2, h2, wo, gm, w1, w2, gf):
    n, d = h2.shape
    tm = min(WIDE_TILE, n)
    row = lambda i: (i, 0)
    return pl.pallas_call(
        _post_kernel,
        grid=(n // tm,),
        in_specs=[pl.BlockSpec((tm, d), row), pl.BlockSpec((tm, d), row),
                  _resident((d, d)), _resident((1, d)), _spec_of(w1), _spec_of(w2),
                  _resident((1, d))],
        out_specs=pl.BlockSpec((tm, d), row),
        out_shape=jax.ShapeDtypeStruct((n, d), F32),
        compiler_params=_params(1),
        name="post",
    )(a2, h2, wo, gm, _array_of(w1), _array_of(w2), gf)


def _row(v):
    return v.reshape(1, -1).astype(F32)


def kernel(x_prompt, x_sample, state_conv, cache_k, cache_v, page_table, norm_mix_g, norm_mlp_g, conv_pw1_w, conv_pw1_b, conv_dw_w, conv_dw_b, conv_ln_g, conv_ln_b, conv_pw2_w, conv_pw2_b, kv_norm_g, w_k, w_v, w_q, lambda_q1, lambda_k1, lambda_q2, lambda_k2, subln_g, w_o, rel_bias_table, mlp_w1, mlp_w2, final_norm_g):
    bsz, seq, d = x_prompt.shape
    db, dseq, _ = x_sample.shape
    n_pool, page = cache_k.shape[0], cache_k.shape[1]
    n_pages = page_table.shape[1]
    past_len = n_pages * page
    hist = CONV_WIDTH - 1

    pw1 = conv_pw1_w[0].astype(BF16)
    pw1b = _row(conv_pw1_b[0])
    dw = jnp.pad(conv_dw_w[0].astype(F32), ((0, HIST_ROWS - CONV_WIDTH), (0, 0)))
    dwb, ln_g, ln_b = _row(conv_dw_b[0]), _row(conv_ln_g[0]), _row(conv_ln_b[0])
    pw2 = conv_pw2_w[0].astype(BF16)
    pw2b = _row(conv_pw2_b[0])
    w1s, w2s = mlp_w1.astype(BF16), mlp_w2.astype(BF16)
    w1 = [_Layer(w1s, l) for l in range(mlp_w1.shape[0])]
    w2 = [_Layer(w2s, l) for l in range(mlp_w2.shape[0])]
    wk, wv, wq, wo = w_k.astype(BF16), w_v.astype(BF16), w_q[0].astype(BF16), w_o[0].astype(BF16)
    lamv = jnp.pad(jnp.stack([lambda_q1[0], lambda_k1[0], lambda_q2[0], lambda_k2[0]]).astype(F32),
                   ((0, SUBLANES - 4), (0, LANES - HEAD_DIM)))
    subg = _row(subln_g[0])
    rel = rel_bias_table.astype(F32)

    g_mix0, g_mix1, g_mlp0, g_kv = _row(norm_mix_g[0]), _row(norm_mix_g[1]), _row(norm_mlp_g[0]), _row(kv_norm_g)

    def back(a2, h2):
        return _post_call(a2, h2, wo, _row(norm_mlp_g[1]), w1[1], w2[1], _row(final_norm_g))

    x_p = x_prompt.reshape(bsz * seq, d)
    g_p = _glu_call(x_p, g_mix0, pw1, pw1b)
    c_p = _conv_long_call(g_p.reshape(bsz, seq, d), dw, dwb, ln_g, ln_b).reshape(bsz * seq, d)
    h_p = _pw2_mlp_call(c_p, x_p, pw2, pw2b, g_mlp0, w1[0], w2[0])
    k_p, v_p, q_p, kb_p, va_p = _qkv_call(h_p, g_kv, g_mix1, wk, wv, wq)
    assert min(ATTN_BLOCK, seq) % MAX_DISTANCE == 0
    r = jnp.arange(MAX_DISTANCE, dtype=jnp.int32)[:, None]
    c = jnp.arange(MAX_DISTANCE, dtype=jnp.int32)[None, :]
    bias_p = jnp.stack([_shifted_bias(rel, r - c, r >= c),
                        _shifted_bias(rel, MAX_DISTANCE + r - c, jnp.full((MAX_DISTANCE,) * 2, True))], axis=1)
    a_p = _attn_call(q_p.reshape(bsz, seq, d), kb_p.reshape(bsz, seq, d),
                     va_p.reshape(bsz, seq, 2 * d), bias_p, lamv, subg)
    y_prompt = back(a_p.reshape(bsz * seq, d), h_p).reshape(bsz, seq, d)
    conv_state_prompt = g_p.reshape(bsz, seq, d)[:, seq - hist:][None]

    st_s = jnp.pad(state_conv[0].astype(F32), ((0, 0), (HIST_ROWS - hist, 0), (0, 0)))
    g_s, h_s, k_s, v_s, q_s = _short_front_call(
        x_sample.reshape(db * dseq, d), st_s, g_mix0, pw1, pw1b, dw, dwb, ln_g, ln_b,
        pw2, pw2b, g_mlp0, w1[0], w2[0], g_kv, g_mix1, wk, wv, wq)
    prow = page * N_HEADS
    n_cols = 2 * N_HEADS * dseq
    assert n_cols == LANES and dseq <= page
    col_q = jnp.tile(jnp.arange(dseq, dtype=jnp.int32), 2 * N_HEADS)[None, :]
    key = jnp.arange(page, dtype=jnp.int32)[:, None]
    pick = lambda b4: jnp.concatenate([b4[h, :, 2 * h * dseq:2 * (h + 1) * dseq] for h in range(N_HEADS)], axis=1)
    blast = pick(_shifted_bias(rel, page + col_q - key, jnp.full((page, n_cols), True)))
    bnew = pick(_shifted_bias(rel, col_q - key, (key <= col_q) & (key < dseq)))
    a_s = _decode_call(page_table.astype(jnp.int32), q_s.reshape(db, dseq, d),
                       cache_k.reshape(n_pool, prow, V_HEAD_DIM), cache_v.reshape(n_pool, prow, V_HEAD_DIM),
                       k_s.reshape(db, dseq, d), v_s.reshape(db, dseq, d), blast, bnew, lamv, subg)
    y_sample = back(a_s.reshape(db * dseq, d), h_s).reshape(db, dseq, d)
    conv_state_sample = jnp.concatenate([state_conv[0].astype(F32), g_s.reshape(db, dseq, d)],
                                        axis=1)[:, dseq:][None]

    kv_shape = (N_HEADS, V_HEAD_DIM)
    return (y_prompt, y_sample, conv_state_prompt, conv_state_sample,
            k_p.reshape(bsz, seq, *kv_shape), v_p.reshape(bsz, seq, *kv_shape),
            k_s.reshape(db, dseq, *kv_shape), v_s.reshape(db, dseq, *kv_shape))
```

```python
import functools
import math

import jax
import jax.numpy as jnp
from jax import lax
from jax.experimental import pallas as pl
from jax.experimental.pallas import tpu as pltpu

F32 = jnp.float32
BF16 = jnp.bfloat16

N_HEADS = 8
HEAD_DIM = 64
V_HEAD_DIM = 2 * HEAD_DIM
CONV_WIDTH = 31
N_BUCKETS = 32
MAX_DISTANCE = 128
NORM_EPS = 1e-6
NEG_INF = -1e30
LAM_INIT = 0.8 - 0.6 * math.exp(-0.3 * 1)
LOG2E = math.log2(math.e)
Q_SCALE = HEAD_DIM ** -0.5 * LOG2E

SUBLANES = 8
LANES = 128
HIST_ROWS = 32
VMEM_LIMIT = 56 * 1024 * 1024

TOKEN_TILE = 512
WIDE_TILE = 1024
FF_CHUNK = 512
CONV_ROWS = 32
ATTN_BLOCK = 512
ATTN_HEADS = 2
PAGES_PER_STEP = 16
DECODE_GROUP = 4


def _params(n_axes):
    return pltpu.CompilerParams(dimension_semantics=("arbitrary",) * n_axes,
                                vmem_limit_bytes=VMEM_LIMIT)


def _resident(shape):
    nd = len(shape)
    return pl.BlockSpec(shape, lambda *_: (0,) * nd, pipeline_mode=pl.Buffered(1))


class _Layer:
    def __init__(self, stack, index):
        self.stack, self.index, self.shape = stack, index, stack.shape[1:]

    def spec(self):
        idx = (self.index,) + (0,) * len(self.shape)
        return pl.BlockSpec((None,) + self.shape, lambda *_: idx, pipeline_mode=pl.Buffered(1))


def _spec_of(a):
    return a.spec() if isinstance(a, _Layer) else _resident(a.shape)


def _array_of(a):
    return a.stack if isinstance(a, _Layer) else a


def _rms_unit(x):
    return x * lax.rsqrt(jnp.mean(x * x, axis=-1, keepdims=True) + NORM_EPS)


def _mlp(h, g_ref, w1_ref, w2_ref):
    hn = (_rms_unit(h) * g_ref[...]).astype(BF16)
    d_ff = w1_ref.shape[1]
    acc = h
    for c in range(d_ff // FF_CHUNK):
        u = jnp.dot(hn, w1_ref[:, c * FF_CHUNK:(c + 1) * FF_CHUNK], preferred_element_type=F32)
        u = jnp.maximum(u, 0.0)
        acc = acc + jnp.dot((u * u).astype(BF16), w2_ref[c * FF_CHUNK:(c + 1) * FF_CHUNK, :],
                            preferred_element_type=F32)
    return acc


def _glu(x, g_ref, w_ref, b_ref):
    d = x.shape[1]
    hn = (_rms_unit(x) * g_ref[...]).astype(BF16)
    u = jnp.dot(hn, w_ref[...], preferred_element_type=F32) + b_ref[...]
    return u[:, :d] * jax.nn.sigmoid(u[:, d:])


def _glu_kernel(x_ref, g_ref, w_ref, b_ref, o_ref):
    o_ref[...] = _glu(x_ref[...], g_ref, w_ref, b_ref)


def _glu_call(x2, g, w, b):
    n, d = x2.shape
    tm = min(WIDE_TILE, n)
    return pl.pallas_call(
        _glu_kernel,
        grid=(n // tm,),
        in_specs=[pl.BlockSpec((tm, d), lambda i: (i, 0)),
                  _resident((1, d)), _resident((d, 2 * d)), _resident((1, 2 * d))],
        out_specs=pl.BlockSpec((tm, d), lambda i: (i, 0)),
        out_shape=jax.ShapeDtypeStruct((n, d), F32),
        compiler_params=_params(1),
        name="glu",
    )(x2, g, w, b)


def _ln_swish(c, ln_g_ref, ln_b_ref):
    mu = jnp.mean(c, axis=-1, keepdims=True)
    cc = c - mu
    var = jnp.mean(cc * cc, axis=-1, keepdims=True)
    y = cc * lax.rsqrt(var + NORM_EPS) * ln_g_ref[...] + ln_b_ref[...]
    return y * jax.nn.sigmoid(y)


def _conv_long_kernel(g_ref, hist_ref, dw_ref, dwb_ref, ln_g_ref, ln_b_ref, o_ref, sh_ref):
    tl = g_ref.shape[1]
    first = pl.program_id(1) == 0
    hist = jnp.where(first, 0.0, hist_ref[0])
    sh_ref[0, 0:HIST_ROWS, :] = hist
    sh_ref[0, HIST_ROWS:HIST_ROWS + tl, :] = g_ref[0]
    n_sh = tl + HIST_ROWS - SUBLANES
    for b in range(1, SUBLANES):
        sh_ref[b, 0:n_sh, :] = sh_ref[0, b:b + n_sh, :]

    def chunk(ci, carry):
        r0 = pl.multiple_of(ci * CONV_ROWS, CONV_ROWS)
        groups = range(0, CONV_ROWS, SUBLANES)
        acc = [jnp.broadcast_to(dwb_ref[...], (SUBLANES, dwb_ref.shape[1])) for _ in groups]
        for k in range(CONV_WIDTH):
            off = HIST_ROWS - (CONV_WIDTH - 1) + k
            a, b = divmod(off, SUBLANES)
            w_k = dw_ref[k]
            acc = [ac + w_k * sh_ref[b, pl.ds(r0 + a * SUBLANES + g, SUBLANES), :] for ac, g in zip(acc, groups)]
        c = jnp.concatenate(acc, axis=0)
        o_ref[0, pl.ds(r0, CONV_ROWS), :] = _ln_swish(c, ln_g_ref, ln_b_ref).astype(o_ref.dtype)
        return carry

    lax.fori_loop(0, tl // CONV_ROWS, chunk, 0, unroll=4)


def _conv_long_call(g3, dw, dwb, ln_g, ln_b):
    bsz, t, d = g3.shape
    tl = min(TOKEN_TILE, t)
    per = tl // HIST_ROWS
    return pl.pallas_call(
        _conv_long_kernel,
        grid=(bsz, t // tl),
        in_specs=[pl.BlockSpec((1, tl, d), lambda b, j: (b, j, 0)),
                  pl.BlockSpec((1, HIST_ROWS, d), lambda b, j: (b, jnp.maximum(j * per - 1, 0), 0)),
                  _resident((HIST_ROWS, SUBLANES, d)), _resident((1, d)), _resident((1, d)), _resident((1, d))],
        out_specs=pl.BlockSpec((1, tl, d), lambda b, j: (b, j, 0)),
        out_shape=jax.ShapeDtypeStruct((bsz, t, d), BF16),
        scratch_shapes=[pltpu.VMEM((SUBLANES, tl + HIST_ROWS, d), F32)],
        compiler_params=_params(2),
        name="conv_long",
    )(g3, g3, dw, dwb, ln_g, ln_b)


def _short_front_kernel(x_ref, st_ref, g0_ref, pw1_ref, pw1b_ref, dw_ref, dwb_ref, ln_g_ref, ln_b_ref,
                        pw2_ref, pw2b_ref, gm_ref, w1_ref, w2_ref, gkv_ref, gq_ref, wk_ref, wv_ref, wq_ref,
                        g_ref, h_ref, k_ref, v_ref, q_ref, ext_ref):
    n, d = x_ref.shape
    s = st_ref.shape[0]
    t = n // s
    x = x_ref[...]
    g = _glu(x, g0_ref, pw1_ref, pw1b_ref)
    g_ref[...] = g
    ext_ref[:, 0:HIST_ROWS, :] = st_ref[...]
    ext_ref[:, HIST_ROWS:HIST_ROWS + t, :] = g.reshape(s, t, d)
    acc = jnp.broadcast_to(dwb_ref[...].reshape(1, 1, d), (s, t, d))
    for k in range(CONV_WIDTH):
        off = HIST_ROWS - (CONV_WIDTH - 1) + k
        acc = acc + dw_ref[k:k + 1, :].reshape(1, 1, d) * ext_ref[:, off:off + t, :]
    c = _ln_swish(acc.reshape(n, d), ln_g_ref, ln_b_ref).astype(BF16)
    h = x + jnp.dot(c, pw2_ref[...], preferred_element_type=F32) + pw2b_ref[...]
    h = _mlp(h, gm_ref, w1_ref, w2_ref)
    h_ref[...] = h
    k_ref[...], v_ref[...], q_ref[...] = _project_qkv(h, gkv_ref, gq_ref, wk_ref, wv_ref, wq_ref)


def _short_front_call(x2, st3, *params):
    n, d = x2.shape
    s = st3.shape[0]
    out = lambda dt: jax.ShapeDtypeStruct((n, d), dt)
    return pl.pallas_call(
        _short_front_kernel,
        grid=(1,),
        in_specs=[_spec_of(a) for a in (x2, st3) + params],
        out_specs=[pl.BlockSpec((n, d), lambda i: (0, 0))] * 5,
        out_shape=[out(F32), out(F32), out(F32), out(F32), out(BF16)],
        scratch_shapes=[pltpu.VMEM((s, HIST_ROWS + n // s, d), F32)],
        compiler_params=_params(1),
        name="short_front",
    )(x2, st3, *[_array_of(a) for a in params])


def _pw2_mlp_kernel(c_ref, x_ref, pw2_ref, pw2b_ref, gm_ref, w1_ref, w2_ref, o_ref):
    h = x_ref[...] + jnp.dot(c_ref[...], pw2_ref[...], preferred_element_type=F32) + pw2b_ref[...]
    o_ref[...] = _mlp(h, gm_ref, w1_ref, w2_ref)


def _pw2_mlp_call(c2, x2, pw2, pw2b, gm, w1, w2):
    n, d = x2.shape
    tm = min(WIDE_TILE, n)
    row = lambda i: (i, 0)
    return pl.pallas_call(
        _pw2_mlp_kernel,
        grid=(n // tm,),
        in_specs=[pl.BlockSpec((tm, d), row), pl.BlockSpec((tm, d), row),
                  _resident((d, d)), _resident((1, d)), _resident((1, d)),
                  _spec_of(w1), _spec_of(w2)],
        out_specs=pl.BlockSpec((tm, d), row),
        out_shape=jax.ShapeDtypeStruct((n, d), F32),
        compiler_params=_params(1),
        name="pw2_mlp",
    )(c2, x2, pw2, pw2b, gm, _array_of(w1), _array_of(w2))


def _project_qkv(h, gkv_ref, gq_ref, wk_ref, wv_ref, wq_ref):
    y = _rms_unit(h)
    kvn = (y * gkv_ref[...]).astype(BF16)
    hn = (y * gq_ref[...]).astype(BF16)
    k = jnp.dot(kvn, wk_ref[...], preferred_element_type=F32)
    v = jnp.dot(kvn, wv_ref[...], preferred_element_type=F32)
    q = jnp.dot(hn, wq_ref[...], preferred_element_type=F32)
    return k, v, (q * Q_SCALE).astype(BF16)


def _qkv_kernel(h_ref, gkv_ref, gq_ref, wk_ref, wv_ref, wq_ref,
                k_ref, v_ref, q_ref, kb_ref, va_ref):
    tm, d = h_ref.shape
    k, v, q = _project_qkv(h_ref[...], gkv_ref, gq_ref, wk_ref, wv_ref, wq_ref)
    k_ref[...] = k
    v_ref[...] = v
    q_ref[...] = q
    kb_ref[...] = k.astype(BF16)
    ones_col = (lax.broadcasted_iota(jnp.int32, (tm, V_HEAD_DIM), 1) == 0).astype(BF16)
    vb = v.astype(BF16)
    for h in range(N_HEADS):
        va_ref[:, 2 * h * V_HEAD_DIM:(2 * h + 1) * V_HEAD_DIM] = vb[:, h * V_HEAD_DIM:(h + 1) * V_HEAD_DIM]
        va_ref[:, (2 * h + 1) * V_HEAD_DIM:(2 * h + 2) * V_HEAD_DIM] = ones_col


def _qkv_call(h2, gkv, gq, wk, wv, wq):
    n, d = h2.shape
    tm = min(TOKEN_TILE, n)
    row = lambda i: (i, 0)
    return pl.pallas_call(
        _qkv_kernel,
        grid=(n // tm,),
        in_specs=[pl.BlockSpec((tm, d), row), _resident((1, d)), _resident((1, d)),
                  _resident((d, d)), _resident((d, d)), _resident((d, d))],
        out_specs=[pl.BlockSpec((tm, d), row), pl.BlockSpec((tm, d), row), pl.BlockSpec((tm, d), row),
                   pl.BlockSpec((tm, d), row),
                   pl.BlockSpec((tm, 2 * d), row)],
        out_shape=[jax.ShapeDtypeStruct((n, d), F32), jax.ShapeDtypeStruct((n, d), F32),
                   jax.ShapeDtypeStruct((n, d), BF16), jax.ShapeDtypeStruct((n, d), BF16),
                   jax.ShapeDtypeStruct((n, 2 * d), BF16)],
        compiler_params=_params(1),
        name="qkv",
    )(h2, gkv, gq, wk, wv, wq)


def _lambda(lam_ref):
    lv = lam_ref[...]
    s1 = jnp.sum(lv[0:1, :] * lv[1:2, :], axis=-1, keepdims=True)
    s2 = jnp.sum(lv[2:3, :] * lv[3:4, :], axis=-1, keepdims=True)
    return jnp.exp(s1) - jnp.exp(s2) + LAM_INIT


def _sub_norm(o, subg_ref):
    return (_rms_unit(o) * subg_ref[...]) * (1.0 - LAM_INIT)


def _rel_bucket(n):
    max_exact = N_BUCKETS // 2
    nf = jnp.maximum(n, 1).astype(F32)
    large = max_exact + (jnp.log(nf / max_exact) / math.log(MAX_DISTANCE / max_exact)
                         * (N_BUCKETS - max_exact)).astype(jnp.int32)
    large = jnp.minimum(large, N_BUCKETS - 1)
    return jnp.where(n < max_exact, n, large)


def _shifted_bias(rel_table, dist, valid):
    bucket = _rel_bucket(jnp.maximum(dist, 0))[None]
    lead = (N_HEADS,) + (1,) * dist.ndim
    vals = jnp.zeros((N_HEADS,) + dist.shape, F32)
    for n in range(N_BUCKETS - 1):
        vals = jnp.where(bucket == n, (rel_table[n] - rel_table[N_BUCKETS - 1]).reshape(lead), vals)
    return jnp.where(valid[None], vals * LOG2E, NEG_INF)


def _attn_kernel(q_ref, k_ref, va_ref, bias_ref, lam_ref, subg_ref, o_ref, qz_ref, acc_ref, m_ref):
    blk = q_ref.shape[1]
    n_hd = q_ref.shape[2] // V_HEAD_DIM
    i = pl.program_id(2)
    for hd in range(n_hd):
        q = q_ref[0, :, hd * V_HEAD_DIM:(hd + 1) * V_HEAD_DIM]
        first_half = lax.broadcasted_iota(jnp.int32, q.shape, 1) < HEAD_DIM
        zero = jnp.zeros_like(q)
        qz_ref[2 * hd * blk:(2 * hd + 1) * blk, :] = jnp.where(first_half, q, zero)
        qz_ref[(2 * hd + 1) * blk:(2 * hd + 2) * blk, :] = jnp.where(first_half, zero, q)
    acc_ref[...] = jnp.zeros_like(acc_ref)
    m_ref[...] = jnp.full_like(m_ref, NEG_INF)
    reps = blk // LANES

    nt = (((1,), (1,)), ((), ()))
    chains = [(hd, slice((2 * hd + c) * blk, (2 * hd + c + 1) * blk)) for hd in range(n_hd) for c in range(2)]

    def scores(j):
        r0 = pl.multiple_of(j * blk, blk)
        return [lax.dot_general(qz_ref[rows, :], k_ref[0, pl.ds(r0, blk), hd * V_HEAD_DIM:(hd + 1) * V_HEAD_DIM],
                                nt, preferred_element_type=F32) for hd, rows in chains]

    sub = bias_ref.shape[2]
    nb = blk // sub

    def add_bias(sc, kind, hd):
        if kind == "prev":
            top = jnp.concatenate([sc[:sub, :blk - sub], sc[:sub, blk - sub:] + bias_ref[hd, 1]], axis=1)
            return jnp.concatenate([top, sc[sub:, :]], axis=0) if nb > 1 else top
        rows = []
        for r in range(nb):
            tiles = [bias_ref[hd, 0] if c == r else bias_ref[hd, 1] if c == r - 1 else
                     jnp.full((sub, sub), 0.0 if c < r else NEG_INF, F32) for c in range(nb)]
            rows.append(jnp.concatenate(tiles, axis=1))
        return sc + jnp.concatenate(rows, axis=0)

    def accumulate(j, s, bias):
        r0 = pl.multiple_of(j * blk, blk)
        if bias is not None:
            s = [add_bias(sc, bias, hd) for sc, (hd, _) in zip(s, chains)]
        m_prev = [m_ref[rows, :] for _, rows in chains]
        m_new = [jnp.maximum(mp, jnp.max(sc, axis=1, keepdims=True)) for mp, sc in zip(m_prev, s)]
        alpha = [jnp.exp2(mp - mn) for mp, mn in zip(m_prev, m_new)]
        p = [jnp.exp2(sc - jnp.concatenate([mn] * reps, axis=1)).astype(BF16) for sc, mn in zip(s, m_new)]
        pv = [jnp.dot(pc, va_ref[0, pl.ds(r0, blk), 2 * hd * V_HEAD_DIM:(2 * hd + 2) * V_HEAD_DIM],
                      preferred_element_type=F32) for pc, (hd, _) in zip(p, chains)]
        for (_, rows), al, pvc, mn in zip(chains, alpha, pv, m_new):
            acc_ref[rows, :] = jnp.concatenate([al, al], axis=1) * acc_ref[rows, :] + pvc
            m_ref[rows, :] = mn

    def process(blocks):
        s_all = [scores(j) for j, _ in blocks]
        for (j, bias), s in zip(blocks, s_all):
            accumulate(j, s, bias)

    n_far = jnp.maximum(i - 1, 0)

    def far_pair(t, carry):
        process([(2 * t, None), (2 * t + 1, None)])
        return carry

    lax.fori_loop(0, n_far // 2, far_pair, 0)
    odd = n_far % 2 == 1
    diag, prev = "diag", "prev"

    @pl.when(i == 0)
    def _():
        process([(i, diag)])

    @pl.when(jnp.logical_and(i > 0, jnp.logical_not(odd)))
    def _():
        process([(i - 1, prev), (i, diag)])

    @pl.when(odd)
    def _():
        process([(i - 2, None), (i - 1, prev), (i, diag)])

    lam = _lambda(lam_ref)
    for hd in range(n_hd):
        a1 = acc_ref[2 * hd * blk:(2 * hd + 1) * blk, :]
        a2 = acc_ref[(2 * hd + 1) * blk:(2 * hd + 2) * blk, :]
        o = (a1[:, :V_HEAD_DIM] * (1.0 / a1[:, V_HEAD_DIM:V_HEAD_DIM + 1])
             - a2[:, :V_HEAD_DIM] * (lam / a2[:, V_HEAD_DIM:V_HEAD_DIM + 1]))
        o_ref[0, :, hd * V_HEAD_DIM:(hd + 1) * V_HEAD_DIM] = _sub_norm(o, subg_ref).astype(o_ref.dtype)


def _attn_call(q3, k3, va3, bias, lamv, subg):
    bsz, t, d = q3.shape
    blk = min(ATTN_BLOCK, t)
    n_hd = ATTN_HEADS
    rows = 2 * n_hd * blk
    return pl.pallas_call(
        _attn_kernel,
        grid=(bsz, N_HEADS // n_hd, t // blk),
        in_specs=[pl.BlockSpec((1, blk, n_hd * V_HEAD_DIM), lambda b, h, i: (b, i, h)),
                  pl.BlockSpec((1, t, n_hd * V_HEAD_DIM), lambda b, h, i: (b, 0, h)),
                  pl.BlockSpec((1, t, 2 * n_hd * V_HEAD_DIM), lambda b, h, i: (b, 0, h)),
                  pl.BlockSpec((n_hd, 2, MAX_DISTANCE, MAX_DISTANCE), lambda b, h, i: (h, 0, 0, 0)),
                  _resident((SUBLANES, LANES)), _resident((1, V_HEAD_DIM))],
        out_specs=pl.BlockSpec((1, blk, n_hd * V_HEAD_DIM), lambda b, h, i: (b, i, h)),
        out_shape=jax.ShapeDtypeStruct((bsz, t, d), BF16),
        scratch_shapes=[pltpu.VMEM((rows, V_HEAD_DIM), BF16),
                        pltpu.VMEM((rows, 2 * V_HEAD_DIM), F32),
                        pltpu.VMEM((rows, LANES), F32)],
        compiler_params=_params(3),
        name="attn_prompt",
    )(q3, k3, va3, bias, lamv, subg)


def _decode_kernel(pt_ref, q_ref, *refs, n_pages_step):
    g_n = n_pages_step
    k_refs = refs[:g_n]
    v_refs = refs[g_n:2 * g_n]
    (knew_ref, vnew_ref, blast_ref, bnew_ref, lam_ref, subg_ref,
     o_ref, qbd_ref, acc_ref, m_ref, l_ref) = refs[2 * g_n:]
    st = pl.program_id(1)
    last = pl.num_programs(1) - 1
    tq = q_ref.shape[1]
    cols, d = qbd_ref.shape
    page = blast_ref.shape[0]

    @pl.when(st == 0)
    def _():
        qt = jnp.concatenate([q_ref[0]] * (cols // tq), axis=0)
        row = lax.broadcasted_iota(jnp.int32, (cols, d), 0)
        lane = lax.broadcasted_iota(jnp.int32, (cols, d), 1)
        qbd_ref[...] = jnp.where(row // tq == lane // HEAD_DIM, qt, jnp.zeros_like(qt))
        acc_ref[...] = jnp.zeros_like(acc_ref)
        m_ref[...] = jnp.full_like(m_ref, NEG_INF)
        l_ref[...] = jnp.zeros_like(l_ref)

    def slab(ref):
        return jnp.concatenate([ref[0, pl.ds(h, page, stride=N_HEADS), :] for h in range(N_HEADS)],
                               axis=1).astype(BF16)

    def per_row(v):
        return jnp.broadcast_to(v, (cols, cols)).T

    nt = (((1,), (1,)), ((), ()))
    tn = (((0,), (0,)), ((), ()))

    def scores(k):
        return lax.dot_general(k, qbd_ref[...], nt, preferred_element_type=F32)

    def update(s, v):
        m_prev = m_ref[...]
        m_new = jnp.maximum(m_prev, jnp.max(s, axis=0, keepdims=True))
        alpha = jnp.exp2(m_prev - m_new)
        p = jnp.exp2(s - m_new)
        l_ref[...] = alpha * l_ref[...] + jnp.sum(p, axis=0, keepdims=True)
        pv = lax.dot_general(p.astype(BF16), v, tn, preferred_element_type=F32)
        acc_ref[...] = jnp.concatenate([per_row(alpha)] * (d // cols), axis=1) * acc_ref[...] + pv
        m_ref[...] = m_new

    sub = math.gcd(DECODE_GROUP, g_n)
    groups = [slice(a, a + sub) for a in range(0, g_n, sub)]
    s_all = [scores(jnp.concatenate([slab(r) for r in k_refs[gs]], axis=0)) for gs in groups]
    is_last = (st == last).astype(F32)
    s_all[-1] = s_all[-1] + jnp.concatenate(
        [jnp.zeros(((sub - 1) * page, cols), F32), is_last * blast_ref[...]], axis=0)
    for gs, s in zip(groups, s_all):
        update(s, jnp.concatenate([slab(r) for r in v_refs[gs]], axis=0))

    def new_slab(ref):
        return jnp.concatenate([ref[0], jnp.zeros((page - tq, d), F32)], axis=0).astype(BF16)

    @pl.when(st == last)
    def _():
        update(scores(new_slab(knew_ref)) + bnew_ref[...], new_slab(vnew_ref))
        lam = _lambda(lam_ref)
        o_all = acc_ref[...] / jnp.concatenate([per_row(l_ref[...])] * (d // cols), axis=1)
        for h in range(N_HEADS):
            blk = o_all[2 * h * tq:(2 * h + 2) * tq, h * V_HEAD_DIM:(h + 1) * V_HEAD_DIM]
            o = blk[:tq] - lam * blk[tq:]
            o_ref[0, :, h * V_HEAD_DIM:(h + 1) * V_HEAD_DIM] = _sub_norm(o, subg_ref).astype(o_ref.dtype)


def _decode_call(page_table, q3, cache_k3, cache_v3, knew, vnew, blast, bnew, lamv, subg):
    db, n_pages = page_table.shape
    _, tq, d = q3.shape
    prow = cache_k3.shape[1]
    page = prow // N_HEADS
    g_n = math.gcd(PAGES_PER_STEP, n_pages)
    cols = N_HEADS * 2 * tq

    def page_spec(g):
        return pl.BlockSpec((1, prow, V_HEAD_DIM), lambda b, s, pt: (pt[b, s * g_n + g], 0, 0))

    per_b = lambda b, s, pt: (b, 0, 0)
    const2 = lambda b, s, pt: (0, 0)
    tile = pl.BlockSpec((page, cols), const2)
    grid_spec = pltpu.PrefetchScalarGridSpec(
        num_scalar_prefetch=1,
        grid=(db, n_pages // g_n),
        in_specs=([pl.BlockSpec((1, tq, d), per_b)]
                  + [page_spec(g) for g in range(g_n)] + [page_spec(g) for g in range(g_n)]
                  + [pl.BlockSpec((1, tq, d), per_b), pl.BlockSpec((1, tq, d), per_b),
                     tile, tile,
                     pl.BlockSpec((SUBLANES, LANES), const2), pl.BlockSpec((1, V_HEAD_DIM), const2)]),
        out_specs=pl.BlockSpec((1, tq, d), per_b),
        scratch_shapes=[pltpu.VMEM((cols, d), BF16), pltpu.VMEM((cols, d), F32),
                        pltpu.VMEM((1, cols), F32), pltpu.VMEM((1, cols), F32)],
    )
    return pl.pallas_call(
        functools.partial(_decode_kernel, n_pages_step=g_n),
        grid_spec=grid_spec,
        out_shape=jax.ShapeDtypeStruct((db, tq, d), BF16),
        compiler_params=_params(2),
        name="attn_decode",
    )(page_table, q3, *([cache_k3] * g_n), *([cache_v3] * g_n), knew, vnew, blast, bnew, lamv, subg)


def _post_kernel(a_ref, h_ref, wo_ref, gm_ref, w1_ref, w2_ref, gf_ref, o_ref):
    h = h_ref[...] + jnp.dot(a_ref[...], wo_ref[...], preferred_element_type=F32)
    h = _mlp(h, gm_ref, w1_ref, w2_ref)
    o_ref[...] = _rms_unit(h) * gf_ref[...]


def _post_call(a2, h2, wo, gm, w1, w2, gf):
    n, d = h2.shape
    tm = min(WIDE_TILE, n)
    row = lambda i: (i, 0)
    return pl.pallas_call(
        _post_kernel,
        grid=(n // tm,),
        in_specs=[pl.BlockSpec((tm, d), row), pl.BlockSpec((tm, d), row),
                  _resident((d, d)), _resident((1, d)), _spec_of(w1), _spec_of(w2),
                  _resident((1, d))],
        out_specs=pl.BlockSpec((tm, d), row),
        out_shape=jax.ShapeDtypeStruct((n, d), F32),
        compiler_params=_params(1),
        name="post",
    )(a2, h2, wo, gm, _array_of(w1), _array_of(w2), gf)


def _row(v):
    return v.reshape(1, -1).astype(F32)


def kernel(x_prompt, x_sample, state_conv, cache_k, cache_v, page_table, norm_mix_g, norm_mlp_g, conv_pw1_w, conv_pw1_b, conv_dw_w, conv_dw_b, conv_ln_g, conv_ln_b, conv_pw2_w, conv_pw2_b, kv_norm_g, w_k, w_v, w_q, lambda_q1, lambda_k1, lambda_q2, lambda_k2, subln_g, w_o, rel_bias_table, mlp_w1, mlp_w2, final_norm_g):
    bsz, seq, d = x_prompt.shape
    db, dseq, _ = x_sample.shape
    n_pool, page = cache_k.shape[0], cache_k.shape[1]
    n_pages = page_table.shape[1]
    past_len = n_pages * page
    hist = CONV_WIDTH - 1

    pw1 = conv_pw1_w[0].astype(BF16)
    pw1b = _row(conv_pw1_b[0])
    dw = jnp.pad(conv_dw_w[0].astype(F32), ((0, HIST_ROWS - CONV_WIDTH), (0, 0)))
    dwb, ln_g, ln_b = _row(conv_dw_b[0]), _row(conv_ln_g[0]), _row(conv_ln_b[0])
    pw2 = conv_pw2_w[0].astype(BF16)
    pw2b = _row(conv_pw2_b[0])
    w1s, w2s = mlp_w1.astype(BF16), mlp_w2.astype(BF16)
    w1 = [_Layer(w1s, l) for l in range(mlp_w1.shape[0])]
    w2 = [_Layer(w2s, l) for l in range(mlp_w2.shape[0])]
    wk, wv, wq, wo = w_k.astype(BF16), w_v.astype(BF16), w_q[0].astype(BF16), w_o[0].astype(BF16)
    lamv = jnp.pad(jnp.stack([lambda_q1[0], lambda_k1[0], lambda_q2[0], lambda_k2[0]]).astype(F32),
                   ((0, SUBLANES - 4), (0, LANES - HEAD_DIM)))
    subg = _row(subln_g[0])
    rel = rel_bias_table.astype(F32)

    g_mix0, g_mix1, g_mlp0, g_kv = _row(norm_mix_g[0]), _row(norm_mix_g[1]), _row(norm_mlp_g[0]), _row(kv_norm_g)

    def back(a2, h2):
        return _post_call(a2, h2, wo, _row(norm_mlp_g[1]), w1[1], w2[1], _row(final_norm_g))

    x_p = x_prompt.reshape(bsz * seq, d)
    g_p = _glu_call(x_p, g_mix0, pw1, pw1b)
    dw_rows = jnp.broadcast_to(dw[:, None, :], (HIST_ROWS, SUBLANES, d))
    c_p = _conv_long_call(g_p.reshape(bsz, seq, d), dw_rows, dwb, ln_g, ln_b).reshape(bsz * seq, d)
    h_p = _pw2_mlp_call(c_p, x_p, pw2, pw2b, g_mlp0, w1[0], w2[0])
    k_p, v_p, q_p, kb_p, va_p = _qkv_call(h_p, g_kv, g_mix1, wk, wv, wq)
    assert min(ATTN_BLOCK, seq) % MAX_DISTANCE == 0
    r = jnp.arange(MAX_DISTANCE, dtype=jnp.int32)[:, None]
    c = jnp.arange(MAX_DISTANCE, dtype=jnp.int32)[None, :]
    bias_p = jnp.stack([_shifted_bias(rel, r - c, r >= c),
                        _shifted_bias(rel, MAX_DISTANCE + r - c, jnp.full((MAX_DISTANCE,) * 2, True))], axis=1)
    a_p = _attn_call(q_p.reshape(bsz, seq, d), kb_p.reshape(bsz, seq, d),
                     va_p.reshape(bsz, seq, 2 * d), bias_p, lamv, subg)
    y_prompt = back(a_p.reshape(bsz * seq, d), h_p).reshape(bsz, seq, d)
    conv_state_prompt = g_p.reshape(bsz, seq, d)[:, seq - hist:][None]

    st_s = jnp.pad(state_conv[0].astype(F32), ((0, 0), (HIST_ROWS - hist, 0), (0, 0)))
    g_s, h_s, k_s, v_s, q_s = _short_front_call(
        x_sample.reshape(db * dseq, d), st_s, g_mix0, pw1, pw1b, dw, dwb, ln_g, ln_b,
        pw2, pw2b, g_mlp0, w1[0], w2[0], g_kv, g_mix1, wk, wv, wq)
    prow = page * N_HEADS
    n_cols = 2 * N_HEADS * dseq
    assert n_cols == LANES and dseq <= page
    col_q = jnp.tile(jnp.arange(dseq, dtype=jnp.int32), 2 * N_HEADS)[None, :]
    key = jnp.arange(page, dtype=jnp.int32)[:, None]
    pick = lambda b4: jnp.concatenate([b4[h, :, 2 * h * dseq:2 * (h + 1) * dseq] for h in range(N_HEADS)], axis=1)
    blast = pick(_shifted_bias(rel, page + col_q - key, jnp.full((page, n_cols), True)))
    bnew = pick(_shifted_bias(rel, col_q - key, (key <= col_q) & (key < dseq)))
    a_s = _decode_call(page_table.astype(jnp.int32), q_s.reshape(db, dseq, d),
                       cache_k.reshape(n_pool, prow, V_HEAD_DIM), cache_v.reshape(n_pool, prow, V_HEAD_DIM),
                       k_s.reshape(db, dseq, d), v_s.reshape(db, dseq, d), blast, bnew, lamv, subg)
    y_sample = back(a_s.reshape(db * dseq, d), h_s).reshape(db, dseq, d)
    conv_state_sample = jnp.concatenate([state_conv[0].astype(F32), g_s.reshape(db, dseq, d)],
                                        axis=1)[:, dseq:][None]

    kv_shape = (N_HEADS, V_HEAD_DIM)
    return (y_prompt, y_sample, conv_state_prompt, conv_state_sample,
            k_p.reshape(bsz, seq, *kv_shape), v_p.reshape(bsz, seq, *kv_shape),
            k_s.reshape(db, dseq, *kv_shape), v_s.reshape(db, dseq, *kv_shape))
```

```python
import functools
import math

import jax
import jax.numpy as jnp
from jax import lax
from jax.experimental import pallas as pl
from jax.experimental.pallas import tpu as pltpu

F32 = jnp.float32
BF16 = jnp.bfloat16

N_HEADS = 8
HEAD_DIM = 64
V_HEAD_DIM = 2 * HEAD_DIM
CONV_WIDTH = 31
N_BUCKETS = 32
MAX_DISTANCE = 128
NORM_EPS = 1e-6
NEG_INF = -1e30
LAM_INIT = 0.8 - 0.6 * math.exp(-0.3 * 1)
LOG2E = math.log2(math.e)
Q_SCALE = HEAD_DIM ** -0.5 * LOG2E

SUBLANES = 8
LANES = 128
HIST_ROWS = 32
VMEM_LIMIT = 56 * 1024 * 1024

TOKEN_TILE = 512
WIDE_TILE = 1024
FF_CHUNK = 512
CONV_ROWS = 32
ATTN_BLOCK = 512
ATTN_HEADS = 2
PAGES_PER_STEP = 16
DECODE_GROUP = 4


def _params(n_axes):
    return pltpu.CompilerParams(dimension_semantics=("arbitrary",) * n_axes,
                                vmem_limit_bytes=VMEM_LIMIT)


def _resident(shape):
    nd = len(shape)
    return pl.BlockSpec(shape, lambda *_: (0,) * nd, pipeline_mode=pl.Buffered(1))


class _Layer:
    def __init__(self, stack, index):
        self.stack, self.index, self.shape = stack, index, stack.shape[1:]

    def spec(self):
        idx = (self.index,) + (0,) * len(self.shape)
        return pl.BlockSpec((None,) + self.shape, lambda *_: idx, pipeline_mode=pl.Buffered(1))


def _spec_of(a):
    return a.spec() if isinstance(a, _Layer) else _resident(a.shape)


def _array_of(a):
    return a.stack if isinstance(a, _Layer) else a


def _rms_unit(x):
    return x * lax.rsqrt(jnp.mean(x * x, axis=-1, keepdims=True) + NORM_EPS)


def _mlp(h, g_ref, w1_ref, w2_ref):
    hn = (_rms_unit(h) * g_ref[...]).astype(BF16)
    d_ff = w1_ref.shape[1]
    acc = h
    for c in range(d_ff // FF_CHUNK):
        u = jnp.dot(hn, w1_ref[:, c * FF_CHUNK:(c + 1) * FF_CHUNK], preferred_element_type=F32)
        u = jnp.maximum(u, 0.0)
        acc = acc + jnp.dot((u * u).astype(BF16), w2_ref[c * FF_CHUNK:(c + 1) * FF_CHUNK, :],
                            preferred_element_type=F32)
    return acc


def _glu(x, g_ref, w_ref, b_ref):
    d = x.shape[1]
    hn = (_rms_unit(x) * g_ref[...]).astype(BF16)
    u = jnp.dot(hn, w_ref[...], preferred_element_type=F32) + b_ref[...]
    return u[:, :d] * jax.nn.sigmoid(u[:, d:])


def _glu_kernel(x_ref, g_ref, w_ref, b_ref, o_ref):
    o_ref[...] = _glu(x_ref[...], g_ref, w_ref, b_ref)


def _glu_call(x2, g, w, b):
    n, d = x2.shape
    tm = min(WIDE_TILE, n)
    return pl.pallas_call(
        _glu_kernel,
        grid=(n // tm,),
        in_specs=[pl.BlockSpec((tm, d), lambda i: (i, 0)),
                  _resident((1, d)), _resident((d, 2 * d)), _resident((1, 2 * d))],
        out_specs=pl.BlockSpec((tm, d), lambda i: (i, 0)),
        out_shape=jax.ShapeDtypeStruct((n, d), F32),
        compiler_params=_params(1),
        name="glu",
    )(x2, g, w, b)


def _ln_swish(c, ln_g_ref, ln_b_ref):
    mu = jnp.mean(c, axis=-1, keepdims=True)
    cc = c - mu
    var = jnp.mean(cc * cc, axis=-1, keepdims=True)
    y = cc * lax.rsqrt(var + NORM_EPS) * ln_g_ref[...] + ln_b_ref[...]
    return y * jax.nn.sigmoid(y)


def _conv_long_kernel(g_ref, hist_ref, dw_ref, dwb_ref, ln_g_ref, ln_b_ref, o_ref, sh_ref):
    tl = g_ref.shape[1]
    first = pl.program_id(1) == 0
    hist = jnp.where(first, 0.0, hist_ref[0])
    sh_ref[0, 0:HIST_ROWS, :] = hist
    sh_ref[0, HIST_ROWS:HIST_ROWS + tl, :] = g_ref[0]
    n_sh = tl + HIST_ROWS - SUBLANES
    for b in range(1, SUBLANES):
        sh_ref[b, 0:n_sh, :] = sh_ref[0, b:b + n_sh, :]

    def chunk(ci, carry):
        r0 = pl.multiple_of(ci * CONV_ROWS, CONV_ROWS)
        groups = range(0, CONV_ROWS, SUBLANES)
        acc = [jnp.broadcast_to(dwb_ref[...], (SUBLANES, dwb_ref.shape[1])) for _ in groups]
        for k in range(CONV_WIDTH):
            off = HIST_ROWS - (CONV_WIDTH - 1) + k
            a, b = divmod(off, SUBLANES)
            w_k = dw_ref[k]
            acc = [ac + w_k * sh_ref[b, pl.ds(r0 + a * SUBLANES + g, SUBLANES), :] for ac, g in zip(acc, groups)]
        c = jnp.concatenate(acc, axis=0)
        o_ref[0, pl.ds(r0, CONV_ROWS), :] = _ln_swish(c, ln_g_ref, ln_b_ref).astype(o_ref.dtype)
        return carry

    lax.fori_loop(0, tl // CONV_ROWS, chunk, 0, unroll=4)


def _conv_long_call(g3, dw, dwb, ln_g, ln_b):
    bsz, t, d = g3.shape
    tl = min(TOKEN_TILE, t)
    per = tl // HIST_ROWS
    return pl.pallas_call(
        _conv_long_kernel,
        grid=(bsz, t // tl),
        in_specs=[pl.BlockSpec((1, tl, d), lambda b, j: (b, j, 0)),
                  pl.BlockSpec((1, HIST_ROWS, d), lambda b, j: (b, jnp.maximum(j * per - 1, 0), 0)),
                  _resident((HIST_ROWS, SUBLANES, d)), _resident((1, d)), _resident((1, d)), _resident((1, d))],
        out_specs=pl.BlockSpec((1, tl, d), lambda b, j: (b, j, 0)),
        out_shape=jax.ShapeDtypeStruct((bsz, t, d), BF16),
        scratch_shapes=[pltpu.VMEM((SUBLANES, tl + HIST_ROWS, d), F32)],
        compiler_params=_params(2),
        name="conv_long",
    )(g3, g3, dw, dwb, ln_g, ln_b)


def _short_front_kernel(x_ref, st_ref, g0_ref, pw1_ref, pw1b_ref, dw_ref, dwb_ref, ln_g_ref, ln_b_ref,
                        pw2_ref, pw2b_ref, gm_ref, w1_ref, w2_ref, gkv_ref, gq_ref, wk_ref, wv_ref, wq_ref,
                        cs_ref, h_ref, k_ref, v_ref, q_ref, ext_ref):
    n, d = x_ref.shape
    s = st_ref.shape[0]
    t = n // s
    x = x_ref[...]
    g = _glu(x, g0_ref, pw1_ref, pw1b_ref)
    ext_ref[:, 0:HIST_ROWS, :] = st_ref[...]
    ext_ref[:, HIST_ROWS:HIST_ROWS + t, :] = g.reshape(s, t, d)
    n_hist = cs_ref.shape[1]
    cs_ref[...] = ext_ref[:, HIST_ROWS + t - n_hist:HIST_ROWS + t, :]
    acc = jnp.broadcast_to(dwb_ref[...].reshape(1, 1, d), (s, t, d))
    for k in range(CONV_WIDTH):
        off = HIST_ROWS - (CONV_WIDTH - 1) + k
        acc = acc + dw_ref[k:k + 1, :].reshape(1, 1, d) * ext_ref[:, off:off + t, :]
    c = _ln_swish(acc.reshape(n, d), ln_g_ref, ln_b_ref).astype(BF16)
    h = x + jnp.dot(c, pw2_ref[...], preferred_element_type=F32) + pw2b_ref[...]
    hn = (_rms_unit(h) * gm_ref[...]).astype(BF16)
    h_ref[...] = h

    def ff_chunk(ci, carry):
        c0 = pl.multiple_of(ci * FF_CHUNK, FF_CHUNK)
        u = jnp.maximum(jnp.dot(hn, w1_ref[:, pl.ds(c0, FF_CHUNK)], preferred_element_type=F32), 0.0)
        h_ref[...] += jnp.dot((u * u).astype(BF16), w2_ref[pl.ds(c0, FF_CHUNK), :], preferred_element_type=F32)
        return carry

    lax.fori_loop(0, w1_ref.shape[1] // FF_CHUNK, ff_chunk, 0)
    k_ref[...], v_ref[...], q_ref[...] = _project_qkv(h_ref[...], gkv_ref, gq_ref, wk_ref, wv_ref, wq_ref)


def _short_front_call(x2, st3, *params):
    n, d = x2.shape
    s = st3.shape[0]
    out = lambda dt: jax.ShapeDtypeStruct((n, d), dt)
    return pl.pallas_call(
        _short_front_kernel,
        grid=(1,),
        in_specs=[_spec_of(a) for a in (x2, st3) + params],
        out_specs=[pl.BlockSpec((s, CONV_WIDTH - 1, d), lambda i: (0, 0, 0))]
                  + [pl.BlockSpec((n, d), lambda i: (0, 0))] * 4,
        out_shape=[jax.ShapeDtypeStruct((s, CONV_WIDTH - 1, d), F32), out(F32), out(F32), out(F32), out(BF16)],
        scratch_shapes=[pltpu.VMEM((s, HIST_ROWS + n // s, d), F32)],
        compiler_params=_params(1),
        name="short_front",
    )(x2, st3, *[_array_of(a) for a in params])


def _pw2_mlp_kernel(c_ref, x_ref, pw2_ref, pw2b_ref, gm_ref, w1_ref, w2_ref, o_ref):
    h = x_ref[...] + jnp.dot(c_ref[...], pw2_ref[...], preferred_element_type=F32) + pw2b_ref[...]
    o_ref[...] = _mlp(h, gm_ref, w1_ref, w2_ref)


def _pw2_mlp_call(c2, x2, pw2, pw2b, gm, w1, w2):
    n, d = x2.shape
    tm = min(WIDE_TILE, n)
    row = lambda i: (i, 0)
    return pl.pallas_call(
        _pw2_mlp_kernel,
        grid=(n // tm,),
        in_specs=[pl.BlockSpec((tm, d), row), pl.BlockSpec((tm, d), row),
                  _resident((d, d)), _resident((1, d)), _resident((1, d)),
                  _spec_of(w1), _spec_of(w2)],
        out_specs=pl.BlockSpec((tm, d), row),
        out_shape=jax.ShapeDtypeStruct((n, d), F32),
        compiler_params=_params(1),
        name="pw2_mlp",
    )(c2, x2, pw2, pw2b, gm, _array_of(w1), _array_of(w2))


def _project_qkv(h, gkv_ref, gq_ref, wk_ref, wv_ref, wq_ref):
    y = _rms_unit(h)
    kvn = (y * gkv_ref[...]).astype(BF16)
    hn = (y * gq_ref[...]).astype(BF16)
    k = jnp.dot(kvn, wk_ref[...], preferred_element_type=F32)
    v = jnp.dot(kvn, wv_ref[...], preferred_element_type=F32)
    q = jnp.dot(hn, wq_ref[...], preferred_element_type=F32)
    return k, v, (q * Q_SCALE).astype(BF16)


def _qkv_kernel(h_ref, gkv_ref, gq_ref, wk_ref, wv_ref, wq_ref,
                k_ref, v_ref, q_ref, kb_ref, va_ref):
    tm, d = h_ref.shape
    k, v, q = _project_qkv(h_ref[...], gkv_ref, gq_ref, wk_ref, wv_ref, wq_ref)
    k_ref[...] = k
    v_ref[...] = v
    q_ref[...] = q
    kb_ref[...] = k.astype(BF16)
    ones_col = jnp.ones((tm, V_HEAD_DIM), BF16)
    vb = v.astype(BF16)
    for h in range(N_HEADS):
        va_ref[:, 2 * h * V_HEAD_DIM:(2 * h + 1) * V_HEAD_DIM] = vb[:, h * V_HEAD_DIM:(h + 1) * V_HEAD_DIM]
        va_ref[:, (2 * h + 1) * V_HEAD_DIM:(2 * h + 2) * V_HEAD_DIM] = ones_col


def _qkv_call(h2, gkv, gq, wk, wv, wq):
    n, d = h2.shape
    tm = min(TOKEN_TILE, n)
    row = lambda i: (i, 0)
    return pl.pallas_call(
        _qkv_kernel,
        grid=(n // tm,),
        in_specs=[pl.BlockSpec((tm, d), row), _resident((1, d)), _resident((1, d)),
                  _resident((d, d)), _resident((d, d)), _resident((d, d))],
        out_specs=[pl.BlockSpec((tm, d), row), pl.BlockSpec((tm, d), row), pl.BlockSpec((tm, d), row),
                   pl.BlockSpec((tm, d), row),
                   pl.BlockSpec((tm, 2 * d), row)],
        out_shape=[jax.ShapeDtypeStruct((n, d), F32), jax.ShapeDtypeStruct((n, d), F32),
                   jax.ShapeDtypeStruct((n, d), BF16), jax.ShapeDtypeStruct((n, d), BF16),
                   jax.ShapeDtypeStruct((n, 2 * d), BF16)],
        compiler_params=_params(1),
        name="qkv",
    )(h2, gkv, gq, wk, wv, wq)


def _lambda(lam_ref):
    lv = lam_ref[...]
    s1 = jnp.sum(lv[0:1, :] * lv[1:2, :], axis=-1, keepdims=True)
    s2 = jnp.sum(lv[2:3, :] * lv[3:4, :], axis=-1, keepdims=True)
    return jnp.exp(s1) - jnp.exp(s2) + LAM_INIT


def _sub_norm(o, subg_ref):
    return (_rms_unit(o) * subg_ref[...]) * (1.0 - LAM_INIT)


def _rel_bucket(n):
    max_exact = N_BUCKETS // 2
    nf = jnp.maximum(n, 1).astype(F32)
    large = max_exact + (jnp.log(nf / max_exact) / math.log(MAX_DISTANCE / max_exact)
                         * (N_BUCKETS - max_exact)).astype(jnp.int32)
    large = jnp.minimum(large, N_BUCKETS - 1)
    return jnp.where(n < max_exact, n, large)


def _shifted_bias(rel_table, dist, valid):
    bucket = _rel_bucket(jnp.maximum(dist, 0))[None]
    lead = (N_HEADS,) + (1,) * dist.ndim
    vals = jnp.zeros((N_HEADS,) + dist.shape, F32)
    for n in range(N_BUCKETS - 1):
        vals = jnp.where(bucket == n, (rel_table[n] - rel_table[N_BUCKETS - 1]).reshape(lead), vals)
    return jnp.where(valid[None], vals * LOG2E, NEG_INF)


def _attn_kernel(q_ref, k_ref, va_ref, bias_ref, lam_ref, subg_ref, o_ref, qz_ref, acc_ref, m_ref):
    blk = q_ref.shape[1]
    n_hd = q_ref.shape[2] // V_HEAD_DIM
    i = pl.program_id(2)
    for hd in range(n_hd):
        q = q_ref[0, :, hd * V_HEAD_DIM:(hd + 1) * V_HEAD_DIM]
        first_half = lax.broadcasted_iota(jnp.int32, q.shape, 1) < HEAD_DIM
        zero = jnp.zeros_like(q)
        qz_ref[2 * hd * blk:(2 * hd + 1) * blk, :] = jnp.where(first_half, q, zero)
        qz_ref[(2 * hd + 1) * blk:(2 * hd + 2) * blk, :] = jnp.where(first_half, zero, q)
    acc_ref[...] = jnp.zeros_like(acc_ref)
    m_ref[...] = jnp.full_like(m_ref, NEG_INF)
    reps = blk // LANES

    nt = (((1,), (1,)), ((), ()))
    chains = [(hd, slice((2 * hd + c) * blk, (2 * hd + c + 1) * blk)) for hd in range(n_hd) for c in range(2)]

    def scores(j):
        r0 = pl.multiple_of(j * blk, blk)
        return [lax.dot_general(qz_ref[rows, :], k_ref[0, pl.ds(r0, blk), hd * V_HEAD_DIM:(hd + 1) * V_HEAD_DIM],
                                nt, preferred_element_type=F32) for hd, rows in chains]

    sub = bias_ref.shape[2]
    nb = blk // sub

    def add_bias(sc, kind, hd):
        if kind == "prev":
            top = jnp.concatenate([sc[:sub, :blk - sub], sc[:sub, blk - sub:] + bias_ref[hd, 1]], axis=1)
            return jnp.concatenate([top, sc[sub:, :]], axis=0) if nb > 1 else top
        rows = []
        for r in range(nb):
            tiles = [bias_ref[hd, 0] if c == r else bias_ref[hd, 1] if c == r - 1 else
                     jnp.full((sub, sub), 0.0 if c < r else NEG_INF, F32) for c in range(nb)]
            rows.append(jnp.concatenate(tiles, axis=1))
        return sc + jnp.concatenate(rows, axis=0)

    def accumulate(j, s, bias):
        r0 = pl.multiple_of(j * blk, blk)
        if bias is not None:
            s = [add_bias(sc, bias, hd) for sc, (hd, _) in zip(s, chains)]
        m_prev = [m_ref[rows, :] for _, rows in chains]
        m_new = [jnp.maximum(mp, jnp.max(sc, axis=1, keepdims=True)) for mp, sc in zip(m_prev, s)]
        alpha = [jnp.exp2(mp - mn) for mp, mn in zip(m_prev, m_new)]
        p = [jnp.exp2(sc - jnp.concatenate([mn] * reps, axis=1)).astype(BF16) for sc, mn in zip(s, m_new)]
        pv = [jnp.dot(pc, va_ref[0, pl.ds(r0, blk), 2 * hd * V_HEAD_DIM:(2 * hd + 2) * V_HEAD_DIM],
                      preferred_element_type=F32) for pc, (hd, _) in zip(p, chains)]
        for (_, rows), al, pvc, mn in zip(chains, alpha, pv, m_new):
            acc_ref[rows, :] = jnp.concatenate([al, al], axis=1) * acc_ref[rows, :] + pvc
            m_ref[rows, :] = mn

    def process(blocks):
        s_all = [scores(j) for j, _ in blocks]
        for (j, bias), s in zip(blocks, s_all):
            accumulate(j, s, bias)

    n_far = jnp.maximum(i - 1, 0)

    def far_pair(t, carry):
        process([(2 * t, None), (2 * t + 1, None)])
        return carry

    lax.fori_loop(0, n_far // 2, far_pair, 0)
    odd = n_far % 2 == 1
    diag, prev = "diag", "prev"

    @pl.when(i == 0)
    def _():
        process([(i, diag)])

    @pl.when(jnp.logical_and(i > 0, jnp.logical_not(odd)))
    def _():
        process([(i - 1, prev), (i, diag)])

    @pl.when(odd)
    def _():
        process([(i - 2, None), (i - 1, prev), (i, diag)])

    lam = _lambda(lam_ref)
    for hd in range(n_hd):
        a1 = acc_ref[2 * hd * blk:(2 * hd + 1) * blk, :]
        a2 = acc_ref[(2 * hd + 1) * blk:(2 * hd + 2) * blk, :]
        o = a1[:, :V_HEAD_DIM] * (1.0 / a1[:, V_HEAD_DIM:]) - a2[:, :V_HEAD_DIM] * (lam / a2[:, V_HEAD_DIM:])
        o_ref[0, :, hd * V_HEAD_DIM:(hd + 1) * V_HEAD_DIM] = _sub_norm(o, subg_ref).astype(o_ref.dtype)


def _attn_call(q3, k3, va3, bias, lamv, subg):
    bsz, t, d = q3.shape
    blk = min(ATTN_BLOCK, t)
    n_hd = ATTN_HEADS
    rows = 2 * n_hd * blk
    return pl.pallas_call(
        _attn_kernel,
        grid=(bsz, N_HEADS // n_hd, t // blk),
        in_specs=[pl.BlockSpec((1, blk, n_hd * V_HEAD_DIM), lambda b, h, i: (b, i, h)),
                  pl.BlockSpec((1, t, n_hd * V_HEAD_DIM), lambda b, h, i: (b, 0, h)),
                  pl.BlockSpec((1, t, 2 * n_hd * V_HEAD_DIM), lambda b, h, i: (b, 0, h)),
                  pl.BlockSpec((n_hd, 2, MAX_DISTANCE, MAX_DISTANCE), lambda b, h, i: (h, 0, 0, 0)),
                  _resident((SUBLANES, LANES)), _resident((1, V_HEAD_DIM))],
        out_specs=pl.BlockSpec((1, blk, n_hd * V_HEAD_DIM), lambda b, h, i: (b, i, h)),
        out_shape=jax.ShapeDtypeStruct((bsz, t, d), BF16),
        scratch_shapes=[pltpu.VMEM((rows, V_HEAD_DIM), BF16),
                        pltpu.VMEM((rows, 2 * V_HEAD_DIM), F32),
                        pltpu.VMEM((rows, LANES), F32)],
        compiler_params=_params(3),
        name="attn_prompt",
    )(q3, k3, va3, bias, lamv, subg)


def _decode_kernel(pt_ref, q_ref, *refs, n_pages_step):
    g_n = n_pages_step
    k_refs = refs[:g_n]
    v_refs = refs[g_n:2 * g_n]
    (knew_ref, vnew_ref, blast_ref, bnew_ref, lam_ref, subg_ref,
     o_ref, qbd_ref, acc_ref, m_ref, l_ref) = refs[2 * g_n:]
    st = pl.program_id(1)
    last = pl.num_programs(1) - 1
    tq = q_ref.shape[1]
    cols, d = qbd_ref.shape
    page = blast_ref.shape[0]

    @pl.when(st == 0)
    def _():
        qt = jnp.concatenate([q_ref[0]] * (cols // tq), axis=0)
        row = lax.broadcasted_iota(jnp.int32, (cols, d), 0)
        lane = lax.broadcasted_iota(jnp.int32, (cols, d), 1)
        qbd_ref[...] = jnp.where(row // tq == lane // HEAD_DIM, qt, jnp.zeros_like(qt))
        acc_ref[...] = jnp.zeros_like(acc_ref)
        m_ref[...] = jnp.full_like(m_ref, NEG_INF)
        l_ref[...] = jnp.zeros_like(l_ref)

    def slab(ref):
        return jnp.concatenate([ref[0, pl.ds(h, page, stride=N_HEADS), :] for h in range(N_HEADS)],
                               axis=1).astype(BF16)

    def per_row(v):
        return jnp.broadcast_to(v, (cols, cols)).T

    nt = (((1,), (1,)), ((), ()))
    tn = (((0,), (0,)), ((), ()))

    def scores(k):
        return lax.dot_general(k, qbd_ref[...], nt, preferred_element_type=F32)

    def update(s, v):
        m_prev = m_ref[...]
        m_new = jnp.maximum(m_prev, jnp.max(s, axis=0, keepdims=True))
        alpha = jnp.exp2(m_prev - m_new)
        p = jnp.exp2(s - m_new)
        l_ref[...] = alpha * l_ref[...] + jnp.sum(p, axis=0, keepdims=True)
        pv = lax.dot_general(p.astype(BF16), v, tn, preferred_element_type=F32)
        acc_ref[...] = jnp.concatenate([per_row(alpha)] * (d // cols), axis=1) * acc_ref[...] + pv
        m_ref[...] = m_new

    sub = math.gcd(DECODE_GROUP, g_n)
    groups = [slice(a, a + sub) for a in range(0, g_n, sub)]
    s_all = [scores(jnp.concatenate([slab(r) for r in k_refs[gs]], axis=0)) for gs in groups]
    is_last = (st == last).astype(F32)
    s_all[-1] = s_all[-1] + jnp.concatenate(
        [jnp.zeros(((sub - 1) * page, cols), F32), is_last * blast_ref[...]], axis=0)
    for gs, s in zip(groups, s_all):
        update(s, jnp.concatenate([slab(r) for r in v_refs[gs]], axis=0))

    def new_slab(ref):
        return jnp.concatenate([ref[0], jnp.zeros((page - tq, d), F32)], axis=0).astype(BF16)

    @pl.when(st == last)
    def _():
        update(scores(new_slab(knew_ref)) + bnew_ref[...], new_slab(vnew_ref))
        lam = _lambda(lam_ref)
        o_all = acc_ref[...] / jnp.concatenate([per_row(l_ref[...])] * (d // cols), axis=1)
        for h in range(N_HEADS):
            blk = o_all[2 * h * tq:(2 * h + 2) * tq, h * V_HEAD_DIM:(h + 1) * V_HEAD_DIM]
            o = blk[:tq] - lam * blk[tq:]
            o_ref[0, :, h * V_HEAD_DIM:(h + 1) * V_HEAD_DIM] = _sub_norm(o, subg_ref).astype(o_ref.dtype)


def _decode_call(page_table, q3, cache_k3, cache_v3, knew, vnew, blast, bnew, lamv, subg):
    db, n_pages = page_table.shape
    _, tq, d = q3.shape
    prow = cache_k3.shape[1]
    page = prow // N_HEADS
    g_n = math.gcd(PAGES_PER_STEP, n_pages)
    cols = N_HEADS * 2 * tq

    def page_spec(g):
        return pl.BlockSpec((1, prow, V_HEAD_DIM), lambda b, s, pt: (pt[b, s * g_n + g], 0, 0))

    per_b = lambda b, s, pt: (b, 0, 0)
    const2 = lambda b, s, pt: (0, 0)
    tile = pl.BlockSpec((page, cols), const2)
    grid_spec = pltpu.PrefetchScalarGridSpec(
        num_scalar_prefetch=1,
        grid=(db, n_pages // g_n),
        in_specs=([pl.BlockSpec((1, tq, d), per_b)]
                  + [page_spec(g) for g in range(g_n)] + [page_spec(g) for g in range(g_n)]
                  + [pl.BlockSpec((1, tq, d), per_b), pl.BlockSpec((1, tq, d), per_b),
                     tile, tile,
                     pl.BlockSpec((SUBLANES, LANES), const2), pl.BlockSpec((1, V_HEAD_DIM), const2)]),
        out_specs=pl.BlockSpec((1, tq, d), per_b),
        scratch_shapes=[pltpu.VMEM((cols, d), BF16), pltpu.VMEM((cols, d), F32),
                        pltpu.VMEM((1, cols), F32), pltpu.VMEM((1, cols), F32)],
    )
    return pl.pallas_call(
        functools.partial(_decode_kernel, n_pages_step=g_n),
        grid_spec=grid_spec,
        out_shape=jax.ShapeDtypeStruct((db, tq, d), BF16),
        compiler_params=_params(2),
        name="attn_decode",
    )(page_table, q3, *([cache_k3] * g_n), *([cache_v3] * g_n), knew, vnew, blast, bnew, lamv, subg)


def _post_kernel(a_ref, h_ref, wo_ref, gm_ref, w1_ref, w2_ref, gf_ref, o_ref):
    h = h_ref[...] + jnp.dot(a_ref[...], wo_ref[...], preferred_element_type=F32)
    h = _mlp(h, gm_ref, w1_ref, w2_ref)
    o_ref[...] = _rms_unit(h) * gf_ref[...]


def _post_call(a2, h2, wo, gm, w1, w2, gf):
    n, d = h2.shape
    tm = min(WIDE_TILE, n)
    row = lambda i: (i, 0)
    return pl.pallas_call(
        _post_kernel,
        grid=(n // tm,),
        in_specs=[pl.BlockSpec((tm, d), row), pl.BlockSpec((tm, d), row),
                  _resident((d, d)), _resident((1, d)), _spec_of(w1), _spec_of(w2),
                  _resident((1, d))],
        out_specs=pl.BlockSpec((tm, d), row),
        out_shape=jax.ShapeDtypeStruct((n, d), F32),
        compiler_params=_params(1),
        name="post",
    )(a2, h2, wo, gm, _array_of(w1), _array_of(w2), gf)


def _row(v):
    return v.reshape(1, -1).astype(F32)


def kernel(x_prompt, x_sample, state_conv, cache_k, cache_v, page_table, norm_mix_g, norm_mlp_g, conv_pw1_w, conv_pw1_b, conv_dw_w, conv_dw_b, conv_ln_g, conv_ln_b, conv_pw2_w, conv_pw2_b, kv_norm_g, w_k, w_v, w_q, lambda_q1, lambda_k1, lambda_q2, lambda_k2, subln_g, w_o, rel_bias_table, mlp_w1, mlp_w2, final_norm_g):
    bsz, seq, d = x_prompt.shape
    db, dseq, _ = x_sample.shape
    n_pool, page = cache_k.shape[0], cache_k.shape[1]
    n_pages = page_table.shape[1]
    past_len = n_pages * page
    hist = CONV_WIDTH - 1

    pw1 = conv_pw1_w[0].astype(BF16)
    pw1b = _row(conv_pw1_b[0])
    dw = jnp.pad(conv_dw_w[0].astype(F32), ((0, HIST_ROWS - CONV_WIDTH), (0, 0)))
    dwb, ln_g, ln_b = _row(conv_dw_b[0]), _row(conv_ln_g[0]), _row(conv_ln_b[0])
    pw2 = conv_pw2_w[0].astype(BF16)
    pw2b = _row(conv_pw2_b[0])
    w1s, w2s = mlp_w1.astype(BF16), mlp_w2.astype(BF16)
    w1 = [_Layer(w1s, l) for l in range(mlp_w1.shape[0])]
    w2 = [_Layer(w2s, l) for l in range(mlp_w2.shape[0])]
    wk, wv, wq, wo = w_k.astype(BF16), w_v.astype(BF16), w_q[0].astype(BF16), w_o[0].astype(BF16)
    lamv = jnp.pad(jnp.stack([lambda_q1[0], lambda_k1[0], lambda_q2[0], lambda_k2[0]]).astype(F32),
                   ((0, SUBLANES - 4), (0, LANES - HEAD_DIM)))
    subg = _row(subln_g[0])
    rel = rel_bias_table.astype(F32)

    g_mix0, g_mix1, g_mlp0, g_kv = _row(norm_mix_g[0]), _row(norm_mix_g[1]), _row(norm_mlp_g[0]), _row(kv_norm_g)

    def back(a2, h2):
        return _post_call(a2, h2, wo, _row(norm_mlp_g[1]), w1[1], w2[1], _row(final_norm_g))

    x_p = x_prompt.reshape(bsz * seq, d)
    g_p = _glu_call(x_p, g_mix0, pw1, pw1b)
    dw_rows = jnp.broadcast_to(dw[:, None, :], (HIST_ROWS, SUBLANES, d))
    c_p = _conv_long_call(g_p.reshape(bsz, seq, d), dw_rows, dwb, ln_g, ln_b).reshape(bsz * seq, d)
    h_p = _pw2_mlp_call(c_p, x_p, pw2, pw2b, g_mlp0, w1[0], w2[0])
    k_p, v_p, q_p, kb_p, va_p = _qkv_call(h_p, g_kv, g_mix1, wk, wv, wq)
    assert min(ATTN_BLOCK, seq) % MAX_DISTANCE == 0
    r = jnp.arange(MAX_DISTANCE, dtype=jnp.int32)[:, None]
    c = jnp.arange(MAX_DISTANCE, dtype=jnp.int32)[None, :]
    bias_p = jnp.stack([_shifted_bias(rel, r - c, r >= c),
                        _shifted_bias(rel, MAX_DISTANCE + r - c, jnp.full((MAX_DISTANCE,) * 2, True))], axis=1)
    a_p = _attn_call(q_p.reshape(bsz, seq, d), kb_p.reshape(bsz, seq, d),
                     va_p.reshape(bsz, seq, 2 * d), bias_p, lamv, subg)
    y_prompt = back(a_p.reshape(bsz * seq, d), h_p).reshape(bsz, seq, d)
    conv_state_prompt = g_p.reshape(bsz, seq, d)[:, seq - hist:][None]

    st_s = jnp.pad(state_conv[0].astype(F32), ((0, 0), (HIST_ROWS - hist, 0), (0, 0)))
    cs_s, h_s, k_s, v_s, q_s = _short_front_call(
        x_sample.reshape(db * dseq, d), st_s, g_mix0, pw1, pw1b, dw, dwb, ln_g, ln_b,
        pw2, pw2b, g_mlp0, w1[0], w2[0], g_kv, g_mix1, wk, wv, wq)
    prow = page * N_HEADS
    n_cols = 2 * N_HEADS * dseq
    assert n_cols == LANES and dseq <= page
    col_q = jnp.tile(jnp.arange(dseq, dtype=jnp.int32), 2 * N_HEADS)[None, :]
    key = jnp.arange(page, dtype=jnp.int32)[:, None]
    pick = lambda b4: jnp.concatenate([b4[h, :, 2 * h * dseq:2 * (h + 1) * dseq] for h in range(N_HEADS)], axis=1)
    blast = pick(_shifted_bias(rel, page + col_q - key, jnp.full((page, n_cols), True)))
    bnew = pick(_shifted_bias(rel, col_q - key, (key <= col_q) & (key < dseq)))
    a_s = _decode_call(page_table.astype(jnp.int32), q_s.reshape(db, dseq, d),
                       cache_k.reshape(n_pool, prow, V_HEAD_DIM), cache_v.reshape(n_pool, prow, V_HEAD_DIM),
                       k_s.reshape(db, dseq, d), v_s.reshape(db, dseq, d), blast, bnew, lamv, subg)
    y_sample = back(a_s.reshape(db * dseq, d), h_s).reshape(db, dseq, d)
    conv_state_sample = cs_s[None]

    kv_shape = (N_HEADS, V_HEAD_DIM)
    return (y_prompt, y_sample, conv_state_prompt, conv_state_sample,
            k_p.reshape(bsz, seq, *kv_shape), v_p.reshape(bsz, seq, *kv_shape),
            k_s.reshape(db, dseq, *kv_shape), v_s.reshape(db, dseq, *kv_shape))
```

```python
import functools
import math

import jax
import jax.numpy as jnp
from jax import lax
from jax.experimental import pallas as pl
from jax.experimental.pallas import tpu as pltpu

F32 = jnp.float32
BF16 = jnp.bfloat16

N_HEADS = 8
HEAD_DIM = 64
V_HEAD_DIM = 2 * HEAD_DIM
CONV_WIDTH = 31
N_BUCKETS = 32
MAX_DISTANCE = 128
NORM_EPS = 1e-6
NEG_INF = -1e30
LAM_INIT = 0.8 - 0.6 * math.exp(-0.3 * 1)
LOG2E = math.log2(math.e)
Q_SCALE = HEAD_DIM ** -0.5 * LOG2E

SUBLANES = 8
LANES = 128
HIST_ROWS = 32
VMEM_LIMIT = 56 * 1024 * 1024

TOKEN_TILE = 512
WIDE_TILE = 1024
FF_CHUNK = 512
CONV_ROWS = 32
ATTN_BLOCK = 512
ATTN_HEADS = 2
PAGES_PER_STEP = 16
DECODE_GROUP = 4


def _params(n_axes):
    return pltpu.CompilerParams(dimension_semantics=("arbitrary",) * n_axes,
                                vmem_limit_bytes=VMEM_LIMIT)


def _resident(shape):
    nd = len(shape)
    return pl.BlockSpec(shape, lambda *_: (0,) * nd, pipeline_mode=pl.Buffered(1))


class _Layer:
    def __init__(self, stack, index):
        self.stack, self.index, self.shape = stack, index, stack.shape[1:]

    def spec(self):
        idx = (self.index,) + (0,) * len(self.shape)
        return pl.BlockSpec((None,) + self.shape, lambda *_: idx, pipeline_mode=pl.Buffered(1))


def _spec_of(a):
    return a.spec() if isinstance(a, _Layer) else _resident(a.shape)


def _array_of(a):
    return a.stack if isinstance(a, _Layer) else a


def _rms_unit(x):
    return x * lax.rsqrt(jnp.mean(x * x, axis=-1, keepdims=True) + NORM_EPS)


def _mlp(h, g_ref, w1_ref, w2_ref):
    hn = (_rms_unit(h) * g_ref[...]).astype(BF16)
    d_ff = w1_ref.shape[1]
    acc = h
    for c in range(d_ff // FF_CHUNK):
        u = jnp.dot(hn, w1_ref[:, c * FF_CHUNK:(c + 1) * FF_CHUNK], preferred_element_type=F32)
        u = jnp.maximum(u, 0.0)
        acc = acc + jnp.dot((u * u).astype(BF16), w2_ref[c * FF_CHUNK:(c + 1) * FF_CHUNK, :],
                            preferred_element_type=F32)
    return acc


def _glu(x, g_ref, w_ref, b_ref):
    d = x.shape[1]
    hn = (_rms_unit(x) * g_ref[...]).astype(BF16)
    u = jnp.dot(hn, w_ref[...], preferred_element_type=F32) + b_ref[...]
    return u[:, :d] * jax.nn.sigmoid(u[:, d:])


def _glu_kernel(x_ref, g_ref, w_ref, b_ref, o_ref):
    o_ref[...] = _glu(x_ref[...], g_ref, w_ref, b_ref)


def _glu_call(x2, g, w, b):
    n, d = x2.shape
    tm = min(WIDE_TILE, n)
    return pl.pallas_call(
        _glu_kernel,
        grid=(n // tm,),
        in_specs=[pl.BlockSpec((tm, d), lambda i: (i, 0)),
                  _resident((1, d)), _resident((d, 2 * d)), _resident((1, 2 * d))],
        out_specs=pl.BlockSpec((tm, d), lambda i: (i, 0)),
        out_shape=jax.ShapeDtypeStruct((n, d), F32),
        compiler_params=_params(1),
        name="glu",
    )(x2, g, w, b)


def _ln_swish(c, ln_g_ref, ln_b_ref):
    mu = jnp.mean(c, axis=-1, keepdims=True)
    cc = c - mu
    var = jnp.mean(cc * cc, axis=-1, keepdims=True)
    y = cc * lax.rsqrt(var + NORM_EPS) * ln_g_ref[...] + ln_b_ref[...]
    return y * jax.nn.sigmoid(y)


def _conv_long_kernel(g_ref, hist_ref, dw_ref, dwb_ref, ln_g_ref, ln_b_ref, o_ref, sh_ref):
    tl = g_ref.shape[1]
    first = pl.program_id(1) == 0
    hist = jnp.where(first, 0.0, hist_ref[0])
    sh_ref[0, 0:HIST_ROWS, :] = hist
    sh_ref[0, HIST_ROWS:HIST_ROWS + tl, :] = g_ref[0]
    n_sh = tl + HIST_ROWS - SUBLANES
    for b in range(1, SUBLANES):
        sh_ref[b, 0:n_sh, :] = sh_ref[0, b:b + n_sh, :]

    def chunk(ci, carry):
        r0 = pl.multiple_of(ci * CONV_ROWS, CONV_ROWS)
        groups = range(0, CONV_ROWS, SUBLANES)
        acc = [jnp.broadcast_to(dwb_ref[...], (SUBLANES, dwb_ref.shape[1])) for _ in groups]
        for k in range(CONV_WIDTH):
            off = HIST_ROWS - (CONV_WIDTH - 1) + k
            a, b = divmod(off, SUBLANES)
            w_k = dw_ref[k]
            acc = [ac + w_k * sh_ref[b, pl.ds(r0 + a * SUBLANES + g, SUBLANES), :] for ac, g in zip(acc, groups)]
        c = jnp.concatenate(acc, axis=0)
        o_ref[0, pl.ds(r0, CONV_ROWS), :] = _ln_swish(c, ln_g_ref, ln_b_ref).astype(o_ref.dtype)
        return carry

    lax.fori_loop(0, tl // CONV_ROWS, chunk, 0, unroll=4)


def _conv_long_call(g3, dw, dwb, ln_g, ln_b):
    bsz, t, d = g3.shape
    tl = min(TOKEN_TILE, t)
    per = tl // HIST_ROWS
    return pl.pallas_call(
        _conv_long_kernel,
        grid=(bsz, t // tl),
        in_specs=[pl.BlockSpec((1, tl, d), lambda b, j: (b, j, 0)),
                  pl.BlockSpec((1, HIST_ROWS, d), lambda b, j: (b, jnp.maximum(j * per - 1, 0), 0)),
                  _resident((HIST_ROWS, SUBLANES, d)), _resident((1, d)), _resident((1, d)), _resident((1, d))],
        out_specs=pl.BlockSpec((1, tl, d), lambda b, j: (b, j, 0)),
        out_shape=jax.ShapeDtypeStruct((bsz, t, d), BF16),
        scratch_shapes=[pltpu.VMEM((SUBLANES, tl + HIST_ROWS, d), F32)],
        compiler_params=_params(2),
        name="conv_long",
    )(g3, g3, dw, dwb, ln_g, ln_b)


def _short_front_kernel(x_ref, st_ref, g0_ref, pw1_ref, pw1b_ref, dw_ref, dwb_ref, ln_g_ref, ln_b_ref,
                        pw2_ref, pw2b_ref, gm_ref, w1_ref, w2_ref, gkv_ref, gq_ref, wk_ref, wv_ref, wq_ref,
                        cs_ref, h_ref, k_ref, v_ref, q_ref, ext_ref):
    n, d = x_ref.shape
    s = st_ref.shape[0]
    t = n // s
    x = x_ref[...]
    g = _glu(x, g0_ref, pw1_ref, pw1b_ref)
    ext_ref[:, 0:HIST_ROWS, :] = st_ref[...]
    ext_ref[:, HIST_ROWS:HIST_ROWS + t, :] = g.reshape(s, t, d)
    n_hist = cs_ref.shape[1]
    cs_ref[...] = ext_ref[:, HIST_ROWS + t - n_hist:HIST_ROWS + t, :]
    acc = jnp.broadcast_to(dwb_ref[...].reshape(1, 1, d), (s, t, d))
    for k in range(CONV_WIDTH):
        off = HIST_ROWS - (CONV_WIDTH - 1) + k
        acc = acc + dw_ref[k:k + 1, :].reshape(1, 1, d) * ext_ref[:, off:off + t, :]
    c = _ln_swish(acc.reshape(n, d), ln_g_ref, ln_b_ref).astype(BF16)
    h = x + jnp.dot(c, pw2_ref[...], preferred_element_type=F32) + pw2b_ref[...]
    h = _mlp(h, gm_ref, w1_ref, w2_ref)
    h_ref[...] = h
    k_ref[...], v_ref[...], q_ref[...] = _project_qkv(h, gkv_ref, gq_ref, wk_ref, wv_ref, wq_ref)


def _short_front_call(x2, st3, *params):
    n, d = x2.shape
    s = st3.shape[0]
    out = lambda dt: jax.ShapeDtypeStruct((n, d), dt)
    return pl.pallas_call(
        _short_front_kernel,
        grid=(1,),
        in_specs=[_spec_of(a) for a in (x2, st3) + params],
        out_specs=[pl.BlockSpec((s, CONV_WIDTH - 1, d), lambda i: (0, 0, 0))]
                  + [pl.BlockSpec((n, d), lambda i: (0, 0))] * 4,
        out_shape=[jax.ShapeDtypeStruct((s, CONV_WIDTH - 1, d), F32), out(F32), out(F32), out(F32), out(BF16)],
        scratch_shapes=[pltpu.VMEM((s, HIST_ROWS + n // s, d), F32)],
        compiler_params=_params(1),
        name="short_front",
    )(x2, st3, *[_array_of(a) for a in params])


def _pw2_mlp_kernel(c_ref, x_ref, pw2_ref, pw2b_ref, gm_ref, w1_ref, w2_ref, o_ref):
    h = x_ref[...] + jnp.dot(c_ref[...], pw2_ref[...], preferred_element_type=F32) + pw2b_ref[...]
    o_ref[...] = _mlp(h, gm_ref, w1_ref, w2_ref)


def _pw2_mlp_call(c2, x2, pw2, pw2b, gm, w1, w2):
    n, d = x2.shape
    tm = min(WIDE_TILE, n)
    row = lambda i: (i, 0)
    return pl.pallas_call(
        _pw2_mlp_kernel,
        grid=(n // tm,),
        in_specs=[pl.BlockSpec((tm, d), row), pl.BlockSpec((tm, d), row),
                  _resident((d, d)), _resident((1, d)), _resident((1, d)),
                  _spec_of(w1), _spec_of(w2)],
        out_specs=pl.BlockSpec((tm, d), row),
        out_shape=jax.ShapeDtypeStruct((n, d), F32),
        compiler_params=_params(1),
        name="pw2_mlp",
    )(c2, x2, pw2, pw2b, gm, _array_of(w1), _array_of(w2))


def _project_qkv(h, gkv_ref, gq_ref, wk_ref, wv_ref, wq_ref):
    y = _rms_unit(h)
    kvn = (y * gkv_ref[...]).astype(BF16)
    hn = (y * gq_ref[...]).astype(BF16)
    k = jnp.dot(kvn, wk_ref[...], preferred_element_type=F32)
    v = jnp.dot(kvn, wv_ref[...], preferred_element_type=F32)
    q = jnp.dot(hn, wq_ref[...], preferred_element_type=F32)
    return k, v, (q * Q_SCALE).astype(BF16)


def _qkv_kernel(h_ref, gkv_ref, gq_ref, wk_ref, wv_ref, wq_ref,
                k_ref, v_ref, q_ref, kb_ref, va_ref):
    tm, d = h_ref.shape
    k, v, q = _project_qkv(h_ref[...], gkv_ref, gq_ref, wk_ref, wv_ref, wq_ref)
    k_ref[...] = k
    v_ref[...] = v
    q_ref[...] = q
    kb_ref[...] = k.astype(BF16)
    ones_col = jnp.ones((tm, V_HEAD_DIM), BF16)
    vb = v.astype(BF16)
    for h in range(N_HEADS):
        va_ref[:, 2 * h * V_HEAD_DIM:(2 * h + 1) * V_HEAD_DIM] = vb[:, h * V_HEAD_DIM:(h + 1) * V_HEAD_DIM]
        va_ref[:, (2 * h + 1) * V_HEAD_DIM:(2 * h + 2) * V_HEAD_DIM] = ones_col


def _qkv_call(h2, gkv, gq, wk, wv, wq):
    n, d = h2.shape
    tm = min(TOKEN_TILE, n)
    row = lambda i: (i, 0)
    return pl.pallas_call(
        _qkv_kernel,
        grid=(n // tm,),
        in_specs=[pl.BlockSpec((tm, d), row), _resident((1, d)), _resident((1, d)),
                  _resident((d, d)), _resident((d, d)), _resident((d, d))],
        out_specs=[pl.BlockSpec((tm, d), row), pl.BlockSpec((tm, d), row), pl.BlockSpec((tm, d), row),
                   pl.BlockSpec((tm, d), row),
                   pl.BlockSpec((tm, 2 * d), row)],
        out_shape=[jax.ShapeDtypeStruct((n, d), F32), jax.ShapeDtypeStruct((n, d), F32),
                   jax.ShapeDtypeStruct((n, d), BF16), jax.ShapeDtypeStruct((n, d), BF16),
                   jax.ShapeDtypeStruct((n, 2 * d), BF16)],
        compiler_params=_params(1),
        name="qkv",
    )(h2, gkv, gq, wk, wv, wq)


def _lambda(lam_ref):
    lv = lam_ref[...]
    s1 = jnp.sum(lv[0:1, :] * lv[1:2, :], axis=-1, keepdims=True)
    s2 = jnp.sum(lv[2:3, :] * lv[3:4, :], axis=-1, keepdims=True)
    return jnp.exp(s1) - jnp.exp(s2) + LAM_INIT


def _sub_norm(o, subg_ref):
    return (_rms_unit(o) * subg_ref[...]) * (1.0 - LAM_INIT)


def _rel_bucket(n):
    max_exact = N_BUCKETS // 2
    nf = jnp.maximum(n, 1).astype(F32)
    large = max_exact + (jnp.log(nf / max_exact) / math.log(MAX_DISTANCE / max_exact)
                         * (N_BUCKETS - max_exact)).astype(jnp.int32)
    large = jnp.minimum(large, N_BUCKETS - 1)
    return jnp.where(n < max_exact, n, large)


def _shifted_bias(rel_table, dist, valid):
    bucket = _rel_bucket(jnp.maximum(dist, 0))[None]
    lead = (N_HEADS,) + (1,) * dist.ndim
    vals = jnp.zeros((N_HEADS,) + dist.shape, F32)
    for n in range(N_BUCKETS - 1):
        vals = jnp.where(bucket == n, (rel_table[n] - rel_table[N_BUCKETS - 1]).reshape(lead), vals)
    return jnp.where(valid[None], vals * LOG2E, NEG_INF)


def _attn_kernel(q_ref, k_ref, va_ref, bias_ref, lam_ref, subg_ref, o_ref, qz_ref, acc_ref, m_ref):
    blk = q_ref.shape[1]
    n_hd = q_ref.shape[2] // V_HEAD_DIM
    i = pl.program_id(2)
    for hd in range(n_hd):
        q = q_ref[0, :, hd * V_HEAD_DIM:(hd + 1) * V_HEAD_DIM]
        first_half = lax.broadcasted_iota(jnp.int32, q.shape, 1) < HEAD_DIM
        zero = jnp.zeros_like(q)
        qz_ref[2 * hd * blk:(2 * hd + 1) * blk, :] = jnp.where(first_half, q, zero)
        qz_ref[(2 * hd + 1) * blk:(2 * hd + 2) * blk, :] = jnp.where(first_half, zero, q)
    acc_ref[...] = jnp.zeros_like(acc_ref)
    m_ref[...] = jnp.full_like(m_ref, NEG_INF)
    reps = blk // LANES

    nt = (((1,), (1,)), ((), ()))
    chains = [(hd, slice((2 * hd + c) * blk, (2 * hd + c + 1) * blk)) for hd in range(n_hd) for c in range(2)]

    def scores(j):
        r0 = pl.multiple_of(j * blk, blk)
        return [lax.dot_general(qz_ref[rows, :], k_ref[0, pl.ds(r0, blk), hd * V_HEAD_DIM:(hd + 1) * V_HEAD_DIM],
                                nt, preferred_element_type=F32) for hd, rows in chains]

    sub = bias_ref.shape[2]
    nb = blk // sub

    def add_bias(sc, kind, hd):
        if kind == "prev":
            top = jnp.concatenate([sc[:sub, :blk - sub], sc[:sub, blk - sub:] + bias_ref[hd, 1]], axis=1)
            return jnp.concatenate([top, sc[sub:, :]], axis=0) if nb > 1 else top
        rows = []
        for r in range(nb):
            tiles = [bias_ref[hd, 0] if c == r else bias_ref[hd, 1] if c == r - 1 else
                     jnp.full((sub, sub), 0.0 if c < r else NEG_INF, F32) for c in range(nb)]
            rows.append(jnp.concatenate(tiles, axis=1))
        return sc + jnp.concatenate(rows, axis=0)

    def accumulate(j, s, bias):
        r0 = pl.multiple_of(j * blk, blk)
        if bias is not None:
            s = [add_bias(sc, bias, hd) for sc, (hd, _) in zip(s, chains)]
        m_prev = [m_ref[rows, :] for _, rows in chains]
        m_new = [jnp.maximum(mp, jnp.max(sc, axis=1, keepdims=True)) for mp, sc in zip(m_prev, s)]
        alpha = [jnp.exp2(mp - mn) for mp, mn in zip(m_prev, m_new)]
        p = [jnp.exp2(sc - jnp.concatenate([mn] * reps, axis=1)).astype(BF16) for sc, mn in zip(s, m_new)]
        pv = [jnp.dot(pc, va_ref[0, pl.ds(r0, blk), 2 * hd * V_HEAD_DIM:(2 * hd + 2) * V_HEAD_DIM],
                      preferred_element_type=F32) for pc, (hd, _) in zip(p, chains)]
        for (_, rows), al, pvc, mn in zip(chains, alpha, pv, m_new):
            acc_ref[rows, :] = jnp.concatenate([al, al], axis=1) * acc_ref[rows, :] + pvc
            m_ref[rows, :] = mn

    def process(blocks):
        s_all = [scores(j) for j, _ in blocks]
        for (j, bias), s in zip(blocks, s_all):
            accumulate(j, s, bias)

    n_far = jnp.maximum(i - 1, 0)

    def far_pair(t, carry):
        process([(2 * t, None), (2 * t + 1, None)])
        return carry

    lax.fori_loop(0, n_far // 2, far_pair, 0)
    odd = n_far % 2 == 1
    diag, prev = "diag", "prev"

    @pl.when(i == 0)
    def _():
        process([(i, diag)])

    @pl.when(jnp.logical_and(i > 0, jnp.logical_not(odd)))
    def _():
        process([(i - 1, prev), (i, diag)])

    @pl.when(odd)
    def _():
        process([(i - 2, None), (i - 1, prev), (i, diag)])

    lam = _lambda(lam_ref)
    for hd in range(n_hd):
        a1 = acc_ref[2 * hd * blk:(2 * hd + 1) * blk, :]
        a2 = acc_ref[(2 * hd + 1) * blk:(2 * hd + 2) * blk, :]
        o = a1[:, :V_HEAD_DIM] * (1.0 / a1[:, V_HEAD_DIM:]) - a2[:, :V_HEAD_DIM] * (lam / a2[:, V_HEAD_DIM:])
        o_ref[0, :, hd * V_HEAD_DIM:(hd + 1) * V_HEAD_DIM] = _sub_norm(o, subg_ref).astype(o_ref.dtype)


def _attn_call(q3, k3, va3, bias, lamv, subg):
    bsz, t, d = q3.shape
    blk = min(ATTN_BLOCK, t)
    n_hd = ATTN_HEADS
    rows = 2 * n_hd * blk
    return pl.pallas_call(
        _attn_kernel,
        grid=(bsz, N_HEADS // n_hd, t // blk),
        in_specs=[pl.BlockSpec((1, blk, n_hd * V_HEAD_DIM), lambda b, h, i: (b, i, h)),
                  pl.BlockSpec((1, t, n_hd * V_HEAD_DIM), lambda b, h, i: (b, 0, h)),
                  pl.BlockSpec((1, t, 2 * n_hd * V_HEAD_DIM), lambda b, h, i: (b, 0, h)),
                  pl.BlockSpec((n_hd, 2, MAX_DISTANCE, MAX_DISTANCE), lambda b, h, i: (h, 0, 0, 0)),
                  _resident((SUBLANES, LANES)), _resident((1, V_HEAD_DIM))],
        out_specs=pl.BlockSpec((1, blk, n_hd * V_HEAD_DIM), lambda b, h, i: (b, i, h)),
        out_shape=jax.ShapeDtypeStruct((bsz, t, d), BF16),
        scratch_shapes=[pltpu.VMEM((rows, V_HEAD_DIM), BF16),
                        pltpu.VMEM((rows, 2 * V_HEAD_DIM), F32),
                        pltpu.VMEM((rows, LANES), F32)],
        compiler_params=_params(3),
        name="attn_prompt",
    )(q3, k3, va3, bias, lamv, subg)


def _decode_kernel(pt_ref, q_ref, *refs, n_pages_step):
    g_n = n_pages_step
    k_refs = refs[:g_n]
    v_refs = refs[g_n:2 * g_n]
    (knew_ref, vnew_ref, blast_ref, bnew_ref, lam_ref, subg_ref,
     o_ref, qbd_ref, acc_ref, m_ref, l_ref) = refs[2 * g_n:]
    st = pl.program_id(1)
    last = pl.num_programs(1) - 1
    tq = q_ref.shape[1]
    cols, d = qbd_ref.shape
    page = blast_ref.shape[0]

    @pl.when(st == 0)
    def _():
        qt = jnp.concatenate([q_ref[0]] * (cols // tq), axis=0)
        row = lax.broadcasted_iota(jnp.int32, (cols, d), 0)
        lane = lax.broadcasted_iota(jnp.int32, (cols, d), 1)
        qbd_ref[...] = jnp.where(row // tq == lane // HEAD_DIM, qt, jnp.zeros_like(qt))
        acc_ref[...] = jnp.zeros_like(acc_ref)
        m_ref[...] = jnp.full_like(m_ref, NEG_INF)
        l_ref[...] = jnp.zeros_like(l_ref)

    def slab(ref):
        return jnp.concatenate([ref[0, pl.ds(h, page, stride=N_HEADS), :] for h in range(N_HEADS)],
                               axis=1).astype(BF16)

    def per_row(v):
        return jnp.broadcast_to(v, (cols, cols)).T

    nt = (((1,), (1,)), ((), ()))
    tn = (((0,), (0,)), ((), ()))

    def scores(k):
        return lax.dot_general(k, qbd_ref[...], nt, preferred_element_type=F32)

    def update(s, v):
        m_prev = m_ref[...]
        m_new = jnp.maximum(m_prev, jnp.max(s, axis=0, keepdims=True))
        alpha = jnp.exp2(m_prev - m_new)
        p = jnp.exp2(s - m_new)
        l_ref[...] = alpha * l_ref[...] + jnp.sum(p, axis=0, keepdims=True)
        pv = lax.dot_general(p.astype(BF16), v, tn, preferred_element_type=F32)
        acc_ref[...] = jnp.concatenate([per_row(alpha)] * (d // cols), axis=1) * acc_ref[...] + pv
        m_ref[...] = m_new

    sub = math.gcd(DECODE_GROUP, g_n)
    groups = [slice(a, a + sub) for a in range(0, g_n, sub)]
    s_all = [scores(jnp.concatenate([slab(r) for r in k_refs[gs]], axis=0)) for gs in groups]
    is_last = (st == last).astype(F32)
    s_all[-1] = s_all[-1] + jnp.concatenate(
        [jnp.zeros(((sub - 1) * page, cols), F32), is_last * blast_ref[...]], axis=0)
    for gs, s in zip(groups, s_all):
        update(s, jnp.concatenate([slab(r) for r in v_refs[gs]], axis=0))

    def new_slab(ref):
        return jnp.concatenate([ref[0], jnp.zeros((page - tq, d), F32)], axis=0).astype(BF16)

    @pl.when(st == last)
    def _():
        update(scores(new_slab(knew_ref)) + bnew_ref[...], new_slab(vnew_ref))
        lam = _lambda(lam_ref)
        o_all = acc_ref[...] / jnp.concatenate([per_row(l_ref[...])] * (d // cols), axis=1)
        for h in range(N_HEADS):
            blk = o_all[2 * h * tq:(2 * h + 2) * tq, h * V_HEAD_DIM:(h + 1) * V_HEAD_DIM]
            o = blk[:tq] - lam * blk[tq:]
            o_ref[0, :, h * V_HEAD_DIM:(h + 1) * V_HEAD_DIM] = _sub_norm(o, subg_ref).astype(o_ref.dtype)


def _decode_call(page_table, q3, cache_k3, cache_v3, knew, vnew, blast, bnew, lamv, subg):
    db, n_pages = page_table.shape
    _, tq, d = q3.shape
    prow = cache_k3.shape[1]
    page = prow // N_HEADS
    g_n = math.gcd(PAGES_PER_STEP, n_pages)
    cols = N_HEADS * 2 * tq

    def page_spec(g):
        return pl.BlockSpec((1, prow, V_HEAD_DIM), lambda b, s, pt: (pt[b, s * g_n + g], 0, 0))

    per_b = lambda b, s, pt: (b, 0, 0)
    const2 = lambda b, s, pt: (0, 0)
    tile = pl.BlockSpec((page, cols), const2)
    grid_spec = pltpu.PrefetchScalarGridSpec(
        num_scalar_prefetch=1,
        grid=(db, n_pages // g_n),
        in_specs=([pl.BlockSpec((1, tq, d), per_b)]
                  + [page_spec(g) for g in range(g_n)] + [page_spec(g) for g in range(g_n)]
                  + [pl.BlockSpec((1, tq, d), per_b), pl.BlockSpec((1, tq, d), per_b),
                     tile, tile,
                     pl.BlockSpec((SUBLANES, LANES), const2), pl.BlockSpec((1, V_HEAD_DIM), const2)]),
        out_specs=pl.BlockSpec((1, tq, d), per_b),
        scratch_shapes=[pltpu.VMEM((cols, d), BF16), pltpu.VMEM((cols, d), F32),
                        pltpu.VMEM((1, cols), F32), pltpu.VMEM((1, cols), F32)],
    )
    return pl.pallas_call(
        functools.partial(_decode_kernel, n_pages_step=g_n),
        grid_spec=grid_spec,
        out_shape=jax.ShapeDtypeStruct((db, tq, d), BF16),
        compiler_params=_params(2),
        name="attn_decode",
    )(page_table, q3, *([cache_k3] * g_n), *([cache_v3] * g_n), knew, vnew, blast, bnew, lamv, subg)


def _post_kernel(a_ref, h_ref, wo_ref, gm_ref, w1_ref, w2_ref, gf_ref, o_ref):
    h = h_ref[...] + jnp.dot(a_ref[...], wo_ref[...], preferred_element_type=F32)
    h = _mlp(h, gm_ref, w1_ref, w2_ref)
    o_ref[...] = _rms_unit(h) * gf_ref[...]


def _post_call(a2, h2, wo, gm, w1, w2, gf):
    n, d = h2.shape
    tm = min(WIDE_TILE, n)
    row = lambda i: (i, 0)
    return pl.pallas_call(
        _post_kernel,
        grid=(n // tm,),
        in_specs=[pl.BlockSpec((tm, d), row), pl.BlockSpec((tm, d), row),
                  _resident((d, d)), _resident((1, d)), _spec_of(w1), _spec_of(w2),
                  _resident((1, d))],
        out_specs=pl.BlockSpec((tm, d), row),
        out_shape=jax.ShapeDtypeStruct((n, d), F32),
        compiler_params=_params(1),
        name="post",
    )(a2, h2, wo, gm, _array_of(w1), _array_of(w2), gf)


def _row(v):
    return v.reshape(1, -1).astype(F32)


def kernel(x_prompt, x_sample, state_conv, cache_k, cache_v, page_table, norm_mix_g, norm_mlp_g, conv_pw1_w, conv_pw1_b, conv_dw_w, conv_dw_b, conv_ln_g, conv_ln_b, conv_pw2_w, conv_pw2_b, kv_norm_g, w_k, w_v, w_q, lambda_q1, lambda_k1, lambda_q2, lambda_k2, subln_g, w_o, rel_bias_table, mlp_w1, mlp_w2, final_norm_g):
    bsz, seq, d = x_prompt.shape
    db, dseq, _ = x_sample.shape
    n_pool, page = cache_k.shape[0], cache_k.shape[1]
    n_pages = page_table.shape[1]
    past_len = n_pages * page
    hist = CONV_WIDTH - 1

    pw1 = conv_pw1_w[0].astype(BF16)
    pw1b = _row(conv_pw1_b[0])
    dw = jnp.pad(conv_dw_w[0].astype(F32), ((0, HIST_ROWS - CONV_WIDTH), (0, 0)))
    dwb, ln_g, ln_b = _row(conv_dw_b[0]), _row(conv_ln_g[0]), _row(conv_ln_b[0])
    pw2 = conv_pw2_w[0].astype(BF16)
    pw2b = _row(conv_pw2_b[0])
    w1s, w2s = mlp_w1.astype(BF16), mlp_w2.astype(BF16)
    w1 = [_Layer(w1s, l) for l in range(mlp_w1.shape[0])]
    w2 = [_Layer(w2s, l) for l in range(mlp_w2.shape[0])]
    wk, wv, wq, wo = w_k.astype(BF16), w_v.astype(BF16), w_q[0].astype(BF16), w_o[0].astype(BF16)
    lamv = jnp.pad(jnp.stack([lambda_q1[0], lambda_k1[0], lambda_q2[0], lambda_k2[0]]).astype(F32),
                   ((0, SUBLANES - 4), (0, LANES - HEAD_DIM)))
    subg = _row(subln_g[0])
    rel = rel_bias_table.astype(F32)

    g_mix0, g_mix1, g_mlp0, g_kv = _row(norm_mix_g[0]), _row(norm_mix_g[1]), _row(norm_mlp_g[0]), _row(kv_norm_g)

    def back(a2, h2):
        return _post_call(a2, h2, wo, _row(norm_mlp_g[1]), w1[1], w2[1], _row(final_norm_g))

    x_p = x_prompt.reshape(bsz * seq, d)
    g_p = _glu_call(x_p, g_mix0, pw1, pw1b)
    dw_rows = jnp.broadcast_to(dw[:, None, :], (HIST_ROWS, SUBLANES, d))
    c_p = _conv_long_call(g_p.reshape(bsz, seq, d), dw_rows, dwb, ln_g, ln_b).reshape(bsz * seq, d)
    h_p = _pw2_mlp_call(c_p, x_p, pw2, pw2b, g_mlp0, w1[0], w2[0])
    k_p, v_p, q_p, kb_p, va_p = _qkv_call(h_p, g_kv, g_mix1, wk, wv, wq)
    assert min(ATTN_BLOCK, seq) % MAX_DISTANCE == 0
    r = jnp.arange(MAX_DISTANCE, dtype=jnp.int32)[:, None]
    c = jnp.arange(MAX_DISTANCE, dtype=jnp.int32)[None, :]
    bias_p = jnp.stack([_shifted_bias(rel, r - c, r >= c),
                        _shifted_bias(rel, MAX_DISTANCE + r - c, jnp.full((MAX_DISTANCE,) * 2, True))], axis=1)
    a_p = _attn_call(q_p.reshape(bsz, seq, d), kb_p.reshape(bsz, seq, d),
                     va_p.reshape(bsz, seq, 2 * d), bias_p, lamv, subg)
    y_prompt = back(a_p.reshape(bsz * seq, d), h_p).reshape(bsz, seq, d)
    conv_state_prompt = g_p.reshape(bsz, seq, d)[:, seq - hist:][None]

    st_s = jnp.pad(state_conv[0].astype(F32), ((0, 0), (HIST_ROWS - hist, 0), (0, 0)))
    cs_s, h_s, k_s, v_s, q_s = _short_front_call(
        x_sample.reshape(db * dseq, d), st_s, g_mix0, pw1, pw1b, dw, dwb, ln_g, ln_b,
        pw2, pw2b, g_mlp0, w1[0], w2[0], g_kv, g_mix1, wk, wv, wq)
    prow = page * N_HEADS
    n_cols = 2 * N_HEADS * dseq
    assert n_cols == LANES and dseq <= page
    col_q = jnp.tile(jnp.arange(dseq, dtype=jnp.int32), 2 * N_HEADS)[None, :]
    key = jnp.arange(page, dtype=jnp.int32)[:, None]
    pick = lambda b4: jnp.concatenate([b4[h, :, 2 * h * dseq:2 * (h + 1) * dseq] for h in range(N_HEADS)], axis=1)
    blast = pick(_shifted_bias(rel, page + col_q - key, jnp.full((page, n_cols), True)))
    bnew = pick(_shifted_bias(rel, col_q - key, (key <= col_q) & (key < dseq)))
    a_s = _decode_call(page_table.astype(jnp.int32), q_s.reshape(db, dseq, d),
                       cache_k.reshape(n_pool, prow, V_HEAD_DIM), cache_v.reshape(n_pool, prow, V_HEAD_DIM),
                       k_s.reshape(db, dseq, d), v_s.reshape(db, dseq, d), blast, bnew, lamv, subg)
    y_sample = back(a_s.reshape(db * dseq, d), h_s).reshape(db, dseq, d)
    conv_state_sample = cs_s[None]

    kv_shape = (N_HEADS, V_HEAD_DIM)
    return (y_prompt, y_sample, conv_state_prompt, conv_state_sample,
            k_p.reshape(bsz, seq, *kv_shape), v_p.reshape(bsz, seq, *kv_shape),
            k_s.reshape(db, dseq, *kv_shape), v_s.reshape(db, dseq, *kv_shape))
```

```python
import functools
import math

import jax
import jax.numpy as jnp
from jax import lax
from jax.experimental import pallas as pl
from jax.experimental.pallas import tpu as pltpu

F32 = jnp.float32
BF16 = jnp.bfloat16

N_HEADS = 8
HEAD_DIM = 64
V_HEAD_DIM = 2 * HEAD_DIM
CONV_WIDTH = 31
N_BUCKETS = 32
MAX_DISTANCE = 128
NORM_EPS = 1e-6
NEG_INF = -1e30
LAM_INIT = 0.8 - 0.6 * math.exp(-0.3 * 1)
LOG2E = math.log2(math.e)
Q_SCALE = HEAD_DIM ** -0.5 * LOG2E

SUBLANES = 8
LANES = 128
HIST_ROWS = 32
VMEM_LIMIT = 56 * 1024 * 1024

TOKEN_TILE = 512
WIDE_TILE = 1024
FF_CHUNK = 512
CONV_ROWS = 32
ATTN_BLOCK = 512
ATTN_HEADS = 2
PAGES_PER_STEP = 16
DECODE_GROUP = 4


def _params(n_axes):
    return pltpu.CompilerParams(dimension_semantics=("arbitrary",) * n_axes,
                                vmem_limit_bytes=VMEM_LIMIT)


def _resident(shape):
    nd = len(shape)
    return pl.BlockSpec(shape, lambda *_: (0,) * nd, pipeline_mode=pl.Buffered(1))


class _Layer:
    def __init__(self, stack, index):
        self.stack, self.index, self.shape = stack, index, stack.shape[1:]

    def spec(self):
        idx = (self.index,) + (0,) * len(self.shape)
        return pl.BlockSpec((None,) + self.shape, lambda *_: idx, pipeline_mode=pl.Buffered(1))


def _spec_of(a):
    return a.spec() if isinstance(a, _Layer) else _resident(a.shape)


def _array_of(a):
    return a.stack if isinstance(a, _Layer) else a


def _rms_unit(x):
    return x * lax.rsqrt(jnp.mean(x * x, axis=-1, keepdims=True) + NORM_EPS)


def _mlp(h, g_ref, w1_ref, w2_ref):
    hn = (_rms_unit(h) * g_ref[...]).astype(BF16)
    d_ff = w1_ref.shape[1]
    acc = h
    for c in range(d_ff // FF_CHUNK):
        u = jnp.dot(hn, w1_ref[:, c * FF_CHUNK:(c + 1) * FF_CHUNK], preferred_element_type=F32)
        u = jnp.maximum(u, 0.0)
        acc = acc + jnp.dot((u * u).astype(BF16), w2_ref[c * FF_CHUNK:(c + 1) * FF_CHUNK, :],
                            preferred_element_type=F32)
    return acc


def _glu(x, g_ref, w_ref, b_ref):
    d = x.shape[1]
    hn = (_rms_unit(x) * g_ref[...]).astype(BF16)
    u = jnp.dot(hn, w_ref[...], preferred_element_type=F32) + b_ref[...]
    return u[:, :d] * jax.nn.sigmoid(u[:, d:])


def _glu_kernel(x_ref, g_ref, w_ref, b_ref, o_ref):
    o_ref[...] = _glu(x_ref[...], g_ref, w_ref, b_ref)


def _glu_call(x2, g, w, b):
    n, d = x2.shape
    tm = min(WIDE_TILE, n)
    return pl.pallas_call(
        _glu_kernel,
        grid=(n // tm,),
        in_specs=[pl.BlockSpec((tm, d), lambda i: (i, 0)),
                  _resident((1, d)), _resident((d, 2 * d)), _resident((1, 2 * d))],
        out_specs=pl.BlockSpec((tm, d), lambda i: (i, 0)),
        out_shape=jax.ShapeDtypeStruct((n, d), F32),
        compiler_params=_params(1),
        name="glu",
    )(x2, g, w, b)


def _ln_swish(c, ln_g_ref, ln_b_ref):
    mu = jnp.mean(c, axis=-1, keepdims=True)
    cc = c - mu
    var = jnp.mean(cc * cc, axis=-1, keepdims=True)
    y = cc * lax.rsqrt(var + NORM_EPS) * ln_g_ref[...] + ln_b_ref[...]
    return y * jax.nn.sigmoid(y)


def _conv_long_kernel(g_ref, hist_ref, dw_ref, dwb_ref, ln_g_ref, ln_b_ref, o_ref, sh_ref):
    tl = g_ref.shape[1]
    first = pl.program_id(1) == 0
    hist = jnp.where(first, 0.0, hist_ref[0])
    sh_ref[0, 0:HIST_ROWS, :] = hist
    sh_ref[0, HIST_ROWS:HIST_ROWS + tl, :] = g_ref[0]
    n_sh = tl + HIST_ROWS - SUBLANES
    for b in range(1, SUBLANES):
        sh_ref[b, 0:n_sh, :] = sh_ref[0, b:b + n_sh, :]

    def chunk(ci, carry):
        r0 = pl.multiple_of(ci * CONV_ROWS, CONV_ROWS)
        groups = range(0, CONV_ROWS, SUBLANES)
        acc = [jnp.broadcast_to(dwb_ref[...], (SUBLANES, dwb_ref.shape[1])) for _ in groups]
        for k in range(CONV_WIDTH):
            off = HIST_ROWS - (CONV_WIDTH - 1) + k
            a, b = divmod(off, SUBLANES)
            w_k = dw_ref[k]
            acc = [ac + w_k * sh_ref[b, pl.ds(r0 + a * SUBLANES + g, SUBLANES), :] for ac, g in zip(acc, groups)]
        c = jnp.concatenate(acc, axis=0)
        o_ref[0, pl.ds(r0, CONV_ROWS), :] = _ln_swish(c, ln_g_ref, ln_b_ref).astype(o_ref.dtype)
        return carry

    lax.fori_loop(0, tl // CONV_ROWS, chunk, 0, unroll=8)


def _conv_long_call(g3, dw, dwb, ln_g, ln_b):
    bsz, t, d = g3.shape
    tl = min(TOKEN_TILE, t)
    per = tl // HIST_ROWS
    return pl.pallas_call(
        _conv_long_kernel,
        grid=(bsz, t // tl),
        in_specs=[pl.BlockSpec((1, tl, d), lambda b, j: (b, j, 0)),
                  pl.BlockSpec((1, HIST_ROWS, d), lambda b, j: (b, jnp.maximum(j * per - 1, 0), 0)),
                  _resident((HIST_ROWS, SUBLANES, d)), _resident((1, d)), _resident((1, d)), _resident((1, d))],
        out_specs=pl.BlockSpec((1, tl, d), lambda b, j: (b, j, 0)),
        out_shape=jax.ShapeDtypeStruct((bsz, t, d), BF16),
        scratch_shapes=[pltpu.VMEM((SUBLANES, tl + HIST_ROWS, d), F32)],
        compiler_params=_params(2),
        name="conv_long",
    )(g3, g3, dw, dwb, ln_g, ln_b)


def _short_front_kernel(x_ref, st_ref, g0_ref, pw1_ref, pw1b_ref, dw_ref, dwb_ref, ln_g_ref, ln_b_ref,
                        pw2_ref, pw2b_ref, gm_ref, w1_ref, w2_ref, gkv_ref, gq_ref, wk_ref, wv_ref, wq_ref,
                        cs_ref, h_ref, k_ref, v_ref, q_ref, ext_ref):
    n, d = x_ref.shape
    s = st_ref.shape[0]
    t = n // s
    x = x_ref[...]
    g = _glu(x, g0_ref, pw1_ref, pw1b_ref)
    ext_ref[:, 0:HIST_ROWS, :] = st_ref[...]
    ext_ref[:, HIST_ROWS:HIST_ROWS + t, :] = g.reshape(s, t, d)
    n_hist = cs_ref.shape[1]
    cs_ref[...] = ext_ref[:, HIST_ROWS + t - n_hist:HIST_ROWS + t, :]
    acc = jnp.broadcast_to(dwb_ref[...].reshape(1, 1, d), (s, t, d))
    for k in range(CONV_WIDTH):
        off = HIST_ROWS - (CONV_WIDTH - 1) + k
        acc = acc + dw_ref[k:k + 1, :].reshape(1, 1, d) * ext_ref[:, off:off + t, :]
    c = _ln_swish(acc.reshape(n, d), ln_g_ref, ln_b_ref).astype(BF16)
    h = x + jnp.dot(c, pw2_ref[...], preferred_element_type=F32) + pw2b_ref[...]
    h = _mlp(h, gm_ref, w1_ref, w2_ref)
    h_ref[...] = h
    k_ref[...], v_ref[...], q_ref[...] = _project_qkv(h, gkv_ref, gq_ref, wk_ref, wv_ref, wq_ref)


def _short_front_call(x2, st3, *params):
    n, d = x2.shape
    s = st3.shape[0]
    out = lambda dt: jax.ShapeDtypeStruct((n, d), dt)
    return pl.pallas_call(
        _short_front_kernel,
        grid=(1,),
        in_specs=[_spec_of(a) for a in (x2, st3) + params],
        out_specs=[pl.BlockSpec((s, CONV_WIDTH - 1, d), lambda i: (0, 0, 0))]
                  + [pl.BlockSpec((n, d), lambda i: (0, 0))] * 4,
        out_shape=[jax.ShapeDtypeStruct((s, CONV_WIDTH - 1, d), F32), out(F32), out(F32), out(F32), out(BF16)],
        scratch_shapes=[pltpu.VMEM((s, HIST_ROWS + n // s, d), F32)],
        compiler_params=_params(1),
        name="short_front",
    )(x2, st3, *[_array_of(a) for a in params])


def _pw2_mlp_kernel(c_ref, x_ref, pw2_ref, pw2b_ref, gm_ref, w1_ref, w2_ref, o_ref):
    h = x_ref[...] + jnp.dot(c_ref[...], pw2_ref[...], preferred_element_type=F32) + pw2b_ref[...]
    o_ref[...] = _mlp(h, gm_ref, w1_ref, w2_ref)


def _pw2_mlp_call(c2, x2, pw2, pw2b, gm, w1, w2):
    n, d = x2.shape
    tm = min(WIDE_TILE, n)
    row = lambda i: (i, 0)
    return pl.pallas_call(
        _pw2_mlp_kernel,
        grid=(n // tm,),
        in_specs=[pl.BlockSpec((tm, d), row), pl.BlockSpec((tm, d), row),
                  _resident((d, d)), _resident((1, d)), _resident((1, d)),
                  _spec_of(w1), _spec_of(w2)],
        out_specs=pl.BlockSpec((tm, d), row),
        out_shape=jax.ShapeDtypeStruct((n, d), F32),
        compiler_params=_params(1),
        name="pw2_mlp",
    )(c2, x2, pw2, pw2b, gm, _array_of(w1), _array_of(w2))


def _project_qkv(h, gkv_ref, gq_ref, wk_ref, wv_ref, wq_ref):
    y = _rms_unit(h)
    kvn = (y * gkv_ref[...]).astype(BF16)
    hn = (y * gq_ref[...]).astype(BF16)
    k = jnp.dot(kvn, wk_ref[...], preferred_element_type=F32)
    v = jnp.dot(kvn, wv_ref[...], preferred_element_type=F32)
    q = jnp.dot(hn, wq_ref[...], preferred_element_type=F32)
    return k, v, (q * Q_SCALE).astype(BF16)


def _qkv_kernel(h_ref, gkv_ref, gq_ref, wk_ref, wv_ref, wq_ref,
                k_ref, v_ref, q_ref, kb_ref, va_ref):
    tm, d = h_ref.shape
    k, v, q = _project_qkv(h_ref[...], gkv_ref, gq_ref, wk_ref, wv_ref, wq_ref)
    k_ref[...] = k
    v_ref[...] = v
    q_ref[...] = q
    kb_ref[...] = k.astype(BF16)
    ones_col = jnp.ones((tm, V_HEAD_DIM), BF16)
    vb = v.astype(BF16)
    for h in range(N_HEADS):
        va_ref[:, 2 * h * V_HEAD_DIM:(2 * h + 1) * V_HEAD_DIM] = vb[:, h * V_HEAD_DIM:(h + 1) * V_HEAD_DIM]
        va_ref[:, (2 * h + 1) * V_HEAD_DIM:(2 * h + 2) * V_HEAD_DIM] = ones_col


def _qkv_call(h2, gkv, gq, wk, wv, wq):
    n, d = h2.shape
    tm = min(TOKEN_TILE, n)
    row = lambda i: (i, 0)
    return pl.pallas_call(
        _qkv_kernel,
        grid=(n // tm,),
        in_specs=[pl.BlockSpec((tm, d), row), _resident((1, d)), _resident((1, d)),
                  _resident((d, d)), _resident((d, d)), _resident((d, d))],
        out_specs=[pl.BlockSpec((tm, d), row), pl.BlockSpec((tm, d), row), pl.BlockSpec((tm, d), row),
                   pl.BlockSpec((tm, d), row),
                   pl.BlockSpec((tm, 2 * d), row)],
        out_shape=[jax.ShapeDtypeStruct((n, d), F32), jax.ShapeDtypeStruct((n, d), F32),
                   jax.ShapeDtypeStruct((n, d), BF16), jax.ShapeDtypeStruct((n, d), BF16),
                   jax.ShapeDtypeStruct((n, 2 * d), BF16)],
        compiler_params=_params(1),
        name="qkv",
    )(h2, gkv, gq, wk, wv, wq)


def _lambda(lam_ref):
    lv = lam_ref[...]
    s1 = jnp.sum(lv[0:1, :] * lv[1:2, :], axis=-1, keepdims=True)
    s2 = jnp.sum(lv[2:3, :] * lv[3:4, :], axis=-1, keepdims=True)
    return jnp.exp(s1) - jnp.exp(s2) + LAM_INIT


def _sub_norm(o, subg_ref):
    return (_rms_unit(o) * subg_ref[...]) * (1.0 - LAM_INIT)


def _rel_bucket(n):
    max_exact = N_BUCKETS // 2
    nf = jnp.maximum(n, 1).astype(F32)
    large = max_exact + (jnp.log(nf / max_exact) / math.log(MAX_DISTANCE / max_exact)
                         * (N_BUCKETS - max_exact)).astype(jnp.int32)
    large = jnp.minimum(large, N_BUCKETS - 1)
    return jnp.where(n < max_exact, n, large)


def _shifted_bias(rel_table, dist, valid):
    bucket = _rel_bucket(jnp.maximum(dist, 0))[None]
    lead = (N_HEADS,) + (1,) * dist.ndim
    vals = jnp.zeros((N_HEADS,) + dist.shape, F32)
    for n in range(N_BUCKETS - 1):
        vals = jnp.where(bucket == n, (rel_table[n] - rel_table[N_BUCKETS - 1]).reshape(lead), vals)
    return jnp.where(valid[None], vals * LOG2E, NEG_INF)


def _attn_kernel(q_ref, k_ref, va_ref, bias_ref, lam_ref, subg_ref, o_ref, qz_ref, acc_ref, m_ref):
    blk = q_ref.shape[1]
    n_hd = q_ref.shape[2] // V_HEAD_DIM
    i = pl.program_id(2)
    for hd in range(n_hd):
        q = q_ref[0, :, hd * V_HEAD_DIM:(hd + 1) * V_HEAD_DIM]
        first_half = lax.broadcasted_iota(jnp.int32, q.shape, 1) < HEAD_DIM
        zero = jnp.zeros_like(q)
        qz_ref[2 * hd * blk:(2 * hd + 1) * blk, :] = jnp.where(first_half, q, zero)
        qz_ref[(2 * hd + 1) * blk:(2 * hd + 2) * blk, :] = jnp.where(first_half, zero, q)
    acc_ref[...] = jnp.zeros_like(acc_ref)
    m_ref[...] = jnp.full_like(m_ref, NEG_INF)
    reps = blk // LANES

    nt = (((1,), (1,)), ((), ()))
    chains = [(hd, slice((2 * hd + c) * blk, (2 * hd + c + 1) * blk)) for hd in range(n_hd) for c in range(2)]

    def scores(j):
        r0 = pl.multiple_of(j * blk, blk)
        return [lax.dot_general(qz_ref[rows, :], k_ref[0, pl.ds(r0, blk), hd * V_HEAD_DIM:(hd + 1) * V_HEAD_DIM],
                                nt, preferred_element_type=F32) for hd, rows in chains]

    sub = bias_ref.shape[2]
    nb = blk // sub

    def add_bias(sc, kind, hd):
        if kind == "prev":
            top = jnp.concatenate([sc[:sub, :blk - sub], sc[:sub, blk - sub:] + bias_ref[hd, 1]], axis=1)
            return jnp.concatenate([top, sc[sub:, :]], axis=0) if nb > 1 else top
        rows = []
        for r in range(nb):
            tiles = [bias_ref[hd, 0] if c == r else bias_ref[hd, 1] if c == r - 1 else
                     jnp.full((sub, sub), 0.0 if c < r else NEG_INF, F32) for c in range(nb)]
            rows.append(jnp.concatenate(tiles, axis=1))
        return sc + jnp.concatenate(rows, axis=0)

    def accumulate(j, s, bias):
        r0 = pl.multiple_of(j * blk, blk)
        if bias is not None:
            s = [add_bias(sc, bias, hd) for sc, (hd, _) in zip(s, chains)]
        m_prev = [m_ref[rows, :] for _, rows in chains]
        m_new = [jnp.maximum(mp, jnp.max(sc, axis=1, keepdims=True)) for mp, sc in zip(m_prev, s)]
        alpha = [jnp.exp2(mp - mn) for mp, mn in zip(m_prev, m_new)]
        p = [jnp.exp2(sc - jnp.concatenate([mn] * reps, axis=1)).astype(BF16) for sc, mn in zip(s, m_new)]
        pv = [jnp.dot(pc, va_ref[0, pl.ds(r0, blk), 2 * hd * V_HEAD_DIM:(2 * hd + 2) * V_HEAD_DIM],
                      preferred_element_type=F32) for pc, (hd, _) in zip(p, chains)]
        for (_, rows), al, pvc, mn in zip(chains, alpha, pv, m_new):
            acc_ref[rows, :] = jnp.concatenate([al, al], axis=1) * acc_ref[rows, :] + pvc
            m_ref[rows, :] = mn

    def process(blocks):
        s_all = [scores(j) for j, _ in blocks]
        for (j, bias), s in zip(blocks, s_all):
            accumulate(j, s, bias)

    n_far = jnp.maximum(i - 1, 0)

    def far_pair(t, carry):
        process([(2 * t, None), (2 * t + 1, None)])
        return carry

    lax.fori_loop(0, n_far // 2, far_pair, 0)
    odd = n_far % 2 == 1
    diag, prev = "diag", "prev"

    @pl.when(i == 0)
    def _():
        process([(i, diag)])

    @pl.when(jnp.logical_and(i > 0, jnp.logical_not(odd)))
    def _():
        process([(i - 1, prev), (i, diag)])

    @pl.when(odd)
    def _():
        process([(i - 2, None), (i - 1, prev), (i, diag)])

    lam = _lambda(lam_ref)
    for hd in range(n_hd):
        a1 = acc_ref[2 * hd * blk:(2 * hd + 1) * blk, :]
        a2 = acc_ref[(2 * hd + 1) * blk:(2 * hd + 2) * blk, :]
        o = a1[:, :V_HEAD_DIM] * (1.0 / a1[:, V_HEAD_DIM:]) - a2[:, :V_HEAD_DIM] * (lam / a2[:, V_HEAD_DIM:])
        o_ref[0, :, hd * V_HEAD_DIM:(hd + 1) * V_HEAD_DIM] = _sub_norm(o, subg_ref).astype(o_ref.dtype)


def _attn_call(q3, k3, va3, bias, lamv, subg):
    bsz, t, d = q3.shape
    blk = min(ATTN_BLOCK, t)
    n_hd = ATTN_HEADS
    rows = 2 * n_hd * blk
    return pl.pallas_call(
        _attn_kernel,
        grid=(bsz, N_HEADS // n_hd, t // blk),
        in_specs=[pl.BlockSpec((1, blk, n_hd * V_HEAD_DIM), lambda b, h, i: (b, i, h)),
                  pl.BlockSpec((1, t, n_hd * V_HEAD_DIM), lambda b, h, i: (b, 0, h)),
                  pl.BlockSpec((1, t, 2 * n_hd * V_HEAD_DIM), lambda b, h, i: (b, 0, h)),
                  pl.BlockSpec((n_hd, 2, MAX_DISTANCE, MAX_DISTANCE), lambda b, h, i: (h, 0, 0, 0)),
                  _resident((SUBLANES, LANES)), _resident((1, V_HEAD_DIM))],
        out_specs=pl.BlockSpec((1, blk, n_hd * V_HEAD_DIM), lambda b, h, i: (b, i, h)),
        out_shape=jax.ShapeDtypeStruct((bsz, t, d), BF16),
        scratch_shapes=[pltpu.VMEM((rows, V_HEAD_DIM), BF16),
                        pltpu.VMEM((rows, 2 * V_HEAD_DIM), F32),
                        pltpu.VMEM((rows, LANES), F32)],
        compiler_params=_params(3),
        name="attn_prompt",
    )(q3, k3, va3, bias, lamv, subg)


def _decode_kernel(pt_ref, q_ref, *refs, n_pages_step):
    g_n = n_pages_step
    k_refs = refs[:g_n]
    v_refs = refs[g_n:2 * g_n]
    (knew_ref, vnew_ref, blast_ref, bnew_ref, lam_ref, subg_ref,
     o_ref, qbd_ref, acc_ref, m_ref, l_ref) = refs[2 * g_n:]
    st = pl.program_id(1)
    last = pl.num_programs(1) - 1
    tq = q_ref.shape[1]
    cols, d = qbd_ref.shape
    page = blast_ref.shape[0]

    @pl.when(st == 0)
    def _():
        qt = jnp.concatenate([q_ref[0]] * (cols // tq), axis=0)
        row = lax.broadcasted_iota(jnp.int32, (cols, d), 0)
        lane = lax.broadcasted_iota(jnp.int32, (cols, d), 1)
        qbd_ref[...] = jnp.where(row // tq == lane // HEAD_DIM, qt, jnp.zeros_like(qt))
        acc_ref[...] = jnp.zeros_like(acc_ref)
        m_ref[...] = jnp.full_like(m_ref, NEG_INF)
        l_ref[...] = jnp.zeros_like(l_ref)

    def slab(ref):
        return jnp.concatenate([ref[0, pl.ds(h, page, stride=N_HEADS), :] for h in range(N_HEADS)],
                               axis=1).astype(BF16)

    def per_row(v):
        return jnp.broadcast_to(v, (cols, cols)).T

    nt = (((1,), (1,)), ((), ()))
    tn = (((0,), (0,)), ((), ()))

    def scores(k):
        return lax.dot_general(k, qbd_ref[...], nt, preferred_element_type=F32)

    def update(s, v):
        m_prev = m_ref[...]
        m_new = jnp.maximum(m_prev, jnp.max(s, axis=0, keepdims=True))
        alpha = jnp.exp2(m_prev - m_new)
        p = jnp.exp2(s - m_new)
        l_ref[...] = alpha * l_ref[...] + jnp.sum(p, axis=0, keepdims=True)
        pv = lax.dot_general(p.astype(BF16), v, tn, preferred_element_type=F32)
        acc_ref[...] = jnp.concatenate([per_row(alpha)] * (d // cols), axis=1) * acc_ref[...] + pv
        m_ref[...] = m_new

    sub = math.gcd(DECODE_GROUP, g_n)
    groups = [slice(a, a + sub) for a in range(0, g_n, sub)]
    s_all = [scores(jnp.concatenate([slab(r) for r in k_refs[gs]], axis=0)) for gs in groups]
    is_last = (st == last).astype(F32)
    s_all[-1] = s_all[-1] + jnp.concatenate(
        [jnp.zeros(((sub - 1) * page, cols), F32), is_last * blast_ref[...]], axis=0)
    for gs, s in zip(groups, s_all):
        update(s, jnp.concatenate([slab(r) for r in v_refs[gs]], axis=0))

    def new_slab(ref):
        return jnp.concatenate([ref[0], jnp.zeros((page - tq, d), F32)], axis=0).astype(BF16)

    @pl.when(st == last)
    def _():
        update(scores(new_slab(knew_ref)) + bnew_ref[...], new_slab(vnew_ref))
        lam = _lambda(lam_ref)
        o_all = acc_ref[...] / jnp.concatenate([per_row(l_ref[...])] * (d // cols), axis=1)
        for h in range(N_HEADS):
            blk = o_all[2 * h * tq:(2 * h + 2) * tq, h * V_HEAD_DIM:(h + 1) * V_HEAD_DIM]
            o = blk[:tq] - lam * blk[tq:]
            o_ref[0, :, h * V_HEAD_DIM:(h + 1) * V_HEAD_DIM] = _sub_norm(o, subg_ref).astype(o_ref.dtype)


def _decode_call(page_table, q3, cache_k3, cache_v3, knew, vnew, blast, bnew, lamv, subg):
    db, n_pages = page_table.shape
    _, tq, d = q3.shape
    prow = cache_k3.shape[1]
    page = prow // N_HEADS
    g_n = math.gcd(PAGES_PER_STEP, n_pages)
    cols = N_HEADS * 2 * tq

    def page_spec(g):
        return pl.BlockSpec((1, prow, V_HEAD_DIM), lambda b, s, pt: (pt[b, s * g_n + g], 0, 0))

    per_b = lambda b, s, pt: (b, 0, 0)
    const2 = lambda b, s, pt: (0, 0)
    tile = pl.BlockSpec((page, cols), const2)
    grid_spec = pltpu.PrefetchScalarGridSpec(
        num_scalar_prefetch=1,
        grid=(db, n_pages // g_n),
        in_specs=([pl.BlockSpec((1, tq, d), per_b)]
                  + [page_spec(g) for g in range(g_n)] + [page_spec(g) for g in range(g_n)]
                  + [pl.BlockSpec((1, tq, d), per_b), pl.BlockSpec((1, tq, d), per_b),
                     tile, tile,
                     pl.BlockSpec((SUBLANES, LANES), const2), pl.BlockSpec((1, V_HEAD_DIM), const2)]),
        out_specs=pl.BlockSpec((1, tq, d), per_b),
        scratch_shapes=[pltpu.VMEM((cols, d), BF16), pltpu.VMEM((cols, d), F32),
                        pltpu.VMEM((1, cols), F32), pltpu.VMEM((1, cols), F32)],
    )
    return pl.pallas_call(
        functools.partial(_decode_kernel, n_pages_step=g_n),
        grid_spec=grid_spec,
        out_shape=jax.ShapeDtypeStruct((db, tq, d), BF16),
        compiler_params=_params(2),
        name="attn_decode",
    )(page_table, q3, *([cache_k3] * g_n), *([cache_v3] * g_n), knew, vnew, blast, bnew, lamv, subg)


def _post_kernel(a_ref, h_ref, wo_ref, gm_ref, w1_ref, w2_ref, gf_ref, o_ref):
    h = h_ref[...] + jnp.dot(a_ref[...], wo_ref[...], preferred_element_type=F32)
    h = _mlp(h, gm_ref, w1_ref, w2_ref)
    o_ref[...] = _rms_unit(h) * gf_ref[...]


def _post_call(a2, h2, wo, gm, w1, w2, gf):
    n, d = h2.shape
    tm = min(WIDE_TILE, n)
    row = lambda i: (i, 0)
    return pl.pallas_call(
        _post_kernel,
        grid=(n // tm,),
        in_specs=[pl.BlockSpec((tm, d), row), pl.BlockSpec((tm, d), row),
                  _resident((d, d)), _resident((1, d)), _spec_of(w1), _spec_of(w2),
                  _resident((1, d))],
        out_specs=pl.BlockSpec((tm, d), row),
        out_shape=jax.ShapeDtypeStruct((n, d), F32),
        compiler_params=_params(1),
        name="post",
    )(a2, h2, wo, gm, _array_of(w1), _array_of(w2), gf)


def _row(v):
    return v.reshape(1, -1).astype(F32)


def kernel(x_prompt, x_sample, state_conv, cache_k, cache_v, page_table, norm_mix_g, norm_mlp_g, conv_pw1_w, conv_pw1_b, conv_dw_w, conv_dw_b, conv_ln_g, conv_ln_b, conv_pw2_w, conv_pw2_b, kv_norm_g, w_k, w_v, w_q, lambda_q1, lambda_k1, lambda_q2, lambda_k2, subln_g, w_o, rel_bias_table, mlp_w1, mlp_w2, final_norm_g):
    bsz, seq, d = x_prompt.shape
    db, dseq, _ = x_sample.shape
    n_pool, page = cache_k.shape[0], cache_k.shape[1]
    n_pages = page_table.shape[1]
    past_len = n_pages * page
    hist = CONV_WIDTH - 1

    pw1 = conv_pw1_w[0].astype(BF16)
    pw1b = _row(conv_pw1_b[0])
    dw = jnp.pad(conv_dw_w[0].astype(F32), ((0, HIST_ROWS - CONV_WIDTH), (0, 0)))
    dwb, ln_g, ln_b = _row(conv_dw_b[0]), _row(conv_ln_g[0]), _row(conv_ln_b[0])
    pw2 = conv_pw2_w[0].astype(BF16)
    pw2b = _row(conv_pw2_b[0])
    w1s, w2s = mlp_w1.astype(BF16), mlp_w2.astype(BF16)
    w1 = [_Layer(w1s, l) for l in range(mlp_w1.shape[0])]
    w2 = [_Layer(w2s, l) for l in range(mlp_w2.shape[0])]
    wk, wv, wq, wo = w_k.astype(BF16), w_v.astype(BF16), w_q[0].astype(BF16), w_o[0].astype(BF16)
    lamv = jnp.pad(jnp.stack([lambda_q1[0], lambda_k1[0], lambda_q2[0], lambda_k2[0]]).astype(F32),
                   ((0, SUBLANES - 4), (0, LANES - HEAD_DIM)))
    subg = _row(subln_g[0])
    rel = rel_bias_table.astype(F32)

    g_mix0, g_mix1, g_mlp0, g_kv = _row(norm_mix_g[0]), _row(norm_mix_g[1]), _row(norm_mlp_g[0]), _row(kv_norm_g)

    def back(a2, h2):
        return _post_call(a2, h2, wo, _row(norm_mlp_g[1]), w1[1], w2[1], _row(final_norm_g))

    x_p = x_prompt.reshape(bsz * seq, d)
    g_p = _glu_call(x_p, g_mix0, pw1, pw1b)
    dw_rows = jnp.broadcast_to(dw[:, None, :], (HIST_ROWS, SUBLANES, d))
    c_p = _conv_long_call(g_p.reshape(bsz, seq, d), dw_rows, dwb, ln_g, ln_b).reshape(bsz * seq, d)
    h_p = _pw2_mlp_call(c_p, x_p, pw2, pw2b, g_mlp0, w1[0], w2[0])
    k_p, v_p, q_p, kb_p, va_p = _qkv_call(h_p, g_kv, g_mix1, wk, wv, wq)
    assert min(ATTN_BLOCK, seq) % MAX_DISTANCE == 0
    r = jnp.arange(MAX_DISTANCE, dtype=jnp.int32)[:, None]
    c = jnp.arange(MAX_DISTANCE, dtype=jnp.int32)[None, :]
    bias_p = jnp.stack([_shifted_bias(rel, r - c, r >= c),
                        _shifted_bias(rel, MAX_DISTANCE + r - c, jnp.full((MAX_DISTANCE,) * 2, True))], axis=1)
    a_p = _attn_call(q_p.reshape(bsz, seq, d), kb_p.reshape(bsz, seq, d),
                     va_p.reshape(bsz, seq, 2 * d), bias_p, lamv, subg)
    y_prompt = back(a_p.reshape(bsz * seq, d), h_p).reshape(bsz, seq, d)
    conv_state_prompt = g_p.reshape(bsz, seq, d)[:, seq - hist:][None]

    st_s = jnp.pad(state_conv[0].astype(F32), ((0, 0), (HIST_ROWS - hist, 0), (0, 0)))
    cs_s, h_s, k_s, v_s, q_s = _short_front_call(
        x_sample.reshape(db * dseq, d), st_s, g_mix0, pw1, pw1b, dw, dwb, ln_g, ln_b,
        pw2, pw2b, g_mlp0, w1[0], w2[0], g_kv, g_mix1, wk, wv, wq)
    prow = page * N_HEADS
    n_cols = 2 * N_HEADS * dseq
    assert n_cols == LANES and dseq <= page
    col_q = jnp.tile(jnp.arange(dseq, dtype=jnp.int32), 2 * N_HEADS)[None, :]
    key = jnp.arange(page, dtype=jnp.int32)[:, None]
    pick = lambda b4: jnp.concatenate([b4[h, :, 2 * h * dseq:2 * (h + 1) * dseq] for h in range(N_HEADS)], axis=1)
    blast = pick(_shifted_bias(rel, page + col_q - key, jnp.full((page, n_cols), True)))
    bnew = pick(_shifted_bias(rel, col_q - key, (key <= col_q) & (key < dseq)))
    a_s = _decode_call(page_table.astype(jnp.int32), q_s.reshape(db, dseq, d),
                       cache_k.reshape(n_pool, prow, V_HEAD_DIM), cache_v.reshape(n_pool, prow, V_HEAD_DIM),
                       k_s.reshape(db, dseq, d), v_s.reshape(db, dseq, d), blast, bnew, lamv, subg)
    y_sample = back(a_s.reshape(db * dseq, d), h_s).reshape(db, dseq, d)
    conv_state_sample = cs_s[None]

    kv_shape = (N_HEADS, V_HEAD_DIM)
    return (y_prompt, y_sample, conv_state_prompt, conv_state_sample,
            k_p.reshape(bsz, seq, *kv_shape), v_p.reshape(bsz, seq, *kv_shape),
            k_s.reshape(db, dseq, *kv_shape), v_s.reshape(db, dseq, *kv_shape))
```

```python
import functools
import math

import jax
import jax.numpy as jnp
from jax import lax
from jax.experimental import pallas as pl
from jax.experimental.pallas import tpu as pltpu

F32 = jnp.float32
BF16 = jnp.bfloat16

N_HEADS = 8
HEAD_DIM = 64
V_HEAD_DIM = 2 * HEAD_DIM
CONV_WIDTH = 31
N_BUCKETS = 32
MAX_DISTANCE = 128
NORM_EPS = 1e-6
NEG_INF = -1e30
LAM_INIT = 0.8 - 0.6 * math.exp(-0.3 * 1)
LOG2E = math.log2(math.e)
Q_SCALE = HEAD_DIM ** -0.5 * LOG2E

SUBLANES = 8
LANES = 128
HIST_ROWS = 32
VMEM_LIMIT = 56 * 1024 * 1024

TOKEN_TILE = 512
WIDE_TILE = 1024
FF_CHUNK = 512
CONV_ROWS = 32
ATTN_BLOCK = 512
ATTN_HEADS = 2
PAGES_PER_STEP = 16
DECODE_GROUP = 4


def _params(n_axes, carried=0):
    semantics = ("parallel",) * (n_axes - carried) + ("arbitrary",) * carried
    return pltpu.CompilerParams(dimension_semantics=semantics, vmem_limit_bytes=VMEM_LIMIT)


def _resident(shape):
    nd = len(shape)
    return pl.BlockSpec(shape, lambda *_: (0,) * nd, pipeline_mode=pl.Buffered(1))


class _Layer:
    def __init__(self, stack, index):
        self.stack, self.index, self.shape = stack, index, stack.shape[1:]

    def spec(self):
        idx = (self.index,) + (0,) * len(self.shape)
        return pl.BlockSpec((None,) + self.shape, lambda *_: idx, pipeline_mode=pl.Buffered(1))


def _spec_of(a):
    return a.spec() if isinstance(a, _Layer) else _resident(a.shape)


def _array_of(a):
    return a.stack if isinstance(a, _Layer) else a


def _rms_unit(x):
    return x * lax.rsqrt(jnp.mean(x * x, axis=-1, keepdims=True) + NORM_EPS)


def _mlp(h, g_ref, w1_ref, w2_ref):
    hn = (_rms_unit(h) * g_ref[...]).astype(BF16)
    d_ff = w1_ref.shape[1]
    acc = h
    for c in range(d_ff // FF_CHUNK):
        u = jnp.dot(hn, w1_ref[:, c * FF_CHUNK:(c + 1) * FF_CHUNK], preferred_element_type=F32)
        u = jnp.maximum(u, 0.0)
        acc = acc + jnp.dot((u * u).astype(BF16), w2_ref[c * FF_CHUNK:(c + 1) * FF_CHUNK, :],
                            preferred_element_type=F32)
    return acc


def _glu(x, g_ref, w_ref, b_ref):
    d = x.shape[1]
    hn = (_rms_unit(x) * g_ref[...]).astype(BF16)
    u = jnp.dot(hn, w_ref[...], preferred_element_type=F32) + b_ref[...]
    return u[:, :d] * jax.nn.sigmoid(u[:, d:])


def _glu_kernel(x_ref, g_ref, w_ref, b_ref, o_ref):
    o_ref[...] = _glu(x_ref[...], g_ref, w_ref, b_ref)


def _glu_call(x2, g, w, b):
    n, d = x2.shape
    tm = min(WIDE_TILE, n)
    return pl.pallas_call(
        _glu_kernel,
        grid=(n // tm,),
        in_specs=[pl.BlockSpec((tm, d), lambda i: (i, 0)),
                  _resident((1, d)), _resident((d, 2 * d)), _resident((1, 2 * d))],
        out_specs=pl.BlockSpec((tm, d), lambda i: (i, 0)),
        out_shape=jax.ShapeDtypeStruct((n, d), F32),
        compiler_params=_params(1),
        name="glu",
    )(x2, g, w, b)


def _ln_swish(c, ln_g_ref, ln_b_ref):
    mu = jnp.mean(c, axis=-1, keepdims=True)
    cc = c - mu
    var = jnp.mean(cc * cc, axis=-1, keepdims=True)
    y = cc * lax.rsqrt(var + NORM_EPS) * ln_g_ref[...] + ln_b_ref[...]
    return y * jax.nn.sigmoid(y)


def _conv_long_kernel(g_ref, hist_ref, dw_ref, dwb_ref, ln_g_ref, ln_b_ref, o_ref, sh_ref):
    tl = g_ref.shape[1]
    first = pl.program_id(1) == 0
    hist = jnp.where(first, 0.0, hist_ref[0])
    sh_ref[0, 0:HIST_ROWS, :] = hist
    sh_ref[0, HIST_ROWS:HIST_ROWS + tl, :] = g_ref[0]
    n_sh = tl + HIST_ROWS - SUBLANES
    for b in range(1, SUBLANES):
        sh_ref[b, 0:n_sh, :] = sh_ref[0, b:b + n_sh, :]

    def chunk(ci, carry):
        r0 = pl.multiple_of(ci * CONV_ROWS, CONV_ROWS)
        groups = range(0, CONV_ROWS, SUBLANES)
        acc = [jnp.broadcast_to(dwb_ref[...], (SUBLANES, dwb_ref.shape[1])) for _ in groups]
        for k in range(CONV_WIDTH):
            off = HIST_ROWS - (CONV_WIDTH - 1) + k
            a, b = divmod(off, SUBLANES)
            w_k = dw_ref[k]
            acc = [ac + w_k * sh_ref[b, pl.ds(r0 + a * SUBLANES + g, SUBLANES), :] for ac, g in zip(acc, groups)]
        c = jnp.concatenate(acc, axis=0)
        o_ref[0, pl.ds(r0, CONV_ROWS), :] = _ln_swish(c, ln_g_ref, ln_b_ref).astype(o_ref.dtype)
        return carry

    lax.fori_loop(0, tl // CONV_ROWS, chunk, 0, unroll=8)


def _conv_long_call(g3, dw, dwb, ln_g, ln_b):
    bsz, t, d = g3.shape
    tl = min(TOKEN_TILE, t)
    per = tl // HIST_ROWS
    return pl.pallas_call(
        _conv_long_kernel,
        grid=(bsz, t // tl),
        in_specs=[pl.BlockSpec((1, tl, d), lambda b, j: (b, j, 0)),
                  pl.BlockSpec((1, HIST_ROWS, d), lambda b, j: (b, jnp.maximum(j * per - 1, 0), 0)),
                  _resident((HIST_ROWS, SUBLANES, d)), _resident((1, d)), _resident((1, d)), _resident((1, d))],
        out_specs=pl.BlockSpec((1, tl, d), lambda b, j: (b, j, 0)),
        out_shape=jax.ShapeDtypeStruct((bsz, t, d), BF16),
        scratch_shapes=[pltpu.VMEM((SUBLANES, tl + HIST_ROWS, d), F32)],
        compiler_params=_params(2),
        name="conv_long",
    )(g3, g3, dw, dwb, ln_g, ln_b)


def _short_front_kernel(x_ref, st_ref, g0_ref, pw1_ref, pw1b_ref, dw_ref, dwb_ref, ln_g_ref, ln_b_ref,
                        pw2_ref, pw2b_ref, gm_ref, w1_ref, w2_ref, gkv_ref, gq_ref, wk_ref, wv_ref, wq_ref,
                        cs_ref, h_ref, k_ref, v_ref, q_ref, ext_ref):
    n, d = x_ref.shape
    s = st_ref.shape[0]
    t = n // s
    x = x_ref[...]
    g = _glu(x, g0_ref, pw1_ref, pw1b_ref)
    ext_ref[:, 0:HIST_ROWS, :] = st_ref[...]
    ext_ref[:, HIST_ROWS:HIST_ROWS + t, :] = g.reshape(s, t, d)
    n_hist = cs_ref.shape[1]
    cs_ref[...] = ext_ref[:, HIST_ROWS + t - n_hist:HIST_ROWS + t, :]
    acc = jnp.broadcast_to(dwb_ref[...].reshape(1, 1, d), (s, t, d))
    for k in range(CONV_WIDTH):
        off = HIST_ROWS - (CONV_WIDTH - 1) + k
        acc = acc + dw_ref[k:k + 1, :].reshape(1, 1, d) * ext_ref[:, off:off + t, :]
    c = _ln_swish(acc.reshape(n, d), ln_g_ref, ln_b_ref).astype(BF16)
    h = x + jnp.dot(c, pw2_ref[...], preferred_element_type=F32) + pw2b_ref[...]
    h = _mlp(h, gm_ref, w1_ref, w2_ref)
    h_ref[...] = h
    k_ref[...], v_ref[...], q_ref[...] = _project_qkv(h, gkv_ref, gq_ref, wk_ref, wv_ref, wq_ref)


def _short_front_call(x2, st3, *params):
    n, d = x2.shape
    s = st3.shape[0]
    out = lambda dt: jax.ShapeDtypeStruct((n, d), dt)
    return pl.pallas_call(
        _short_front_kernel,
        grid=(1,),
        in_specs=[_spec_of(a) for a in (x2, st3) + params],
        out_specs=[pl.BlockSpec((s, CONV_WIDTH - 1, d), lambda i: (0, 0, 0))]
                  + [pl.BlockSpec((n, d), lambda i: (0, 0))] * 4,
        out_shape=[jax.ShapeDtypeStruct((s, CONV_WIDTH - 1, d), F32), out(F32), out(F32), out(F32), out(BF16)],
        scratch_shapes=[pltpu.VMEM((s, HIST_ROWS + n // s, d), F32)],
        compiler_params=_params(1),
        name="short_front",
    )(x2, st3, *[_array_of(a) for a in params])


def _pw2_mlp_kernel(c_ref, x_ref, pw2_ref, pw2b_ref, gm_ref, w1_ref, w2_ref, o_ref):
    h = x_ref[...] + jnp.dot(c_ref[...], pw2_ref[...], preferred_element_type=F32) + pw2b_ref[...]
    o_ref[...] = _mlp(h, gm_ref, w1_ref, w2_ref)


def _pw2_mlp_call(c2, x2, pw2, pw2b, gm, w1, w2):
    n, d = x2.shape
    tm = min(WIDE_TILE, n)
    row = lambda i: (i, 0)
    return pl.pallas_call(
        _pw2_mlp_kernel,
        grid=(n // tm,),
        in_specs=[pl.BlockSpec((tm, d), row), pl.BlockSpec((tm, d), row),
                  _resident((d, d)), _resident((1, d)), _resident((1, d)),
                  _spec_of(w1), _spec_of(w2)],
        out_specs=pl.BlockSpec((tm, d), row),
        out_shape=jax.ShapeDtypeStruct((n, d), F32),
        compiler_params=_params(1),
        name="pw2_mlp",
    )(c2, x2, pw2, pw2b, gm, _array_of(w1), _array_of(w2))


def _project_qkv(h, gkv_ref, gq_ref, wk_ref, wv_ref, wq_ref):
    y = _rms_unit(h)
    kvn = (y * gkv_ref[...]).astype(BF16)
    hn = (y * gq_ref[...]).astype(BF16)
    k = jnp.dot(kvn, wk_ref[...], preferred_element_type=F32)
    v = jnp.dot(kvn, wv_ref[...], preferred_element_type=F32)
    q = jnp.dot(hn, wq_ref[...], preferred_element_type=F32)
    return k, v, (q * Q_SCALE).astype(BF16)


def _qkv_kernel(h_ref, gkv_ref, gq_ref, wk_ref, wv_ref, wq_ref,
                k_ref, v_ref, q_ref, kb_ref, va_ref):
    tm, d = h_ref.shape
    k, v, q = _project_qkv(h_ref[...], gkv_ref, gq_ref, wk_ref, wv_ref, wq_ref)
    k_ref[...] = k
    v_ref[...] = v
    q_ref[...] = q
    kb_ref[...] = k.astype(BF16)
    ones_col = jnp.ones((tm, V_HEAD_DIM), BF16)
    vb = v.astype(BF16)
    for h in range(N_HEADS):
        va_ref[:, 2 * h * V_HEAD_DIM:(2 * h + 1) * V_HEAD_DIM] = vb[:, h * V_HEAD_DIM:(h + 1) * V_HEAD_DIM]
        va_ref[:, (2 * h + 1) * V_HEAD_DIM:(2 * h + 2) * V_HEAD_DIM] = ones_col


def _qkv_call(h2, gkv, gq, wk, wv, wq):
    n, d = h2.shape
    tm = min(TOKEN_TILE, n)
    row = lambda i: (i, 0)
    return pl.pallas_call(
        _qkv_kernel,
        grid=(n // tm,),
        in_specs=[pl.BlockSpec((tm, d), row), _resident((1, d)), _resident((1, d)),
                  _resident((d, d)), _resident((d, d)), _resident((d, d))],
        out_specs=[pl.BlockSpec((tm, d), row), pl.BlockSpec((tm, d), row), pl.BlockSpec((tm, d), row),
                   pl.BlockSpec((tm, d), row),
                   pl.BlockSpec((tm, 2 * d), row)],
        out_shape=[jax.ShapeDtypeStruct((n, d), F32), jax.ShapeDtypeStruct((n, d), F32),
                   jax.ShapeDtypeStruct((n, d), BF16), jax.ShapeDtypeStruct((n, d), BF16),
                   jax.ShapeDtypeStruct((n, 2 * d), BF16)],
        compiler_params=_params(1),
        name="qkv",
    )(h2, gkv, gq, wk, wv, wq)


def _lambda(lam_ref):
    lv = lam_ref[...]
    s1 = jnp.sum(lv[0:1, :] * lv[1:2, :], axis=-1, keepdims=True)
    s2 = jnp.sum(lv[2:3, :] * lv[3:4, :], axis=-1, keepdims=True)
    return jnp.exp(s1) - jnp.exp(s2) + LAM_INIT


def _sub_norm(o, subg_ref):
    return (_rms_unit(o) * subg_ref[...]) * (1.0 - LAM_INIT)


def _rel_bucket(n):
    max_exact = N_BUCKETS // 2
    nf = jnp.maximum(n, 1).astype(F32)
    large = max_exact + (jnp.log(nf / max_exact) / math.log(MAX_DISTANCE / max_exact)
                         * (N_BUCKETS - max_exact)).astype(jnp.int32)
    large = jnp.minimum(large, N_BUCKETS - 1)
    return jnp.where(n < max_exact, n, large)


def _shifted_bias(rel_table, dist, valid):
    bucket = _rel_bucket(jnp.maximum(dist, 0))[None]
    lead = (N_HEADS,) + (1,) * dist.ndim
    vals = jnp.zeros((N_HEADS,) + dist.shape, F32)
    for n in range(N_BUCKETS - 1):
        vals = jnp.where(bucket == n, (rel_table[n] - rel_table[N_BUCKETS - 1]).reshape(lead), vals)
    return jnp.where(valid[None], vals * LOG2E, NEG_INF)


def _attn_kernel(q_ref, k_ref, va_ref, bias_ref, lam_ref, subg_ref, o_ref, qz_ref, acc_ref, m_ref):
    blk = q_ref.shape[1]
    n_hd = q_ref.shape[2] // V_HEAD_DIM
    i = pl.program_id(2)
    for hd in range(n_hd):
        q = q_ref[0, :, hd * V_HEAD_DIM:(hd + 1) * V_HEAD_DIM]
        first_half = lax.broadcasted_iota(jnp.int32, q.shape, 1) < HEAD_DIM
        zero = jnp.zeros_like(q)
        qz_ref[2 * hd * blk:(2 * hd + 1) * blk, :] = jnp.where(first_half, q, zero)
        qz_ref[(2 * hd + 1) * blk:(2 * hd + 2) * blk, :] = jnp.where(first_half, zero, q)
    acc_ref[...] = jnp.zeros_like(acc_ref)
    m_ref[...] = jnp.full_like(m_ref, NEG_INF)
    reps = blk // LANES

    nt = (((1,), (1,)), ((), ()))
    chains = [(hd, slice((2 * hd + c) * blk, (2 * hd + c + 1) * blk)) for hd in range(n_hd) for c in range(2)]

    def scores(j):
        r0 = pl.multiple_of(j * blk, blk)
        return [lax.dot_general(qz_ref[rows, :], k_ref[0, pl.ds(r0, blk), hd * V_HEAD_DIM:(hd + 1) * V_HEAD_DIM],
                                nt, preferred_element_type=F32) for hd, rows in chains]

    sub = bias_ref.shape[2]
    nb = blk // sub

    def add_bias(sc, kind, hd):
        if kind == "prev":
            top = jnp.concatenate([sc[:sub, :blk - sub], sc[:sub, blk - sub:] + bias_ref[hd, 1]], axis=1)
            return jnp.concatenate([top, sc[sub:, :]], axis=0) if nb > 1 else top
        rows = []
        for r in range(nb):
            tiles = [bias_ref[hd, 0] if c == r else bias_ref[hd, 1] if c == r - 1 else
                     jnp.full((sub, sub), 0.0 if c < r else NEG_INF, F32) for c in range(nb)]
            rows.append(jnp.concatenate(tiles, axis=1))
        return sc + jnp.concatenate(rows, axis=0)

    def accumulate(j, s, bias):
        r0 = pl.multiple_of(j * blk, blk)
        if bias is not None:
            s = [add_bias(sc, bias, hd) for sc, (hd, _) in zip(s, chains)]
        m_prev = [m_ref[rows, :] for _, rows in chains]
        m_new = [jnp.maximum(mp, jnp.max(sc, axis=1, keepdims=True)) for mp, sc in zip(m_prev, s)]
        alpha = [jnp.exp2(mp - mn) for mp, mn in zip(m_prev, m_new)]
        p = [jnp.exp2(sc - jnp.concatenate([mn] * reps, axis=1)).astype(BF16) for sc, mn in zip(s, m_new)]
        pv = [jnp.dot(pc, va_ref[0, pl.ds(r0, blk), 2 * hd * V_HEAD_DIM:(2 * hd + 2) * V_HEAD_DIM],
                      preferred_element_type=F32) for pc, (hd, _) in zip(p, chains)]
        for (_, rows), al, pvc, mn in zip(chains, alpha, pv, m_new):
            acc_ref[rows, :] = jnp.concatenate([al, al], axis=1) * acc_ref[rows, :] + pvc
            m_ref[rows, :] = mn

    def process(blocks):
        s_all = [scores(j) for j, _ in blocks]
        for (j, bias), s in zip(blocks, s_all):
            accumulate(j, s, bias)

    n_far = jnp.maximum(i - 1, 0)

    def far_pair(t, carry):
        process([(2 * t, None), (2 * t + 1, None)])
        return carry

    lax.fori_loop(0, n_far // 2, far_pair, 0)
    odd = n_far % 2 == 1
    diag, prev = "diag", "prev"

    @pl.when(i == 0)
    def _():
        process([(i, diag)])

    @pl.when(jnp.logical_and(i > 0, jnp.logical_not(odd)))
    def _():
        process([(i - 1, prev), (i, diag)])

    @pl.when(odd)
    def _():
        process([(i - 2, None), (i - 1, prev), (i, diag)])

    lam = _lambda(lam_ref)
    for hd in range(n_hd):
        a1 = acc_ref[2 * hd * blk:(2 * hd + 1) * blk, :]
        a2 = acc_ref[(2 * hd + 1) * blk:(2 * hd + 2) * blk, :]
        o = a1[:, :V_HEAD_DIM] * (1.0 / a1[:, V_HEAD_DIM:]) - a2[:, :V_HEAD_DIM] * (lam / a2[:, V_HEAD_DIM:])
        o_ref[0, :, hd * V_HEAD_DIM:(hd + 1) * V_HEAD_DIM] = _sub_norm(o, subg_ref).astype(o_ref.dtype)


def _attn_call(q3, k3, va3, bias, lamv, subg):
    bsz, t, d = q3.shape
    blk = min(ATTN_BLOCK, t)
    n_hd = ATTN_HEADS
    rows = 2 * n_hd * blk
    return pl.pallas_call(
        _attn_kernel,
        grid=(bsz, N_HEADS // n_hd, t // blk),
        in_specs=[pl.BlockSpec((1, blk, n_hd * V_HEAD_DIM), lambda b, h, i: (b, i, h)),
                  pl.BlockSpec((1, t, n_hd * V_HEAD_DIM), lambda b, h, i: (b, 0, h)),
                  pl.BlockSpec((1, t, 2 * n_hd * V_HEAD_DIM), lambda b, h, i: (b, 0, h)),
                  pl.BlockSpec((n_hd, 2, MAX_DISTANCE, MAX_DISTANCE), lambda b, h, i: (h, 0, 0, 0)),
                  _resident((SUBLANES, LANES)), _resident((1, V_HEAD_DIM))],
        out_specs=pl.BlockSpec((1, blk, n_hd * V_HEAD_DIM), lambda b, h, i: (b, i, h)),
        out_shape=jax.ShapeDtypeStruct((bsz, t, d), BF16),
        scratch_shapes=[pltpu.VMEM((rows, V_HEAD_DIM), BF16),
                        pltpu.VMEM((rows, 2 * V_HEAD_DIM), F32),
                        pltpu.VMEM((rows, LANES), F32)],
        compiler_params=_params(3),
        name="attn_prompt",
    )(q3, k3, va3, bias, lamv, subg)


def _decode_kernel(pt_ref, q_ref, *refs, n_pages_step):
    g_n = n_pages_step
    k_refs = refs[:g_n]
    v_refs = refs[g_n:2 * g_n]
    (knew_ref, vnew_ref, blast_ref, bnew_ref, lam_ref, subg_ref,
     o_ref, qbd_ref, acc_ref, m_ref, l_ref) = refs[2 * g_n:]
    st = pl.program_id(1)
    last = pl.num_programs(1) - 1
    tq = q_ref.shape[1]
    cols, d = qbd_ref.shape
    page = blast_ref.shape[0]

    @pl.when(st == 0)
    def _():
        qt = jnp.concatenate([q_ref[0]] * (cols // tq), axis=0)
        row = lax.broadcasted_iota(jnp.int32, (cols, d), 0)
        lane = lax.broadcasted_iota(jnp.int32, (cols, d), 1)
        qbd_ref[...] = jnp.where(row // tq == lane // HEAD_DIM, qt, jnp.zeros_like(qt))
        acc_ref[...] = jnp.zeros_like(acc_ref)
        m_ref[...] = jnp.full_like(m_ref, NEG_INF)
        l_ref[...] = jnp.zeros_like(l_ref)

    def slab(ref):
        return jnp.concatenate([ref[0, pl.ds(h, page, stride=N_HEADS), :] for h in range(N_HEADS)],
                               axis=1).astype(BF16)

    def per_row(v):
        return jnp.broadcast_to(v, (cols, cols)).T

    nt = (((1,), (1,)), ((), ()))
    tn = (((0,), (0,)), ((), ()))

    def scores(k):
        return lax.dot_general(k, qbd_ref[...], nt, preferred_element_type=F32)

    def update(s, v):
        m_prev = m_ref[...]
        m_new = jnp.maximum(m_prev, jnp.max(s, axis=0, keepdims=True))
        alpha = jnp.exp2(m_prev - m_new)
        p = jnp.exp2(s - m_new)
        l_ref[...] = alpha * l_ref[...] + jnp.sum(p, axis=0, keepdims=True)
        pv = lax.dot_general(p.astype(BF16), v, tn, preferred_element_type=F32)
        acc_ref[...] = jnp.concatenate([per_row(alpha)] * (d // cols), axis=1) * acc_ref[...] + pv
        m_ref[...] = m_new

    sub = math.gcd(DECODE_GROUP, g_n)
    groups = [slice(a, a + sub) for a in range(0, g_n, sub)]
    s_all = [scores(jnp.concatenate([slab(r) for r in k_refs[gs]], axis=0)) for gs in groups]
    is_last = (st == last).astype(F32)
    s_all[-1] = s_all[-1] + jnp.concatenate(
        [jnp.zeros(((sub - 1) * page, cols), F32), is_last * blast_ref[...]], axis=0)
    for gs, s in zip(groups, s_all):
        update(s, jnp.concatenate([slab(r) for r in v_refs[gs]], axis=0))

    def new_slab(ref):
        return jnp.concatenate([ref[0], jnp.zeros((page - tq, d), F32)], axis=0).astype(BF16)

    @pl.when(st == last)
    def _():
        update(scores(new_slab(knew_ref)) + bnew_ref[...], new_slab(vnew_ref))
        lam = _lambda(lam_ref)
        o_all = acc_ref[...] / jnp.concatenate([per_row(l_ref[...])] * (d // cols), axis=1)
        for h in range(N_HEADS):
            blk = o_all[2 * h * tq:(2 * h + 2) * tq, h * V_HEAD_DIM:(h + 1) * V_HEAD_DIM]
            o = blk[:tq] - lam * blk[tq:]
            o_ref[0, :, h * V_HEAD_DIM:(h + 1) * V_HEAD_DIM] = _sub_norm(o, subg_ref).astype(o_ref.dtype)


def _decode_call(page_table, q3, cache_k3, cache_v3, knew, vnew, blast, bnew, lamv, subg):
    db, n_pages = page_table.shape
    _, tq, d = q3.shape
    prow = cache_k3.shape[1]
    page = prow // N_HEADS
    g_n = math.gcd(PAGES_PER_STEP, n_pages)
    cols = N_HEADS * 2 * tq

    def page_spec(g):
        return pl.BlockSpec((1, prow, V_HEAD_DIM), lambda b, s, pt: (pt[b, s * g_n + g], 0, 0))

    per_b = lambda b, s, pt: (b, 0, 0)
    const2 = lambda b, s, pt: (0, 0)
    tile = pl.BlockSpec((page, cols), const2)
    grid_spec = pltpu.PrefetchScalarGridSpec(
        num_scalar_prefetch=1,
        grid=(db, n_pages // g_n),
        in_specs=([pl.BlockSpec((1, tq, d), per_b)]
                  + [page_spec(g) for g in range(g_n)] + [page_spec(g) for g in range(g_n)]
                  + [pl.BlockSpec((1, tq, d), per_b), pl.BlockSpec((1, tq, d), per_b),
                     tile, tile,
                     pl.BlockSpec((SUBLANES, LANES), const2), pl.BlockSpec((1, V_HEAD_DIM), const2)]),
        out_specs=pl.BlockSpec((1, tq, d), per_b),
        scratch_shapes=[pltpu.VMEM((cols, d), BF16), pltpu.VMEM((cols, d), F32),
                        pltpu.VMEM((1, cols), F32), pltpu.VMEM((1, cols), F32)],
    )
    return pl.pallas_call(
        functools.partial(_decode_kernel, n_pages_step=g_n),
        grid_spec=grid_spec,
        out_shape=jax.ShapeDtypeStruct((db, tq, d), BF16),
        compiler_params=_params(2, carried=1),
        name="attn_decode",
    )(page_table, q3, *([cache_k3] * g_n), *([cache_v3] * g_n), knew, vnew, blast, bnew, lamv, subg)


def _post_kernel(a_ref, h_ref, wo_ref, gm_ref, w1_ref, w2_ref, gf_ref, o_ref):
    h = h_ref[...] + jnp.dot(a_ref[...], wo_ref[...], preferred_element_type=F32)
    h = _mlp(h, gm_ref, w1_ref, w2_ref)
    o_ref[...] = _rms_unit(h) * gf_ref[...]


def _post_call(a2, h2, wo, gm, w1, w2, gf):
    n, d = h2.shape
    tm = min(WIDE_TILE, n)
    row = lambda i: (i, 0)
    return pl.pallas_call(
        _post_kernel,
        grid=(n // tm,),
        in_specs=[pl.BlockSpec((tm, d), row), pl.BlockSpec((tm, d), row),
                  _resident((d, d)), _resident((1, d)), _spec_of(w1), _spec_of(w2),
                  _resident((1, d))],
        out_specs=pl.BlockSpec((tm, d), row),
        out_shape=jax.ShapeDtypeStruct((n, d), F32),
        compiler_params=_params(1),
        name="post",
    )(a2, h2, wo, gm, _array_of(w1), _array_of(w2), gf)


def _row(v):
    return v.reshape(1, -1).astype(F32)


def kernel(x_prompt, x_sample, state_conv, cache_k, cache_v, page_table, norm_mix_g, norm_mlp_g, conv_pw1_w, conv_pw1_b, conv_dw_w, conv_dw_b, conv_ln_g, conv_ln_b, conv_pw2_w, conv_pw2_b, kv_norm_g, w_k, w_v, w_q, lambda_q1, lambda_k1, lambda_q2, lambda_k2, subln_g, w_o, rel_bias_table, mlp_w1, mlp_w2, final_norm_g):
    bsz, seq, d = x_prompt.shape
    db, dseq, _ = x_sample.shape
    n_pool, page = cache_k.shape[0], cache_k.shape[1]
    n_pages = page_table.shape[1]
    past_len = n_pages * page
    hist = CONV_WIDTH - 1

    pw1 = conv_pw1_w[0].astype(BF16)
    pw1b = _row(conv_pw1_b[0])
    dw = jnp.pad(conv_dw_w[0].astype(F32), ((0, HIST_ROWS - CONV_WIDTH), (0, 0)))
    dwb, ln_g, ln_b = _row(conv_dw_b[0]), _row(conv_ln_g[0]), _row(conv_ln_b[0])
    pw2 = conv_pw2_w[0].astype(BF16)
    pw2b = _row(conv_pw2_b[0])
    w1s, w2s = mlp_w1.astype(BF16), mlp_w2.astype(BF16)
    w1 = [_Layer(w1s, l) for l in range(mlp_w1.shape[0])]
    w2 = [_Layer(w2s, l) for l in range(mlp_w2.shape[0])]
    wk, wv, wq, wo = w_k.astype(BF16), w_v.astype(BF16), w_q[0].astype(BF16), w_o[0].astype(BF16)
    lamv = jnp.pad(jnp.stack([lambda_q1[0], lambda_k1[0], lambda_q2[0], lambda_k2[0]]).astype(F32),
                   ((0, SUBLANES - 4), (0, LANES - HEAD_DIM)))
    subg = _row(subln_g[0])
    rel = rel_bias_table.astype(F32)

    g_mix0, g_mix1, g_mlp0, g_kv = _row(norm_mix_g[0]), _row(norm_mix_g[1]), _row(norm_mlp_g[0]), _row(kv_norm_g)

    def back(a2, h2):
        return _post_call(a2, h2, wo, _row(norm_mlp_g[1]), w1[1], w2[1], _row(final_norm_g))

    x_p = x_prompt.reshape(bsz * seq, d)
    g_p = _glu_call(x_p, g_mix0, pw1, pw1b)
    dw_rows = jnp.broadcast_to(dw[:, None, :], (HIST_ROWS, SUBLANES, d))
    c_p = _conv_long_call(g_p.reshape(bsz, seq, d), dw_rows, dwb, ln_g, ln_b).reshape(bsz * seq, d)
    h_p = _pw2_mlp_call(c_p, x_p, pw2, pw2b, g_mlp0, w1[0], w2[0])
    k_p, v_p, q_p, kb_p, va_p = _qkv_call(h_p, g_kv, g_mix1, wk, wv, wq)
    assert min(ATTN_BLOCK, seq) % MAX_DISTANCE == 0
    r = jnp.arange(MAX_DISTANCE, dtype=jnp.int32)[:, None]
    c = jnp.arange(MAX_DISTANCE, dtype=jnp.int32)[None, :]
    bias_p = jnp.stack([_shifted_bias(rel, r - c, r >= c),
                        _shifted_bias(rel, MAX_DISTANCE + r - c, jnp.full((MAX_DISTANCE,) * 2, True))], axis=1)
    a_p = _attn_call(q_p.reshape(bsz, seq, d), kb_p.reshape(bsz, seq, d),
                     va_p.reshape(bsz, seq, 2 * d), bias_p, lamv, subg)
    y_prompt = back(a_p.reshape(bsz * seq, d), h_p).reshape(bsz, seq, d)
    conv_state_prompt = g_p.reshape(bsz, seq, d)[:, seq - hist:][None]

    st_s = jnp.pad(state_conv[0].astype(F32), ((0, 0), (HIST_ROWS - hist, 0), (0, 0)))
    cs_s, h_s, k_s, v_s, q_s = _short_front_call(
        x_sample.reshape(db * dseq, d), st_s, g_mix0, pw1, pw1b, dw, dwb, ln_g, ln_b,
        pw2, pw2b, g_mlp0, w1[0], w2[0], g_kv, g_mix1, wk, wv, wq)
    prow = page * N_HEADS
    n_cols = 2 * N_HEADS * dseq
    assert n_cols == LANES and dseq <= page
    col_q = jnp.tile(jnp.arange(dseq, dtype=jnp.int32), 2 * N_HEADS)[None, :]
    key = jnp.arange(page, dtype=jnp.int32)[:, None]
    pick = lambda b4: jnp.concatenate([b4[h, :, 2 * h * dseq:2 * (h + 1) * dseq] for h in range(N_HEADS)], axis=1)
    blast = pick(_shifted_bias(rel, page + col_q - key, jnp.full((page, n_cols), True)))
    bnew = pick(_shifted_bias(rel, col_q - key, (key <= col_q) & (key < dseq)))
    a_s = _decode_call(page_table.astype(jnp.int32), q_s.reshape(db, dseq, d),
                       cache_k.reshape(n_pool, prow, V_HEAD_DIM), cache_v.reshape(n_pool, prow, V_HEAD_DIM),
                       k_s.reshape(db, dseq, d), v_s.reshape(db, dseq, d), blast, bnew, lamv, subg)
    y_sample = back(a_s.reshape(db * dseq, d), h_s).reshape(db, dseq, d)
    conv_state_sample = cs_s[None]

    kv_shape = (N_HEADS, V_HEAD_DIM)
    return (y_prompt, y_sample, conv_state_prompt, conv_state_sample,
            k_p.reshape(bsz, seq, *kv_shape), v_p.reshape(bsz, seq, *kv_shape),
            k_s.reshape(db, dseq, *kv_shape), v_s.reshape(db, dseq, *kv_shape))
```
